```python
import math
import jax, jax.numpy as jnp
from jax import lax
import numpy as np

D_MODEL = 1024
BATCH = 1
SEQ = 16384
DEPTH = 1

GLA_HEADS = 4
GLA_DK = 64
GLA_DV = 128
GLA_LOWRANK = 16
GLA_TAU = 16.0
GLA_CHUNK = 64
MOBA_HEADS = 8
MOBA_HD = 64
MOBA_BLOCK = 256
MOBA_TOPK = 3
Q_BLOCK = 128
REL_BUCKETS = 32
REL_MAX_DIST = 4096
D_FF = 2816
D_PLE = 256
EPS = 1e-6
NEG = -1e30

GLA_QK = GLA_HEADS * GLA_DK
GLA_V = GLA_HEADS * GLA_DV
MOBA_W = MOBA_HEADS * MOBA_HD
MIX_WIDTH = GLA_V + MOBA_W
IN_SIZES = (GLA_QK, GLA_QK, GLA_V, GLA_V, GLA_LOWRANK, MOBA_W, MOBA_W, MOBA_W)
IN_WIDTH = 2 * GLA_QK + 2 * GLA_V + GLA_LOWRANK + 3 * MOBA_W

kernel_name = "hymba_gla_moba_macaron_t5bias_ple"


def rmsnorm(x, g):
    xf = x.astype(jnp.float32)
    y = xf * lax.rsqrt(jnp.mean(xf * xf, axis=-1, keepdims=True) + EPS)
    return (y * g.astype(jnp.float32)).astype(x.dtype)


def swiglu(x, w_gu, w_down):
    gate, up = jnp.split(x @ w_gu, 2, axis=-1)
    return (jax.nn.silu(gate) * up) @ w_down


def rel_bucket(dist):
    n = jnp.maximum(dist, 0)
    max_exact = REL_BUCKETS // 2
    nf = jnp.maximum(n, 1).astype(jnp.float32)
    large = max_exact + (jnp.log(nf / max_exact) / math.log(REL_MAX_DIST / max_exact)
                         * (REL_BUCKETS - max_exact)).astype(jnp.int32)
    large = jnp.minimum(large, REL_BUCKETS - 1)
    return jnp.where(n < max_exact, n, large)


def gla_chunked(q, k, v, log_a):
    B, H, T, dk = q.shape
    dv = v.shape[-1]
    C = GLA_CHUNK
    n = T // C

    def to_chunks(t):
        return t.astype(jnp.float32).reshape(B, H, n, C, t.shape[-1]).transpose(2, 0, 1, 3, 4)

    qc, kc, vc, ac = to_chunks(q), to_chunks(k), to_chunks(v), to_chunks(log_a)
    causal = jnp.tril(jnp.ones((C, C), dtype=bool))[:, :, None]

    def step(S, inp):
        qi, ki, vi, ai = inp
        b = jnp.cumsum(ai, axis=-2)
        b_last = b[..., -1:, :]
        o_inter = jnp.einsum('bhcd,bhde->bhce', qi * jnp.exp(b), S)
        diff = b[..., :, None, :] - b[..., None, :, :]
        decay = jnp.exp(jnp.where(causal, diff, -jnp.inf))
        attn = jnp.einsum('bhid,bhjd,bhijd->bhij', qi, ki, decay)
        o_intra = jnp.einsum('bhij,bhje->bhie', attn, vi)
        S_new = (jnp.exp(b_last)[..., 0, :, None] * S
                 + jnp.einsum('bhcd,bhce->bhde', ki * jnp.exp(b_last - b), vi))
        return S_new, o_inter + o_intra

    S0 = jnp.zeros((B, H, dk, dv), jnp.float32)
    _, o = lax.scan(step, S0, (qc, kc, vc, ac))
    return o.transpose(1, 2, 0, 3, 4).reshape(B, H, T, dv).astype(v.dtype)


def moba_attention(q, k, v, rel_bias):
    B, H, T, hd = q.shape
    nb = -(-T // MOBA_BLOCK)
    pad = nb * MOBA_BLOCK - T
    kb = jnp.pad(k, ((0, 0), (0, 0), (0, pad), (0, 0))).reshape(B, H, nb, MOBA_BLOCK, hd)
    vb = jnp.pad(v, ((0, 0), (0, 0), (0, pad), (0, 0))).reshape(B, H, nb, MOBA_BLOCK, hd)
    k_mean = jnp.mean(kb.astype(jnp.float32), axis=3)
    topk = min(MOBA_TOPK, nb)
    scale = hd ** -0.5
    bias_tab = rel_bias.astype(jnp.float32).T
    bi = jnp.arange(B)[:, None, None, None]
    hi = jnp.arange(H)[None, :, None, None]
    blk_off = jnp.arange(MOBA_BLOCK)
    nq = T // Q_BLOCK
    qs = q.reshape(B, H, nq, Q_BLOCK, hd).transpose(2, 0, 1, 3, 4)

    def per_block(args):
        qi, iq = args
        qpos = iq * Q_BLOCK + jnp.arange(Q_BLOCK)
        own = (iq * Q_BLOCK) // MOBA_BLOCK
        gate = jnp.einsum('bhqd,bhnd->bhqn', qi.astype(jnp.float32), k_mean)
        gate = jnp.where(jnp.arange(nb) < own, gate, -jnp.inf)
        _, sel = lax.top_k(gate, topk)
        valid = sel < own
        k_sel = kb[bi, hi, sel]
        v_sel = vb[bi, hi, sel]
        s_past = jnp.einsum('bhqd,bhqnsd->bhqns', qi, k_sel,
                            preferred_element_type=jnp.float32) * scale
        kpos = sel[..., None] * MOBA_BLOCK + blk_off
        s_past = s_past + bias_tab[hi[..., None], rel_bucket(qpos[:, None, None] - kpos)]
        s_past = jnp.where(valid[..., None], s_past, NEG).reshape(B, H, Q_BLOCK, topk * MOBA_BLOCK)
        k_own = lax.dynamic_index_in_dim(kb, own, axis=2, keepdims=False)
        v_own = lax.dynamic_index_in_dim(vb, own, axis=2, keepdims=False)
        dist_own = qpos[:, None] - (own * MOBA_BLOCK + blk_off)[None, :]
        s_own = (jnp.einsum('bhqd,bhsd->bhqs', qi, k_own, preferred_element_type=jnp.float32) * scale
                 + bias_tab[:, rel_bucket(dist_own)])
        s_own = jnp.where(dist_own >= 0, s_own, NEG)
        probs = jax.nn.softmax(jnp.concatenate([s_past, s_own], axis=-1), axis=-1)
        p_past = probs[..., :topk * MOBA_BLOCK].reshape(B, H, Q_BLOCK, topk, MOBA_BLOCK).astype(v.dtype)
        p_own = probs[..., topk * MOBA_BLOCK:].astype(v.dtype)
        o = (jnp.einsum('bhqns,bhqnsd->bhqd', p_past, v_sel, preferred_element_type=jnp.float32)
             + jnp.einsum('bhqs,bhsd->bhqd', p_own, v_own, preferred_element_type=jnp.float32))
        return o.astype(q.dtype)

    o = lax.map(per_block, (qs, jnp.arange(nq)))
    return o.transpose(1, 2, 0, 3, 4).reshape(B, H, T, hd)


def hybrid_mixer(xn, w_in, w_gk_up, b_gk, gla_out_norm, w_out, rel_bias):
    B, T, _ = xn.shape
    split_points = np.cumsum(IN_SIZES)[:-1].tolist()
    q_g, k_g, v_g, g_g, gk_lr, q_m, k_m, v_m = jnp.split(xn @ w_in, split_points, axis=-1)

    def heads(t, nh):
        return t.reshape(B, T, nh, -1).transpose(0, 2, 1, 3)

    log_a = jax.nn.log_sigmoid((gk_lr @ w_gk_up + b_gk).astype(jnp.float32)) / GLA_TAU
    o_g = gla_chunked(heads(q_g, GLA_HEADS) * (GLA_DK ** -0.5), heads(k_g, GLA_HEADS),
                      heads(v_g, GLA_HEADS), heads(log_a, GLA_HEADS))
    o_g = rmsnorm(o_g, gla_out_norm)
    o_g = o_g.transpose(0, 2, 1, 3).reshape(B, T, GLA_V) * jax.nn.silu(g_g)
    o_m = moba_attention(heads(q_m, MOBA_HEADS), heads(k_m, MOBA_HEADS), heads(v_m, MOBA_HEADS), rel_bias)
    o_m = o_m.transpose(0, 2, 1, 3).reshape(B, T, MOBA_W)
    return jnp.concatenate([o_g, o_m], axis=-1) @ w_out


def setup_inputs(seed: int = 0) -> dict:
    key = jax.random.key(seed)
    ks = jax.random.split(key, 24)
    f32 = jnp.float32

    def w(k, shape, fan_in):
        return jax.random.normal(k, shape, f32) * (fan_in ** -0.5)

    def gain(k, shape):
        return 1.0 + 0.02 * jax.random.normal(k, shape, f32)

    L = DEPTH
    return {
        "x": jax.random.normal(ks[0], (BATCH, SEQ, D_MODEL), f32),
        "p": jax.random.normal(ks[1], (DEPTH, BATCH, SEQ, D_PLE), f32),
        "ffn1_norm": gain(ks[2], (L, D_MODEL)),
        "ffn1_w_gu": w(ks[3], (L, D_MODEL, 2 * D_FF), D_MODEL),
        "ffn1_w_down": w(ks[4], (L, D_FF, D_MODEL), D_FF),
        "mix_norm": gain(ks[5], (L, D_MODEL)),
        "w_in": w(ks[6], (L, D_MODEL, IN_WIDTH), D_MODEL),
        "w_gk_up": w(ks[7], (L, GLA_LOWRANK, GLA_QK), GLA_LOWRANK),
        "b_gk": 0.01 * jax.random.normal(ks[8], (L, GLA_QK), f32),
        "gla_out_norm": gain(ks[9], (L, GLA_DV)),
        "w_out": w(ks[10], (L, MIX_WIDTH, D_MODEL), MIX_WIDTH),
        "ffn2_norm": gain(ks[11], (L, D_MODEL)),
        "ffn2_w_gu": w(ks[12], (L, D_MODEL, 2 * D_FF), D_MODEL),
        "ffn2_w_down": w(ks[13], (L, D_FF, D_MODEL), D_FF),
        "ple_norm": gain(ks[14], (L, D_MODEL)),
        "w_ple_gate": w(ks[15], (L, D_MODEL, D_MODEL), D_MODEL),
        "w_ple_proj": w(ks[16], (L, D_PLE, D_MODEL), D_PLE),
        "rel_bias": 0.2 * jax.random.normal(ks[17], (REL_BUCKETS, MOBA_HEADS), f32),
        "final_norm": gain(ks[18], (D_MODEL,)),
    }


def reference(x, p, ffn1_norm, ffn1_w_gu, ffn1_w_down, mix_norm, w_in, w_gk_up, b_gk,
              gla_out_norm, w_out, ffn2_norm, ffn2_w_gu, ffn2_w_down, ple_norm,
              w_ple_gate, w_ple_proj, rel_bias, final_norm):
    h = x
    for i in range(DEPTH):
        h = h + 0.5 * swiglu(rmsnorm(h, ffn1_norm[i]), ffn1_w_gu[i], ffn1_w_down[i])
        h = h + hybrid_mixer(rmsnorm(h, mix_norm[i]), w_in[i], w_gk_up[i], b_gk[i],
                             gla_out_norm[i], w_out[i], rel_bias)
        h = h + 0.5 * swiglu(rmsnorm(h, ffn2_norm[i]), ffn2_w_gu[i], ffn2_w_down[i])
        ple_gate = jax.nn.sigmoid(rmsnorm(h, ple_norm[i]) @ w_ple_gate[i])
        h = h + ple_gate * (p[i] @ w_ple_proj[i])
    return rmsnorm(h, final_norm)
```

```python
import functools
import math

import jax
import jax.numpy as jnp
from jax import lax
from jax.experimental import pallas as pl
from jax.experimental.pallas import tpu as pltpu

F32 = jnp.float32
BF16 = jnp.bfloat16

GLA_HEADS = 4
GLA_DK = 64
GLA_DV = 128
GLA_LOWRANK = 16
GLA_TAU = 16.0
GLA_CHUNK = 64
MOBA_HEADS = 8
MOBA_HD = 64
MOBA_BLOCK = 256
MOBA_TOPK = 3
REL_BUCKETS = 32
REL_MAX_DIST = 4096
EPS = 1e-6
NEG = -1e30

GLA_QK = GLA_HEADS * GLA_DK
GLA_V = GLA_HEADS * GLA_DV
MOBA_W = MOBA_HEADS * MOBA_HD

LANES = 128
VMEM_LIMIT = 56 * 1024 * 1024

ROW_TILE = 512
GLA_TILE = 256
PREP_BLOCKS = 8
BIAS_TILES = 14


def _rms(x, g):
    return x * lax.rsqrt(jnp.mean(x * x, axis=-1, keepdims=True) + EPS) * g


def _ff_chunks(d_ff):
    step = 1024
    return [(s, min(step, d_ff - s)) for s in range(0, d_ff, step)]


def _swiglu(xn, wgu_ref, wd_ref):
    d_ff = wd_ref.shape[0]
    acc = None
    for start, size in _ff_chunks(d_ff):
        gate = jnp.dot(xn, wgu_ref[:, start:start + size], preferred_element_type=F32)
        up = jnp.dot(xn, wgu_ref[:, d_ff + start:d_ff + start + size], preferred_element_type=F32)
        act = (gate * jax.nn.sigmoid(gate) * up).astype(BF16)
        part = jnp.dot(act, wd_ref[start:start + size, :], preferred_element_type=F32)
        acc = part if acc is None else acc + part
    return acc


def _const_spec(shape):
    return pl.BlockSpec(shape, lambda *_: (0,) * len(shape))


def _ffn_in_kernel(x_ref, n1_ref, wgu_ref, wd_ref, n2_ref, wa_ref, wlr_ref, wm_ref, wgk_ref, bgk_ref,
                   h_ref, qg_ref, kg_ref, vg_ref, gg_ref, la_ref, qm_ref, km_ref, vm_ref):
    x = x_ref[...]
    h = x + 0.5 * _swiglu(_rms(x, n1_ref[...]).astype(BF16), wgu_ref, wd_ref)
    h_ref[...] = h
    n = _rms(h, n2_ref[...]).astype(BF16)
    ua = jnp.dot(n, wa_ref[...], preferred_element_type=F32)
    qg_ref[...] = ua[:, 0:GLA_QK]
    kg_ref[...] = ua[:, GLA_QK:2 * GLA_QK]
    vg_ref[...] = ua[:, 2 * GLA_QK:2 * GLA_QK + GLA_V]
    gg_ref[...] = ua[:, 2 * GLA_QK + GLA_V:]
    lr = jnp.dot(n, wlr_ref[...], preferred_element_type=F32)
    z = jnp.dot(lr.astype(BF16), wgk_ref[...], preferred_element_type=F32) + bgk_ref[...]
    la_ref[...] = (jnp.minimum(z, 0.0) - jnp.log1p(jnp.exp(-jnp.abs(z)))) * (1.0 / GLA_TAU)
    um = jnp.dot(n, wm_ref[...], preferred_element_type=F32)
    qm_ref[...] = um[:, 0:MOBA_W]
    km_ref[...] = um[:, MOBA_W:2 * MOBA_W]
    vm_ref[...] = um[:, 2 * MOBA_W:]


def _ffn_in(x, n1, wgu, wd, n2, wa, wlr, wm, wgk, bgk):
    t, d = x.shape
    tm = min(ROW_TILE, t)
    row = lambda w: pl.BlockSpec((tm, w), lambda i: (i, 0))
    outs = [(d, F32), (GLA_QK, F32), (GLA_QK, F32), (GLA_V, F32), (GLA_V, F32), (GLA_QK, F32),
            (MOBA_W, F32), (MOBA_W, F32), (MOBA_W, F32)]
    return pl.pallas_call(
        _ffn_in_kernel,
        grid=(t // tm,),
        in_specs=[row(d)] + [_const_spec(a.shape) for a in (n1, wgu, wd, n2, wa, wlr, wm, wgk, bgk)],
        out_specs=[row(w) for w, _ in outs],
        out_shape=[jax.ShapeDtypeStruct((t, w), dt) for w, dt in outs],
        compiler_params=pltpu.CompilerParams(dimension_semantics=("parallel",), vmem_limit_bytes=VMEM_LIMIT),
        name="ffn_in",
    )(x, n1, wgu, wd, n2, wa, wlr, wm, wgk, bgk)


def _gla_kernel(q_ref, k_ref, la_ref, v_ref, g_ref, gn_ref, o_ref, st_ref):
    @pl.when(pl.program_id(0) == 0)
    def _():
        st_ref[...] = jnp.zeros_like(st_ref)

    c = GLA_CHUNK
    ri = lax.broadcasted_iota(jnp.int32, (c, c), 0)
    ci = lax.broadcasted_iota(jnp.int32, (c, c), 1)
    causal = ri >= ci
    tril = jnp.where(causal, 1.0, 0.0).astype(BF16)
    gn = gn_ref[...]
    for ch in range(q_ref.shape[0] // c):
        rows = slice(ch * c, (ch + 1) * c)
        la = la_ref[rows, :]
        la_hi = la.astype(BF16)
        la_lo = (la - la_hi.astype(F32)).astype(BF16)
        b = jnp.dot(tril, la_hi, preferred_element_type=F32) + jnp.dot(tril, la_lo, preferred_element_type=F32)
        b_last = b[c - 1:c, :]
        q = q_ref[rows, :] * (GLA_DK ** -0.5)
        k = k_ref[rows, :]
        qe = q * jnp.exp(b)
        ki = k * jnp.exp(-b)
        ke = k * jnp.exp(b_last - b)
        a_last = jnp.exp(b_last)
        for h in range(GLA_HEADS):
            ks = slice(h * GLA_DK, (h + 1) * GLA_DK)
            vs = slice(h * GLA_DV, (h + 1) * GLA_DV)
            v = v_ref[rows, vs].astype(BF16)
            st = st_ref[h]
            qeh = qe[:, ks].astype(BF16)
            attn = lax.dot_general(qeh, ki[:, ks].astype(BF16), (((1,), (1,)), ((), ())),
                                   preferred_element_type=F32)
            attn = jnp.where(causal, attn, 0.0).astype(BF16)
            o = jnp.dot(attn, v, preferred_element_type=F32)
            o = o + lax.dot_general(qeh, st.astype(BF16), (((1,), (1,)), ((), ())), preferred_element_type=F32)
            st_ref[h] = st * a_last[:, ks] + lax.dot_general(v, ke[:, ks].astype(BF16), (((0,), (0,)), ((), ())),
                                                             preferred_element_type=F32)
            g = g_ref[rows, vs]
            o_ref[rows, vs] = _rms(o, gn) * (g * jax.nn.sigmoid(g))


def _gla(qg, kg, la, vg, gg, gn):
    t = qg.shape[0]
    tb = min(GLA_TILE, t)
    row = lambda w: pl.BlockSpec((tb, w), lambda i: (i, 0))
    return pl.pallas_call(
        _gla_kernel,
        grid=(t // tb,),
        in_specs=[row(GLA_QK), row(GLA_QK), row(GLA_QK), row(GLA_V), row(GLA_V), _const_spec(gn.shape)],
        out_specs=row(GLA_V),
        out_shape=jax.ShapeDtypeStruct((t, GLA_V), F32),
        scratch_shapes=[pltpu.VMEM((GLA_HEADS, GLA_DV, GLA_DK), F32)],
        compiler_params=pltpu.CompilerParams(dimension_semantics=("arbitrary",), vmem_limit_bytes=VMEM_LIMIT),
        name="gla",
    )(qg, kg, la, vg, gg, gn)


def _moba_bias_kernel(tab_ref, o_ref):
    h = pl.program_id(0)
    delta = pl.program_id(1)
    blk = MOBA_BLOCK
    kk = lax.broadcasted_iota(jnp.int32, (blk, blk), 0)
    qq = lax.broadcasted_iota(jnp.int32, (blk, blk), 1)
    dist = delta * blk + qq - kk
    n = jnp.maximum(dist, 0)
    max_exact = REL_BUCKETS // 2
    nf = jnp.maximum(n, 1).astype(F32)
    large = max_exact + (jnp.log(nf / max_exact) / math.log(REL_MAX_DIST / max_exact)
                         * (REL_BUCKETS - max_exact)).astype(jnp.int32)
    large = jnp.minimum(large, REL_BUCKETS - 1)
    bucket = jnp.where(n < max_exact, n, large)
    bias = jnp.zeros((blk, blk), F32)
    for b in range(REL_BUCKETS):
        bias = jnp.where(bucket == b, tab_ref[b * MOBA_HEADS + h], bias)
    o_ref[0, 0] = jnp.where(dist >= 0, bias, NEG)


def _moba_bias(rel_bias):
    blk = MOBA_BLOCK
    return pl.pallas_call(
        _moba_bias_kernel,
        grid=(MOBA_HEADS, BIAS_TILES),
        in_specs=[pl.BlockSpec(memory_space=pltpu.SMEM)],
        out_specs=pl.BlockSpec((1, 1, blk, blk), lambda h, d: (h, d, 0, 0)),
        out_shape=jax.ShapeDtypeStruct((MOBA_HEADS, BIAS_TILES, blk, blk), F32),
        compiler_params=pltpu.CompilerParams(dimension_semantics=("parallel", "parallel")),
        name="moba_bias",
    )(rel_bias.reshape(-1))


def _moba_prep_kernel(q_ref, k_ref, v_ref, qt_ref, ka_ref, vt_ref, km_ref):
    t = pl.program_id(1)
    blk = MOBA_BLOCK
    rows = PREP_BLOCKS * blk
    hd = MOBA_HD

    @pl.when(t == 0)
    def _():
        km_ref[...] = jnp.zeros_like(km_ref)

    lane = lax.broadcasted_iota(jnp.int32, (rows, LANES), 1)
    own = t * PREP_BLOCKS + lax.broadcasted_iota(jnp.int32, (rows, LANES), 0) // blk
    head_lanes = lane < hd
    lane8 = lax.broadcasted_iota(jnp.int32, (PREP_BLOCKS, LANES), 1)
    for a in range(2):
        q = q_ref[...]
        k = k_ref[...]
        v = v_ref[...]
        if a == 1:
            q = pltpu.roll(q, hd, 1)
            k = pltpu.roll(k, hd, 1)
            v = pltpu.roll(v, hd, 1)
        kmean = jnp.mean(k.reshape(PREP_BLOCKS, blk, LANES), axis=1)
        km_ref[a, pl.ds(pl.multiple_of(hd + t * PREP_BLOCKS, PREP_BLOCKS), PREP_BLOCKS), :] = (
            jnp.where(lane8 < hd, kmean, 0.0))
        gate = lax.dot_general(jnp.where(head_lanes, q, 0.0), km_ref[a], (((1,), (1,)), ((), ())),
                               precision=lax.Precision.HIGHEST, preferred_element_type=F32)
        gate = jnp.where((lane >= hd) & (lane - hd < own), gate, -jnp.inf)
        chosen = lane - hd == own
        for _ in range(MOBA_TOPK):
            top = jnp.max(gate, axis=1, keepdims=True)
            first = jnp.min(jnp.where(gate == top, lane, 2 * LANES), axis=1, keepdims=True)
            pick = (lane == first) & (top > -jnp.inf)
            chosen = chosen | pick
            gate = jnp.where(lane == first, -jnp.inf, gate)
        q_aug = jnp.where(head_lanes, q * (hd ** -0.5), jnp.where(chosen, 0.0, -1.0))
        k_aug = jnp.where(head_lanes, k, jnp.where(lane - hd == own, -NEG, 0.0))
        v_aug = jnp.where(head_lanes, v, jnp.where(lane == hd, 1.0, 0.0))
        q_t = q_aug.T.astype(BF16)
        v_t = v_aug.T.astype(BF16)
        k_aug = k_aug.astype(BF16)
        for b in range(PREP_BLOCKS):
            qt_ref[a, b] = q_t[:, b * blk:(b + 1) * blk]
            vt_ref[a, b] = v_t[:, b * blk:(b + 1) * blk]
            ka_ref[a, b] = k_aug[b * blk:(b + 1) * blk, :]


def _moba_prep(qm, km, vm):
    t = qm.shape[0]
    blk = MOBA_BLOCK
    nb = t // blk
    rows = PREP_BLOCKS * blk
    pair = pl.BlockSpec((rows, LANES), lambda p, i: (i, p))
    t_spec = pl.BlockSpec((2, PREP_BLOCKS, LANES, blk), lambda p, i: (p, i, 0, 0))
    k_spec = pl.BlockSpec((2, PREP_BLOCKS, blk, LANES), lambda p, i: (p, i, 0, 0))
    return pl.pallas_call(
        _moba_prep_kernel,
        grid=(MOBA_HEADS // 2, nb // PREP_BLOCKS),
        in_specs=[pair, pair, pair],
        out_specs=[t_spec, k_spec, t_spec],
        out_shape=[jax.ShapeDtypeStruct((MOBA_HEADS, nb, LANES, blk), BF16),
                   jax.ShapeDtypeStruct((MOBA_HEADS, nb, blk, LANES), BF16),
                   jax.ShapeDtypeStruct((MOBA_HEADS, nb, LANES, blk), BF16)],
        scratch_shapes=[pltpu.VMEM((2, LANES, LANES), F32)],
        compiler_params=pltpu.CompilerParams(dimension_semantics=("parallel", "arbitrary"),
                                             vmem_limit_bytes=VMEM_LIMIT),
        name="moba_prep",
    )(qm, km, vm)


def _moba_attn_kernel(qt_ref, ka_ref, vt_ref, bias_ref, o_ref):
    qi = pl.program_id(1)
    qt = qt_ref[0, 0]

    def step(j, carry):
        m, acc = carry
        s = jnp.dot(ka_ref[0, j], qt, preferred_element_type=F32)
        s = s + bias_ref[0, jnp.minimum(qi - j, BIAS_TILES - 1)]
        m_new = jnp.maximum(m, jnp.max(s, axis=0, keepdims=True))
        p = jnp.exp(s - m_new).astype(BF16)
        acc = acc * jnp.exp(m - m_new) + jnp.dot(vt_ref[0, j], p, preferred_element_type=F32)
        return m_new, acc

    init = (jnp.full((1, MOBA_BLOCK), -jnp.inf, F32), jnp.zeros((LANES, MOBA_BLOCK), F32))
    m, acc = lax.fori_loop(0, qi, step, step(qi, init))
    o_ref[0] = acc[0:MOBA_HD] / acc[MOBA_HD:MOBA_HD + 1]


def _moba_attn(qt, ka, vt, bias):
    nb = qt.shape[1]
    blk = MOBA_BLOCK
    return pl.pallas_call(
        _moba_attn_kernel,
        grid=(MOBA_HEADS, nb),
        in_specs=[pl.BlockSpec((1, 1, LANES, blk), lambda h, i: (h, i, 0, 0)),
                  pl.BlockSpec((1, nb, blk, LANES), lambda h, i: (h, 0, 0, 0)),
                  pl.BlockSpec((1, nb, LANES, blk), lambda h, i: (h, 0, 0, 0)),
                  pl.BlockSpec((1, BIAS_TILES, blk, blk), lambda h, i: (h, 0, 0, 0))],
        out_specs=pl.BlockSpec((1, MOBA_HD, blk), lambda h, i: (h, 0, i)),
        out_shape=jax.ShapeDtypeStruct((MOBA_HEADS, MOBA_HD, nb * blk), F32),
        compiler_params=pltpu.CompilerParams(dimension_semantics=("parallel", "parallel"),
                                             vmem_limit_bytes=VMEM_LIMIT),
        name="moba_attn",
    )(qt, ka, vt, bias)


def _out_ffn_kernel(final, h_ref, og_ref, omt_ref, p_ref, wog_ref, wom_ref, n3_ref, wgu_ref, wd_ref,
                    n4_ref, wpg_ref, wpp_ref, nf_ref, o_ref):
    mix = jnp.dot(og_ref[...].astype(BF16), wog_ref[...], preferred_element_type=F32)
    mix = mix + jnp.dot(omt_ref[...].T.astype(BF16), wom_ref[...], preferred_element_type=F32)
    h = h_ref[...] + mix
    h = h + 0.5 * _swiglu(_rms(h, n3_ref[...]).astype(BF16), wgu_ref, wd_ref)
    gate = jax.nn.sigmoid(jnp.dot(_rms(h, n4_ref[...]).astype(BF16), wpg_ref[...], preferred_element_type=F32))
    h = h + gate * jnp.dot(p_ref[...].astype(BF16), wpp_ref[...], preferred_element_type=F32)
    o_ref[...] = _rms(h, nf_ref[...]) if final else h


def _out_ffn(final, h, og, omt, p, wog, wom, n3, wgu, wd, n4, wpg, wpp, nf):
    t, d = h.shape
    tm = min(ROW_TILE, t)
    row = lambda w: pl.BlockSpec((tm, w), lambda i: (i, 0))
    consts = (wog, wom, n3, wgu, wd, n4, wpg, wpp, nf)
    return pl.pallas_call(
        functools.partial(_out_ffn_kernel, final),
        grid=(t // tm,),
        in_specs=[row(d), row(GLA_V), pl.BlockSpec((MOBA_W, tm), lambda i: (0, i)), row(p.shape[1])]
        + [_const_spec(a.shape) for a in consts],
        out_specs=row(d),
        out_shape=jax.ShapeDtypeStruct((t, d), F32),
        compiler_params=pltpu.CompilerParams(dimension_semantics=("parallel",), vmem_limit_bytes=VMEM_LIMIT),
        name="out_ffn",
    )(h, og, omt, p, *consts)


def kernel(x, p, ffn1_norm, ffn1_w_gu, ffn1_w_down, mix_norm, w_in, w_gk_up, b_gk, gla_out_norm, w_out,
           ffn2_norm, ffn2_w_gu, ffn2_w_down, ple_norm, w_ple_gate, w_ple_proj, rel_bias, final_norm):
    batch, t, d = x.shape
    depth = p.shape[0]
    assert batch == 1 and t % (PREP_BLOCKS * MOBA_BLOCK) == 0
    row = lambda a: a.reshape(1, -1)
    bias = _moba_bias(rel_bias)
    lr0 = 2 * GLA_QK + 2 * GLA_V
    h = x[0]
    for i in range(depth):
        wi = w_in[i]
        wa = wi[:, :lr0].astype(BF16)
        wlr = jnp.pad(wi[:, lr0:lr0 + GLA_LOWRANK], ((0, 0), (0, LANES - GLA_LOWRANK))).astype(BF16)
        wm = wi[:, lr0 + GLA_LOWRANK:].astype(BF16)
        wgk = jnp.pad(w_gk_up[i], ((0, LANES - GLA_LOWRANK), (0, 0))).astype(BF16)
        h1, qg, kg, vg, gg, la, qm, km, vm = _ffn_in(
            h, row(ffn1_norm[i]), ffn1_w_gu[i].astype(BF16), ffn1_w_down[i].astype(BF16), row(mix_norm[i]),
            wa, wlr, wm, wgk, row(b_gk[i]))
        og = _gla(qg, kg, la, vg, gg, row(gla_out_norm[i]))
        qt, ka, vt = _moba_prep(qm, km, vm)
        omt = _moba_attn(qt, ka, vt, bias).reshape(MOBA_W, t)
        wo = w_out[i].astype(BF16)
        h = _out_ffn(i == depth - 1, h1, og, omt, p[i, 0], wo[:GLA_V], wo[GLA_V:], row(ffn2_norm[i]),
                     ffn2_w_gu[i].astype(BF16), ffn2_w_down[i].astype(BF16), row(ple_norm[i]),
                     w_ple_gate[i].astype(BF16), w_ple_proj[i].astype(BF16), row(final_norm))
    return h[None]
```

```python
import functools
import math

import jax
import jax.numpy as jnp
from jax import lax
from jax.experimental import pallas as pl
from jax.experimental.pallas import tpu as pltpu

F32 = jnp.float32
BF16 = jnp.bfloat16

GLA_HEADS = 4
GLA_DK = 64
GLA_DV = 128
GLA_LOWRANK = 16
GLA_TAU = 16.0
GLA_CHUNK = 64
MOBA_HEADS = 8
MOBA_HD = 64
MOBA_BLOCK = 256
MOBA_TOPK = 3
REL_BUCKETS = 32
REL_MAX_DIST = 4096
EPS = 1e-6
NEG = -1e30

GLA_QK = GLA_HEADS * GLA_DK
GLA_V = GLA_HEADS * GLA_DV
MOBA_W = MOBA_HEADS * MOBA_HD

LANES = 128
VMEM_LIMIT = 56 * 1024 * 1024

ROW_TILE = 512
GLA_TILE = 256
PREP_BLOCKS = 8
BIAS_TILES = 14
KV_GROUP = 4
V_ROWS = 80
LOG2E = 1.4426950408889634


def _rms(x, g):
    return x * lax.rsqrt(jnp.mean(x * x, axis=-1, keepdims=True) + EPS) * g


def _ff_chunks(d_ff):
    step = 1024
    return [(s, min(step, d_ff - s)) for s in range(0, d_ff, step)]


def _swiglu(xn, wgu_ref, wd_ref):
    d_ff = wd_ref.shape[0]
    acc = None
    for start, size in _ff_chunks(d_ff):
        gate = jnp.dot(xn, wgu_ref[:, start:start + size], preferred_element_type=F32)
        up = jnp.dot(xn, wgu_ref[:, d_ff + start:d_ff + start + size], preferred_element_type=F32)
        act = (gate * jax.nn.sigmoid(gate) * up).astype(BF16)
        part = jnp.dot(act, wd_ref[start:start + size, :], preferred_element_type=F32)
        acc = part if acc is None else acc + part
    return acc


def _const_spec(shape):
    return pl.BlockSpec(shape, lambda *_: (0,) * len(shape))


def _ffn_in_kernel(x_ref, n1_ref, wgu_ref, wd_ref, n2_ref, wa_ref, wlr_ref, wm_ref, wgk_ref, bgk_ref,
                   h_ref, qg_ref, kg_ref, vg_ref, gg_ref, la_ref, qm_ref, km_ref, vm_ref):
    x = x_ref[...]
    h = x + 0.5 * _swiglu(_rms(x, n1_ref[...]).astype(BF16), wgu_ref, wd_ref)
    h_ref[...] = h
    n = _rms(h, n2_ref[...]).astype(BF16)
    ua = jnp.dot(n, wa_ref[...], preferred_element_type=F32)
    qg_ref[...] = ua[:, 0:GLA_QK]
    kg_ref[...] = ua[:, GLA_QK:2 * GLA_QK]
    vg_ref[...] = ua[:, 2 * GLA_QK:2 * GLA_QK + GLA_V]
    gg_ref[...] = ua[:, 2 * GLA_QK + GLA_V:]
    lr = jnp.dot(n, wlr_ref[...], preferred_element_type=F32)
    z = jnp.dot(lr.astype(BF16), wgk_ref[...], preferred_element_type=F32) + bgk_ref[...]
    la_ref[...] = (jnp.minimum(z, 0.0) - jnp.log1p(jnp.exp(-jnp.abs(z)))) * (1.0 / GLA_TAU)
    um = jnp.dot(n, wm_ref[...], preferred_element_type=F32)
    qm_ref[...] = um[:, 0:MOBA_W]
    km_ref[...] = um[:, MOBA_W:2 * MOBA_W]
    vm_ref[...] = um[:, 2 * MOBA_W:]


def _ffn_in(x, n1, wgu, wd, n2, wa, wlr, wm, wgk, bgk):
    t, d = x.shape
    tm = min(ROW_TILE, t)
    row = lambda w: pl.BlockSpec((tm, w), lambda i: (i, 0))
    outs = [(d, F32), (GLA_QK, F32), (GLA_QK, F32), (GLA_V, F32), (GLA_V, F32), (GLA_QK, F32),
            (MOBA_W, F32), (MOBA_W, F32), (MOBA_W, F32)]
    return pl.pallas_call(
        _ffn_in_kernel,
        grid=(t // tm,),
        in_specs=[row(d)] + [_const_spec(a.shape) for a in (n1, wgu, wd, n2, wa, wlr, wm, wgk, bgk)],
        out_specs=[row(w) for w, _ in outs],
        out_shape=[jax.ShapeDtypeStruct((t, w), dt) for w, dt in outs],
        compiler_params=pltpu.CompilerParams(dimension_semantics=("parallel",), vmem_limit_bytes=VMEM_LIMIT),
        name="ffn_in",
    )(x, n1, wgu, wd, n2, wa, wlr, wm, wgk, bgk)


def _gla_kernel(q_ref, k_ref, la_ref, v_ref, g_ref, gn_ref, o_ref, st_ref):
    @pl.when(pl.program_id(0) == 0)
    def _():
        st_ref[...] = jnp.zeros_like(st_ref)

    c = GLA_CHUNK
    ri = lax.broadcasted_iota(jnp.int32, (c, c), 0)
    ci = lax.broadcasted_iota(jnp.int32, (c, c), 1)
    causal = ri >= ci
    tril = jnp.where(causal, 1.0, 0.0).astype(BF16)
    gn = gn_ref[...]
    for ch in range(q_ref.shape[0] // c):
        rows = slice(ch * c, (ch + 1) * c)
        la = la_ref[rows, :]
        la_hi = la.astype(BF16)
        la_lo = (la - la_hi.astype(F32)).astype(BF16)
        b = jnp.dot(tril, la_hi, preferred_element_type=F32) + jnp.dot(tril, la_lo, preferred_element_type=F32)
        b_last = b[c - 1:c, :]
        q = q_ref[rows, :] * (GLA_DK ** -0.5)
        k = k_ref[rows, :]
        qe = q * jnp.exp(b)
        ki = k * jnp.exp(-b)
        ke = k * jnp.exp(b_last - b)
        a_last = jnp.exp(b_last)
        for h in range(GLA_HEADS):
            ks = slice(h * GLA_DK, (h + 1) * GLA_DK)
            vs = slice(h * GLA_DV, (h + 1) * GLA_DV)
            v = v_ref[rows, vs].astype(BF16)
            st = st_ref[h]
            qeh = qe[:, ks].astype(BF16)
            attn = lax.dot_general(qeh, ki[:, ks].astype(BF16), (((1,), (1,)), ((), ())),
                                   preferred_element_type=F32)
            attn = jnp.where(causal, attn, 0.0).astype(BF16)
            o = jnp.dot(attn, v, preferred_element_type=F32)
            o = o + lax.dot_general(qeh, st.astype(BF16), (((1,), (1,)), ((), ())), preferred_element_type=F32)
            st_ref[h] = st * a_last[:, ks] + lax.dot_general(v, ke[:, ks].astype(BF16), (((0,), (0,)), ((), ())),
                                                             preferred_element_type=F32)
            g = g_ref[rows, vs]
            o_ref[rows, vs] = _rms(o, gn) * (g * jax.nn.sigmoid(g))


def _gla(qg, kg, la, vg, gg, gn):
    t = qg.shape[0]
    tb = min(GLA_TILE, t)
    row = lambda w: pl.BlockSpec((tb, w), lambda i: (i, 0))
    return pl.pallas_call(
        _gla_kernel,
        grid=(t // tb,),
        in_specs=[row(GLA_QK), row(GLA_QK), row(GLA_QK), row(GLA_V), row(GLA_V), _const_spec(gn.shape)],
        out_specs=row(GLA_V),
        out_shape=jax.ShapeDtypeStruct((t, GLA_V), F32),
        scratch_shapes=[pltpu.VMEM((GLA_HEADS, GLA_DV, GLA_DK), F32)],
        compiler_params=pltpu.CompilerParams(dimension_semantics=("arbitrary",), vmem_limit_bytes=VMEM_LIMIT),
        name="gla",
    )(qg, kg, la, vg, gg, gn)


def _moba_bias_kernel(tab_ref, o_ref):
    h = pl.program_id(0)
    delta = pl.program_id(1)
    blk = MOBA_BLOCK
    kk = lax.broadcasted_iota(jnp.int32, (blk, blk), 0)
    qq = lax.broadcasted_iota(jnp.int32, (blk, blk), 1)
    dist = delta * blk + qq - kk
    n = jnp.maximum(dist, 0)
    max_exact = REL_BUCKETS // 2
    nf = jnp.maximum(n, 1).astype(F32)
    large = max_exact + (jnp.log(nf / max_exact) / math.log(REL_MAX_DIST / max_exact)
                         * (REL_BUCKETS - max_exact)).astype(jnp.int32)
    large = jnp.minimum(large, REL_BUCKETS - 1)
    bucket = jnp.where(n < max_exact, n, large)
    bias = jnp.zeros((blk, blk), F32)
    for b in range(REL_BUCKETS):
        bias = jnp.where(bucket == b, tab_ref[b * MOBA_HEADS + h], bias)
    o_ref[0, 0] = jnp.where(dist >= 0, bias * LOG2E, NEG)


def _moba_bias(rel_bias):
    blk = MOBA_BLOCK
    return pl.pallas_call(
        _moba_bias_kernel,
        grid=(MOBA_HEADS, BIAS_TILES),
        in_specs=[pl.BlockSpec(memory_space=pltpu.SMEM)],
        out_specs=pl.BlockSpec((1, 1, blk, blk), lambda h, d: (h, d, 0, 0)),
        out_shape=jax.ShapeDtypeStruct((MOBA_HEADS, BIAS_TILES, blk, blk), F32),
        compiler_params=pltpu.CompilerParams(dimension_semantics=("parallel", "parallel")),
        name="moba_bias",
    )(rel_bias.reshape(-1))


def _moba_prep_kernel(q_ref, k_ref, v_ref, qt_ref, ka_ref, vt_ref, km_ref):
    t = pl.program_id(1)
    blk = MOBA_BLOCK
    rows = PREP_BLOCKS * blk
    hd = MOBA_HD

    @pl.when(t == 0)
    def _():
        km_ref[...] = jnp.zeros_like(km_ref)

    lane = lax.broadcasted_iota(jnp.int32, (rows, LANES), 1)
    own = t * PREP_BLOCKS + lax.broadcasted_iota(jnp.int32, (rows, LANES), 0) // blk
    head_lanes = lane < hd
    lane8 = lax.broadcasted_iota(jnp.int32, (PREP_BLOCKS, LANES), 1)
    for a in range(2):
        q = q_ref[...]
        k = k_ref[...]
        v = v_ref[...]
        if a == 1:
            q = pltpu.roll(q, hd, 1)
            k = pltpu.roll(k, hd, 1)
            v = pltpu.roll(v, hd, 1)
        kmean = jnp.mean(k.reshape(PREP_BLOCKS, blk, LANES), axis=1)
        km_ref[a, pl.ds(pl.multiple_of(hd + t * PREP_BLOCKS, PREP_BLOCKS), PREP_BLOCKS), :] = (
            jnp.where(lane8 < hd, kmean, 0.0))
        gate = lax.dot_general(jnp.where(head_lanes, q, 0.0), km_ref[a], (((1,), (1,)), ((), ())),
                               precision=lax.Precision.HIGHEST, preferred_element_type=F32)
        gate = jnp.where((lane >= hd) & (lane - hd < own), gate, -jnp.inf)
        chosen = lane - hd == own
        for _ in range(MOBA_TOPK):
            top = jnp.max(gate, axis=1, keepdims=True)
            first = jnp.min(jnp.where(gate == top, lane, 2 * LANES), axis=1, keepdims=True)
            pick = (lane == first) & (top > -jnp.inf)
            chosen = chosen | pick
            gate = jnp.where(lane == first, -jnp.inf, gate)
        q_aug = jnp.where(head_lanes, q * (hd ** -0.5 * LOG2E), jnp.where(chosen, 0.0, -1.0))
        k_aug = jnp.where(head_lanes, k, jnp.where(lane - hd == own, -NEG, 0.0))
        v_aug = jnp.where(head_lanes, v, jnp.where(lane == hd, 1.0, 0.0))
        q_t = q_aug.T.astype(BF16)
        v_t = v_aug.T.astype(BF16)
        k_aug = k_aug.astype(BF16)
        for b in range(PREP_BLOCKS):
            qt_ref[a, b] = q_t[:, b * blk:(b + 1) * blk]
            ka_ref[a, b] = k_aug[b * blk:(b + 1) * blk, :]
        group = KV_GROUP * blk
        for g in range(PREP_BLOCKS // KV_GROUP):
            vt_ref[a, g] = v_t[0:V_ROWS, g * group:(g + 1) * group]


def _moba_prep(qm, km, vm):
    t = qm.shape[0]
    blk = MOBA_BLOCK
    nb = t // blk
    rows = PREP_BLOCKS * blk
    pair = pl.BlockSpec((rows, LANES), lambda p, i: (i, p))
    t_spec = pl.BlockSpec((2, PREP_BLOCKS, LANES, blk), lambda p, i: (p, i, 0, 0))
    k_spec = pl.BlockSpec((2, PREP_BLOCKS, blk, LANES), lambda p, i: (p, i, 0, 0))
    v_spec = pl.BlockSpec((2, PREP_BLOCKS // KV_GROUP, V_ROWS, KV_GROUP * blk), lambda p, i: (p, i, 0, 0))
    return pl.pallas_call(
        _moba_prep_kernel,
        grid=(MOBA_HEADS // 2, nb // PREP_BLOCKS),
        in_specs=[pair, pair, pair],
        out_specs=[t_spec, k_spec, v_spec],
        out_shape=[jax.ShapeDtypeStruct((MOBA_HEADS, nb, LANES, blk), BF16),
                   jax.ShapeDtypeStruct((MOBA_HEADS, nb, blk, LANES), BF16),
                   jax.ShapeDtypeStruct((MOBA_HEADS, nb // KV_GROUP, V_ROWS, KV_GROUP * blk), BF16)],
        scratch_shapes=[pltpu.VMEM((2, LANES, LANES), F32)],
        compiler_params=pltpu.CompilerParams(dimension_semantics=("parallel", "arbitrary"),
                                             vmem_limit_bytes=VMEM_LIMIT),
        name="moba_prep",
    )(qm, km, vm)


def _moba_attn_kernel(qi_ref, gi_ref, qt_ref, ka_ref, vt_ref, bias_ref, o_ref, s0_ref, s1_ref, p0_ref, p1_ref,
                      acc_ref):
    n_pairs = qi_ref.shape[0] - 2
    blk = MOBA_BLOCK
    s_refs = (s0_ref, s1_ref)
    p_refs = (p0_ref, p1_ref)

    def scores(t, slot):
        s_ref = s_refs[slot]
        qi = qi_ref[t]
        gi = gi_ref[t]
        qt = qt_ref[0, qi]
        col_max = None
        for b in range(KV_GROUP):
            j = gi * KV_GROUP + b
            s = jnp.dot(ka_ref[0, j], qt, preferred_element_type=F32)
            s = s + bias_ref[0, jnp.clip(qi - j, 0, BIAS_TILES - 1)]
            s_ref[b] = s
            cm = jnp.max(s, axis=0, keepdims=True)
            col_max = cm if col_max is None else jnp.maximum(col_max, cm)
        return col_max

    def softmax(t, slot, m_run, col_max):
        m_prev = jnp.where(gi_ref[t] == 0, -jnp.inf, m_run)
        m = jnp.maximum(m_prev, col_max)
        for b in range(KV_GROUP):
            p_refs[slot][b * blk:(b + 1) * blk, :] = jnp.exp2(s_refs[slot][b] - m).astype(BF16)
        return m, jnp.exp2(m_prev - m)

    def values(t, slot, alpha):
        acc = acc_ref[...] * alpha + jnp.dot(vt_ref[0, gi_ref[t]], p_refs[slot][...], preferred_element_type=F32)
        acc_ref[...] = acc
        o_ref[0, qi_ref[t]] = (acc[0:MOBA_HD] / acc[MOBA_HD:MOBA_HD + 1]).astype(o_ref.dtype)

    acc_ref[...] = jnp.zeros_like(acc_ref)
    cm0 = scores(0, 0)
    cm1 = scores(1, 1)
    m0, alpha0 = softmax(0, 0, jnp.full((1, blk), -jnp.inf, F32), cm0)

    def step(u, carry):
        m, alpha, cm = carry
        t = 2 * u
        cm_a = scores(t + 2, 0)
        m_a, alpha_a = softmax(t + 1, 1, m, cm)
        values(t, 0, alpha)
        cm_b = scores(t + 3, 1)
        m_b, alpha_b = softmax(t + 2, 0, m_a, cm_a)
        values(t + 1, 1, alpha_a)
        return m_b, alpha_b, cm_b

    assert n_pairs % 2 == 0
    lax.fori_loop(0, n_pairs // 2, step, (m0, alpha0, cm1))


def _moba_attn(qt, ka, vt, bias):
    nb = qt.shape[1]
    blk = MOBA_BLOCK
    pairs = [(qi, g) for qi in range(nb) for g in range(qi // KV_GROUP + 1)] + [(0, 0), (0, 0)]
    qi_tab = jnp.asarray([a for a, _ in pairs], jnp.int32)
    gi_tab = jnp.asarray([b for _, b in pairs], jnp.int32)
    head = lambda shape: pl.BlockSpec((1,) + shape, lambda h, *_: (h,) + (0,) * len(shape))
    return pl.pallas_call(
        _moba_attn_kernel,
        grid_spec=pltpu.PrefetchScalarGridSpec(
            num_scalar_prefetch=2,
            grid=(MOBA_HEADS,),
            in_specs=[head((nb, LANES, blk)), head((nb, blk, LANES)),
                      head((nb // KV_GROUP, V_ROWS, KV_GROUP * blk)), head((BIAS_TILES, blk, blk))],
            out_specs=head((nb, MOBA_HD, blk)),
            scratch_shapes=[pltpu.VMEM((KV_GROUP, blk, blk), F32), pltpu.VMEM((KV_GROUP, blk, blk), F32),
                            pltpu.VMEM((KV_GROUP * blk, blk), BF16), pltpu.VMEM((KV_GROUP * blk, blk), BF16),
                            pltpu.VMEM((V_ROWS, blk), F32)]),
        out_shape=jax.ShapeDtypeStruct((MOBA_HEADS, nb, MOBA_HD, blk), BF16),
        compiler_params=pltpu.CompilerParams(dimension_semantics=("parallel",), vmem_limit_bytes=VMEM_LIMIT),
        name="moba_attn",
    )(qi_tab, gi_tab, qt, ka, vt, bias)


def _out_ffn_kernel(final, h_ref, og_ref, om_ref, p_ref, wog_ref, wom_ref, n3_ref, wgu_ref, wd_ref,
                    n4_ref, wpg_ref, wpp_ref, nf_ref, o_ref):
    mix = jnp.dot(og_ref[...].astype(BF16), wog_ref[...], preferred_element_type=F32)
    mix = mix + jnp.dot(om_ref[...], wom_ref[...], preferred_element_type=F32)
    h = h_ref[...] + mix
    h = h + 0.5 * _swiglu(_rms(h, n3_ref[...]).astype(BF16), wgu_ref, wd_ref)
    gate = jax.nn.sigmoid(jnp.dot(_rms(h, n4_ref[...]).astype(BF16), wpg_ref[...], preferred_element_type=F32))
    h = h + gate * jnp.dot(p_ref[...].astype(BF16), wpp_ref[...], preferred_element_type=F32)
    o_ref[...] = _rms(h, nf_ref[...]) if final else h


def _out_ffn(final, h, og, om, p, wog, wom, n3, wgu, wd, n4, wpg, wpp, nf):
    t, d = h.shape
    tm = min(ROW_TILE, t)
    row = lambda w: pl.BlockSpec((tm, w), lambda i: (i, 0))
    consts = (wog, wom, n3, wgu, wd, n4, wpg, wpp, nf)
    return pl.pallas_call(
        functools.partial(_out_ffn_kernel, final),
        grid=(t // tm,),
        in_specs=[row(d), row(GLA_V), row(MOBA_W), row(p.shape[1])]
        + [_const_spec(a.shape) for a in consts],
        out_specs=row(d),
        out_shape=jax.ShapeDtypeStruct((t, d), F32),
        compiler_params=pltpu.CompilerParams(dimension_semantics=("parallel",), vmem_limit_bytes=VMEM_LIMIT),
        name="out_ffn",
    )(h, og, om, p, *consts)


def kernel(x, p, ffn1_norm, ffn1_w_gu, ffn1_w_down, mix_norm, w_in, w_gk_up, b_gk, gla_out_norm, w_out,
           ffn2_norm, ffn2_w_gu, ffn2_w_down, ple_norm, w_ple_gate, w_ple_proj, rel_bias, final_norm):
    batch, t, d = x.shape
    depth = p.shape[0]
    assert batch == 1 and t % (PREP_BLOCKS * MOBA_BLOCK) == 0
    row = lambda a: a.reshape(1, -1)
    bias = _moba_bias(rel_bias)
    lr0 = 2 * GLA_QK + 2 * GLA_V
    h = x[0]
    for i in range(depth):
        wi = w_in[i]
        wa = wi[:, :lr0].astype(BF16)
        wlr = jnp.pad(wi[:, lr0:lr0 + GLA_LOWRANK], ((0, 0), (0, LANES - GLA_LOWRANK))).astype(BF16)
        wm = wi[:, lr0 + GLA_LOWRANK:].astype(BF16)
        wgk = jnp.pad(w_gk_up[i], ((0, LANES - GLA_LOWRANK), (0, 0))).astype(BF16)
        h1, qg, kg, vg, gg, la, qm, km, vm = _ffn_in(
            h, row(ffn1_norm[i]), ffn1_w_gu[i].astype(BF16), ffn1_w_down[i].astype(BF16), row(mix_norm[i]),
            wa, wlr, wm, wgk, row(b_gk[i]))
        og = _gla(qg, kg, la, vg, gg, row(gla_out_norm[i]))
        qt, ka, vt = _moba_prep(qm, km, vm)
        om = _moba_attn(qt, ka, vt, bias).transpose(1, 3, 0, 2).reshape(t, MOBA_W)
        wo = w_out[i].astype(BF16)
        h = _out_ffn(i == depth - 1, h1, og, om, p[i, 0], wo[:GLA_V], wo[GLA_V:], row(ffn2_norm[i]),
                     ffn2_w_gu[i].astype(BF16), ffn2_w_down[i].astype(BF16), row(ple_norm[i]),
                     w_ple_gate[i].astype(BF16), w_ple_proj[i].astype(BF16), row(final_norm))
    return h[None]
```

```python
import functools
import math

import jax
import jax.numpy as jnp
from jax import lax
from jax.experimental import pallas as pl
from jax.experimental.pallas import tpu as pltpu

F32 = jnp.float32
BF16 = jnp.bfloat16

GLA_HEADS = 4
GLA_DK = 64
GLA_DV = 128
GLA_LOWRANK = 16
GLA_TAU = 16.0
GLA_CHUNK = 64
MOBA_HEADS = 8
MOBA_HD = 64
MOBA_BLOCK = 256
MOBA_TOPK = 3
REL_BUCKETS = 32
REL_MAX_DIST = 4096
EPS = 1e-6
NEG = -1e30

GLA_QK = GLA_HEADS * GLA_DK
GLA_V = GLA_HEADS * GLA_DV
MOBA_W = MOBA_HEADS * MOBA_HD

LANES = 128
VMEM_LIMIT = 56 * 1024 * 1024

ROW_TILE = 512
GLA_TILE = 256
PREP_BLOCKS = 8
BIAS_TILES = 14
KV_GROUP = 4
ATTN_UNROLL = 8
V_ROWS = 80
LOG2E = 1.4426950408889634


def _rms(x, g):
    return x * lax.rsqrt(jnp.mean(x * x, axis=-1, keepdims=True) + EPS) * g


def _ff_chunks(d_ff):
    step = 1024
    return [(s, min(step, d_ff - s)) for s in range(0, d_ff, step)]


def _swiglu(xn, wgu_ref, wd_ref):
    d_ff = wd_ref.shape[0]
    acc = None
    for start, size in _ff_chunks(d_ff):
        gate = jnp.dot(xn, wgu_ref[:, start:start + size], preferred_element_type=F32)
        up = jnp.dot(xn, wgu_ref[:, d_ff + start:d_ff + start + size], preferred_element_type=F32)
        act = (gate * jax.nn.sigmoid(gate) * up).astype(BF16)
        part = jnp.dot(act, wd_ref[start:start + size, :], preferred_element_type=F32)
        acc = part if acc is None else acc + part
    return acc


def _const_spec(shape):
    return pl.BlockSpec(shape, lambda *_: (0,) * len(shape))


def _ffn_in_kernel(x_ref, n1_ref, wgu_ref, wd_ref, n2_ref, wa_ref, wlr_ref, wm_ref, wgk_ref, bgk_ref,
                   h_ref, qg_ref, kg_ref, vg_ref, gg_ref, la_ref, qm_ref, km_ref, vm_ref):
    x = x_ref[...]
    h = x + 0.5 * _swiglu(_rms(x, n1_ref[...]).astype(BF16), wgu_ref, wd_ref)
    h_ref[...] = h
    n = _rms(h, n2_ref[...]).astype(BF16)
    ua = jnp.dot(n, wa_ref[...], preferred_element_type=F32)
    qg_ref[...] = ua[:, 0:GLA_QK]
    kg_ref[...] = ua[:, GLA_QK:2 * GLA_QK]
    vg_ref[...] = ua[:, 2 * GLA_QK:2 * GLA_QK + GLA_V]
    gg_ref[...] = ua[:, 2 * GLA_QK + GLA_V:]
    lr = jnp.dot(n, wlr_ref[...], preferred_element_type=F32)
    z = jnp.dot(lr.astype(BF16), wgk_ref[...], preferred_element_type=F32) + bgk_ref[...]
    la_ref[...] = (jnp.minimum(z, 0.0) - jnp.log1p(jnp.exp(-jnp.abs(z)))) * (1.0 / GLA_TAU)
    um = jnp.dot(n, wm_ref[...], preferred_element_type=F32)
    qm_ref[...] = um[:, 0:MOBA_W]
    km_ref[...] = um[:, MOBA_W:2 * MOBA_W]
    vm_ref[...] = um[:, 2 * MOBA_W:]


def _ffn_in(x, n1, wgu, wd, n2, wa, wlr, wm, wgk, bgk):
    t, d = x.shape
    tm = min(ROW_TILE, t)
    row = lambda w: pl.BlockSpec((tm, w), lambda i: (i, 0))
    outs = [(d, F32), (GLA_QK, F32), (GLA_QK, F32), (GLA_V, F32), (GLA_V, F32), (GLA_QK, F32),
            (MOBA_W, F32), (MOBA_W, F32), (MOBA_W, F32)]
    return pl.pallas_call(
        _ffn_in_kernel,
        grid=(t // tm,),
        in_specs=[row(d)] + [_const_spec(a.shape) for a in (n1, wgu, wd, n2, wa, wlr, wm, wgk, bgk)],
        out_specs=[row(w) for w, _ in outs],
        out_shape=[jax.ShapeDtypeStruct((t, w), dt) for w, dt in outs],
        compiler_params=pltpu.CompilerParams(dimension_semantics=("parallel",), vmem_limit_bytes=VMEM_LIMIT),
        name="ffn_in",
    )(x, n1, wgu, wd, n2, wa, wlr, wm, wgk, bgk)


def _gla_kernel(q_ref, k_ref, la_ref, v_ref, g_ref, gn_ref, o_ref, st_ref):
    @pl.when(pl.program_id(0) == 0)
    def _():
        st_ref[...] = jnp.zeros_like(st_ref)

    c = GLA_CHUNK
    ri = lax.broadcasted_iota(jnp.int32, (c, c), 0)
    ci = lax.broadcasted_iota(jnp.int32, (c, c), 1)
    causal = ri >= ci
    tril = jnp.where(causal, 1.0, 0.0).astype(BF16)
    gn = gn_ref[...]
    for ch in range(q_ref.shape[0] // c):
        rows = slice(ch * c, (ch + 1) * c)
        la = la_ref[rows, :]
        la_hi = la.astype(BF16)
        la_lo = (la - la_hi.astype(F32)).astype(BF16)
        b = jnp.dot(tril, la_hi, preferred_element_type=F32) + jnp.dot(tril, la_lo, preferred_element_type=F32)
        b_last = b[c - 1:c, :]
        q = q_ref[rows, :] * (GLA_DK ** -0.5)
        k = k_ref[rows, :]
        qe = q * jnp.exp(b)
        ki = k * jnp.exp(-b)
        ke = k * jnp.exp(b_last - b)
        a_last = jnp.exp(b_last)
        for h in range(GLA_HEADS):
            ks = slice(h * GLA_DK, (h + 1) * GLA_DK)
            vs = slice(h * GLA_DV, (h + 1) * GLA_DV)
            v = v_ref[rows, vs].astype(BF16)
            st = st_ref[h]
            qeh = qe[:, ks].astype(BF16)
            attn = lax.dot_general(qeh, ki[:, ks].astype(BF16), (((1,), (1,)), ((), ())),
                                   preferred_element_type=F32)
            attn = jnp.where(causal, attn, 0.0).astype(BF16)
            o = jnp.dot(attn, v, preferred_element_type=F32)
            o = o + lax.dot_general(qeh, st.astype(BF16), (((1,), (1,)), ((), ())), preferred_element_type=F32)
            st_ref[h] = st * a_last[:, ks] + lax.dot_general(v, ke[:, ks].astype(BF16), (((0,), (0,)), ((), ())),
                                                             preferred_element_type=F32)
            g = g_ref[rows, vs]
            o_ref[rows, vs] = _rms(o, gn) * (g * jax.nn.sigmoid(g))


def _gla(qg, kg, la, vg, gg, gn):
    t = qg.shape[0]
    tb = min(GLA_TILE, t)
    row = lambda w: pl.BlockSpec((tb, w), lambda i: (i, 0))
    return pl.pallas_call(
        _gla_kernel,
        grid=(t // tb,),
        in_specs=[row(GLA_QK), row(GLA_QK), row(GLA_QK), row(GLA_V), row(GLA_V), _const_spec(gn.shape)],
        out_specs=row(GLA_V),
        out_shape=jax.ShapeDtypeStruct((t, GLA_V), F32),
        scratch_shapes=[pltpu.VMEM((GLA_HEADS, GLA_DV, GLA_DK), F32)],
        compiler_params=pltpu.CompilerParams(dimension_semantics=("arbitrary",), vmem_limit_bytes=VMEM_LIMIT),
        name="gla",
    )(qg, kg, la, vg, gg, gn)


def _moba_bias_kernel(tab_ref, o_ref):
    h = pl.program_id(0)
    delta = pl.program_id(1)
    blk = MOBA_BLOCK
    kk = lax.broadcasted_iota(jnp.int32, (blk, blk), 0)
    qq = lax.broadcasted_iota(jnp.int32, (blk, blk), 1)
    dist = delta * blk + qq - kk
    n = jnp.maximum(dist, 0)
    max_exact = REL_BUCKETS // 2
    nf = jnp.maximum(n, 1).astype(F32)
    large = max_exact + (jnp.log(nf / max_exact) / math.log(REL_MAX_DIST / max_exact)
                         * (REL_BUCKETS - max_exact)).astype(jnp.int32)
    large = jnp.minimum(large, REL_BUCKETS - 1)
    bucket = jnp.where(n < max_exact, n, large)
    bias = jnp.zeros((blk, blk), F32)
    for b in range(REL_BUCKETS):
        bias = jnp.where(bucket == b, tab_ref[b * MOBA_HEADS + h], bias)
    o_ref[0, 0] = jnp.where(dist >= 0, bias * LOG2E, NEG).astype(o_ref.dtype)


def _moba_bias(rel_bias):
    blk = MOBA_BLOCK
    return pl.pallas_call(
        _moba_bias_kernel,
        grid=(MOBA_HEADS, BIAS_TILES),
        in_specs=[pl.BlockSpec(memory_space=pltpu.SMEM)],
        out_specs=pl.BlockSpec((1, 1, blk, blk), lambda h, d: (h, d, 0, 0)),
        out_shape=jax.ShapeDtypeStruct((MOBA_HEADS, BIAS_TILES, blk, blk), BF16),
        compiler_params=pltpu.CompilerParams(dimension_semantics=("parallel", "parallel")),
        name="moba_bias",
    )(rel_bias.reshape(-1))


def _moba_prep_kernel(q_ref, k_ref, v_ref, qt_ref, ka_ref, vt_ref, km_ref):
    t = pl.program_id(1)
    blk = MOBA_BLOCK
    rows = PREP_BLOCKS * blk
    hd = MOBA_HD

    @pl.when(t == 0)
    def _():
        km_ref[...] = jnp.zeros_like(km_ref)

    lane = lax.broadcasted_iota(jnp.int32, (rows, LANES), 1)
    own = t * PREP_BLOCKS + lax.broadcasted_iota(jnp.int32, (rows, LANES), 0) // blk
    head_lanes = lane < hd
    lane8 = lax.broadcasted_iota(jnp.int32, (PREP_BLOCKS, LANES), 1)
    for a in range(2):
        q = q_ref[...]
        k = k_ref[...]
        v = v_ref[...]
        if a == 1:
            q = pltpu.roll(q, hd, 1)
            k = pltpu.roll(k, hd, 1)
            v = pltpu.roll(v, hd, 1)
        kmean = jnp.mean(k.reshape(PREP_BLOCKS, blk, LANES), axis=1)
        km_ref[a, pl.ds(pl.multiple_of(hd + t * PREP_BLOCKS, PREP_BLOCKS), PREP_BLOCKS), :] = (
            jnp.where(lane8 < hd, kmean, 0.0))
        gate = lax.dot_general(jnp.where(head_lanes, q, 0.0), km_ref[a], (((1,), (1,)), ((), ())),
                               precision=lax.Precision.HIGHEST, preferred_element_type=F32)
        gate = jnp.where((lane >= hd) & (lane - hd < own), gate, -jnp.inf)
        chosen = lane - hd == own
        for _ in range(MOBA_TOPK):
            top = jnp.max(gate, axis=1, keepdims=True)
            first = jnp.min(jnp.where(gate == top, lane, 2 * LANES), axis=1, keepdims=True)
            pick = (lane == first) & (top > -jnp.inf)
            chosen = chosen | pick
            gate = jnp.where(lane == first, -jnp.inf, gate)
        q_aug = jnp.where(head_lanes, q * (hd ** -0.5 * LOG2E), jnp.where(chosen, 0.0, -1.0))
        k_aug = jnp.where(head_lanes, k, jnp.where(lane - hd == own, -NEG, 0.0))
        v_aug = jnp.where(head_lanes, v, jnp.where(lane == hd, 1.0, 0.0))
        q_t = q_aug.T.astype(BF16)
        v_t = v_aug.T.astype(BF16)
        k_aug = k_aug.astype(BF16)
        for b in range(PREP_BLOCKS):
            qt_ref[a, b] = q_t[:, b * blk:(b + 1) * blk]
            ka_ref[a, b] = k_aug[b * blk:(b + 1) * blk, :]
        group = KV_GROUP * blk
        for g in range(PREP_BLOCKS // KV_GROUP):
            vt_ref[a, g] = v_t[0:V_ROWS, g * group:(g + 1) * group]


def _moba_prep(qm, km, vm):
    t = qm.shape[0]
    blk = MOBA_BLOCK
    nb = t // blk
    rows = PREP_BLOCKS * blk
    pair = pl.BlockSpec((rows, LANES), lambda p, i: (i, p))
    t_spec = pl.BlockSpec((2, PREP_BLOCKS, LANES, blk), lambda p, i: (p, i, 0, 0))
    k_spec = pl.BlockSpec((2, PREP_BLOCKS, blk, LANES), lambda p, i: (p, i, 0, 0))
    v_spec = pl.BlockSpec((2, PREP_BLOCKS // KV_GROUP, V_ROWS, KV_GROUP * blk), lambda p, i: (p, i, 0, 0))
    return pl.pallas_call(
        _moba_prep_kernel,
        grid=(MOBA_HEADS // 2, nb // PREP_BLOCKS),
        in_specs=[pair, pair, pair],
        out_specs=[t_spec, k_spec, v_spec],
        out_shape=[jax.ShapeDtypeStruct((MOBA_HEADS, nb, LANES, blk), BF16),
                   jax.ShapeDtypeStruct((MOBA_HEADS, nb, blk, LANES), BF16),
                   jax.ShapeDtypeStruct((MOBA_HEADS, nb // KV_GROUP, V_ROWS, KV_GROUP * blk), BF16)],
        scratch_shapes=[pltpu.VMEM((2, LANES, LANES), F32)],
        compiler_params=pltpu.CompilerParams(dimension_semantics=("parallel", "arbitrary"),
                                             vmem_limit_bytes=VMEM_LIMIT),
        name="moba_prep",
    )(qm, km, vm)


def _moba_attn_kernel(qi_ref, gi_ref, qt_ref, ka_ref, vt_ref, bias_ref, o_ref, s0_ref, s1_ref, p0_ref, p1_ref,
                      acc_ref):
    n_pairs = qi_ref.shape[0] - 2
    blk = MOBA_BLOCK
    s_refs = (s0_ref, s1_ref)
    p_refs = (p0_ref, p1_ref)

    def scores(t, slot):
        s_ref = s_refs[slot]
        qi = qi_ref[t]
        gi = gi_ref[t]
        qt = qt_ref[0, qi]
        col_max = None
        for b in range(KV_GROUP):
            j = gi * KV_GROUP + b
            s = jnp.dot(ka_ref[0, j], qt, preferred_element_type=F32).astype(BF16)
            s = s + bias_ref[0, jnp.clip(qi - j, 0, BIAS_TILES - 1)]
            s_ref[b] = s
            cm = jnp.max(s, axis=0, keepdims=True)
            col_max = cm if col_max is None else jnp.maximum(col_max, cm)
        return col_max.astype(F32)

    def softmax(t, slot, m_run, col_max):
        m_prev = jnp.where(gi_ref[t] == 0, -jnp.inf, m_run)
        m = jnp.maximum(m_prev, col_max)
        for b in range(KV_GROUP):
            p_refs[slot][b * blk:(b + 1) * blk, :] = jnp.exp2(s_refs[slot][b] - m.astype(BF16))
        return m, jnp.exp2(m_prev - m)

    def values(t, slot, alpha):
        acc = acc_ref[...] * alpha + jnp.dot(vt_ref[0, gi_ref[t]], p_refs[slot][...], preferred_element_type=F32)
        acc_ref[...] = acc
        o_ref[0, qi_ref[t]] = (acc[0:MOBA_HD] / acc[MOBA_HD:MOBA_HD + 1]).astype(o_ref.dtype)

    acc_ref[...] = jnp.zeros_like(acc_ref)
    cm0 = scores(0, 0)
    cm1 = scores(1, 1)
    m0, alpha0 = softmax(0, 0, jnp.full((1, blk), -jnp.inf, F32), cm0)

    def step(u, carry):
        m, alpha, cm = carry
        for i in range(ATTN_UNROLL):
            t = ATTN_UNROLL * u + i
            cm_next = scores(t + 2, i % 2)
            m, alpha_next = softmax(t + 1, 1 - i % 2, m, cm)
            values(t, i % 2, alpha)
            alpha, cm = alpha_next, cm_next
        return m, alpha, cm

    assert n_pairs % ATTN_UNROLL == 0 and ATTN_UNROLL % 2 == 0
    lax.fori_loop(0, n_pairs // ATTN_UNROLL, step, (m0, alpha0, cm1))


def _moba_attn(qt, ka, vt, bias):
    nb = qt.shape[1]
    blk = MOBA_BLOCK
    pairs = [(qi, g) for qi in range(nb) for g in range(qi // KV_GROUP + 1)] + [(0, 0), (0, 0)]
    qi_tab = jnp.asarray([a for a, _ in pairs], jnp.int32)
    gi_tab = jnp.asarray([b for _, b in pairs], jnp.int32)
    head = lambda shape: pl.BlockSpec((1,) + shape, lambda h, *_: (h,) + (0,) * len(shape))
    return pl.pallas_call(
        _moba_attn_kernel,
        grid_spec=pltpu.PrefetchScalarGridSpec(
            num_scalar_prefetch=2,
            grid=(MOBA_HEADS,),
            in_specs=[head((nb, LANES, blk)), head((nb, blk, LANES)),
                      head((nb // KV_GROUP, V_ROWS, KV_GROUP * blk)), head((BIAS_TILES, blk, blk))],
            out_specs=head((nb, MOBA_HD, blk)),
            scratch_shapes=[pltpu.VMEM((KV_GROUP, blk, blk), BF16), pltpu.VMEM((KV_GROUP, blk, blk), BF16),
                            pltpu.VMEM((KV_GROUP * blk, blk), BF16), pltpu.VMEM((KV_GROUP * blk, blk), BF16),
                            pltpu.VMEM((V_ROWS, blk), F32)]),
        out_shape=jax.ShapeDtypeStruct((MOBA_HEADS, nb, MOBA_HD, blk), BF16),
        compiler_params=pltpu.CompilerParams(dimension_semantics=("parallel",), vmem_limit_bytes=VMEM_LIMIT),
        name="moba_attn",
    )(qi_tab, gi_tab, qt, ka, vt, bias)


def _out_ffn_kernel(final, h_ref, og_ref, om_ref, p_ref, wog_ref, wom_ref, n3_ref, wgu_ref, wd_ref,
                    n4_ref, wpg_ref, wpp_ref, nf_ref, o_ref):
    mix = jnp.dot(og_ref[...].astype(BF16), wog_ref[...], preferred_element_type=F32)
    mix = mix + jnp.dot(om_ref[...], wom_ref[...], preferred_element_type=F32)
    h = h_ref[...] + mix
    h = h + 0.5 * _swiglu(_rms(h, n3_ref[...]).astype(BF16), wgu_ref, wd_ref)
    gate = jax.nn.sigmoid(jnp.dot(_rms(h, n4_ref[...]).astype(BF16), wpg_ref[...], preferred_element_type=F32))
    h = h + gate * jnp.dot(p_ref[...].astype(BF16), wpp_ref[...], preferred_element_type=F32)
    o_ref[...] = _rms(h, nf_ref[...]) if final else h


def _out_ffn(final, h, og, om, p, wog, wom, n3, wgu, wd, n4, wpg, wpp, nf):
    t, d = h.shape
    tm = min(ROW_TILE, t)
    row = lambda w: pl.BlockSpec((tm, w), lambda i: (i, 0))
    consts = (wog, wom, n3, wgu, wd, n4, wpg, wpp, nf)
    return pl.pallas_call(
        functools.partial(_out_ffn_kernel, final),
        grid=(t // tm,),
        in_specs=[row(d), row(GLA_V), row(MOBA_W), row(p.shape[1])]
        + [_const_spec(a.shape) for a in consts],
        out_specs=row(d),
        out_shape=jax.ShapeDtypeStruct((t, d), F32),
        compiler_params=pltpu.CompilerParams(dimension_semantics=("parallel",), vmem_limit_bytes=VMEM_LIMIT),
        name="out_ffn",
    )(h, og, om, p, *consts)


def kernel(x, p, ffn1_norm, ffn1_w_gu, ffn1_w_down, mix_norm, w_in, w_gk_up, b_gk, gla_out_norm, w_out,
           ffn2_norm, ffn2_w_gu, ffn2_w_down, ple_norm, w_ple_gate, w_ple_proj, rel_bias, final_norm):
    batch, t, d = x.shape
    depth = p.shape[0]
    assert batch == 1 and t % (PREP_BLOCKS * MOBA_BLOCK) == 0
    row = lambda a: a.reshape(1, -1)
    bias = _moba_bias(rel_bias)
    lr0 = 2 * GLA_QK + 2 * GLA_V
    h = x[0]
    for i in range(depth):
        wi = w_in[i]
        wa = wi[:, :lr0].astype(BF16)
        wlr = jnp.pad(wi[:, lr0:lr0 + GLA_LOWRANK], ((0, 0), (0, LANES - GLA_LOWRANK))).astype(BF16)
        wm = wi[:, lr0 + GLA_LOWRANK:].astype(BF16)
        wgk = jnp.pad(w_gk_up[i], ((0, LANES - GLA_LOWRANK), (0, 0))).astype(BF16)
        h1, qg, kg, vg, gg, la, qm, km, vm = _ffn_in(
            h, row(ffn1_norm[i]), ffn1_w_gu[i].astype(BF16), ffn1_w_down[i].astype(BF16), row(mix_norm[i]),
            wa, wlr, wm, wgk, row(b_gk[i]))
        og = _gla(qg, kg, la, vg, gg, row(gla_out_norm[i]))
        qt, ka, vt = _moba_prep(qm, km, vm)
        om = _moba_attn(qt, ka, vt, bias).transpose(1, 3, 0, 2).reshape(t, MOBA_W)
        wo = w_out[i].astype(BF16)
        h = _out_ffn(i == depth - 1, h1, og, om, p[i, 0], wo[:GLA_V], wo[GLA_V:], row(ffn2_norm[i]),
                     ffn2_w_gu[i].astype(BF16), ffn2_w_down[i].astype(BF16), row(ple_norm[i]),
                     w_ple_gate[i].astype(BF16), w_ple_proj[i].astype(BF16), row(final_norm))
    return h[None]
```

```python
import functools
import math

import jax
import jax.numpy as jnp
from jax import lax
from jax.experimental import pallas as pl
from jax.experimental.pallas import tpu as pltpu

F32 = jnp.float32
BF16 = jnp.bfloat16

GLA_HEADS = 4
GLA_DK = 64
GLA_DV = 128
GLA_LOWRANK = 16
GLA_TAU = 16.0
GLA_CHUNK = 64
MOBA_HEADS = 8
MOBA_HD = 64
MOBA_BLOCK = 256
MOBA_TOPK = 3
REL_BUCKETS = 32
REL_MAX_DIST = 4096
EPS = 1e-6
NEG = -1e30

GLA_QK = GLA_HEADS * GLA_DK
GLA_V = GLA_HEADS * GLA_DV
MOBA_W = MOBA_HEADS * MOBA_HD

LANES = 128
VMEM_LIMIT = 56 * 1024 * 1024

ROW_TILE = 512
GLA_TILE = 256
PREP_BLOCKS = 8
BIAS_TILES = 14
KV_GROUP = 4
ATTN_UNROLL = 8
V_ROWS = 80
LOG2E = 1.4426950408889634


def _rms(x, g):
    return x * lax.rsqrt(jnp.mean(x * x, axis=-1, keepdims=True) + EPS) * g


def _ff_chunks(d_ff):
    step = 1024
    return [(s, min(step, d_ff - s)) for s in range(0, d_ff, step)]


def _swiglu(xn, wgu_ref, wd_ref):
    d_ff = wd_ref.shape[0]
    acc = None
    for start, size in _ff_chunks(d_ff):
        gate = jnp.dot(xn, wgu_ref[:, start:start + size], preferred_element_type=F32)
        up = jnp.dot(xn, wgu_ref[:, d_ff + start:d_ff + start + size], preferred_element_type=F32)
        act = (gate * jax.nn.sigmoid(gate) * up).astype(BF16)
        part = jnp.dot(act, wd_ref[start:start + size, :], preferred_element_type=F32)
        acc = part if acc is None else acc + part
    return acc


def _const_spec(shape):
    return pl.BlockSpec(shape, lambda *_: (0,) * len(shape))


def _ffn_in_kernel(x_ref, n1_ref, wgu_ref, wd_ref, n2_ref, wa_ref, wlr_ref, wm_ref, wgk_ref, bgk_ref,
                   h_ref, qg_ref, kg_ref, vg_ref, gg_ref, la_ref, qm_ref, km_ref, vm_ref):
    x = x_ref[...]
    h = x + 0.5 * _swiglu(_rms(x, n1_ref[...]).astype(BF16), wgu_ref, wd_ref)
    h_ref[...] = h
    n = _rms(h, n2_ref[...]).astype(BF16)
    ua = jnp.dot(n, wa_ref[...], preferred_element_type=F32)
    qg_ref[...] = ua[:, 0:GLA_QK]
    kg_ref[...] = ua[:, GLA_QK:2 * GLA_QK]
    vg_ref[...] = ua[:, 2 * GLA_QK:2 * GLA_QK + GLA_V]
    gg_ref[...] = ua[:, 2 * GLA_QK + GLA_V:]
    lr = jnp.dot(n, wlr_ref[...], preferred_element_type=F32)
    z = jnp.dot(lr.astype(BF16), wgk_ref[...], preferred_element_type=F32) + bgk_ref[...]
    la_ref[...] = (jnp.minimum(z, 0.0) - jnp.log1p(jnp.exp(-jnp.abs(z)))) * (1.0 / GLA_TAU)
    um = jnp.dot(n, wm_ref[...], preferred_element_type=F32)
    qm_ref[...] = um[:, 0:MOBA_W]
    km_ref[...] = um[:, MOBA_W:2 * MOBA_W]
    vm_ref[...] = um[:, 2 * MOBA_W:]


def _ffn_in(x, n1, wgu, wd, n2, wa, wlr, wm, wgk, bgk):
    t, d = x.shape
    tm = min(ROW_TILE, t)
    row = lambda w: pl.BlockSpec((tm, w), lambda i: (i, 0))
    outs = [(d, F32), (GLA_QK, F32), (GLA_QK, F32), (GLA_V, F32), (GLA_V, F32), (GLA_QK, F32),
            (MOBA_W, F32), (MOBA_W, F32), (MOBA_W, F32)]
    return pl.pallas_call(
        _ffn_in_kernel,
        grid=(t // tm,),
        in_specs=[row(d)] + [_const_spec(a.shape) for a in (n1, wgu, wd, n2, wa, wlr, wm, wgk, bgk)],
        out_specs=[row(w) for w, _ in outs],
        out_shape=[jax.ShapeDtypeStruct((t, w), dt) for w, dt in outs],
        compiler_params=pltpu.CompilerParams(dimension_semantics=("parallel",), vmem_limit_bytes=VMEM_LIMIT),
        name="ffn_in",
    )(x, n1, wgu, wd, n2, wa, wlr, wm, wgk, bgk)


def _gla_kernel(q_ref, k_ref, la_ref, v_ref, g_ref, gn_ref, o_ref, st_ref):
    @pl.when(pl.program_id(0) == 0)
    def _():
        st_ref[...] = jnp.zeros_like(st_ref)

    c = GLA_CHUNK
    ri = lax.broadcasted_iota(jnp.int32, (c, c), 0)
    ci = lax.broadcasted_iota(jnp.int32, (c, c), 1)
    causal = ri >= ci
    tril = jnp.where(causal, 1.0, 0.0).astype(BF16)
    gn = gn_ref[...]
    for ch in range(q_ref.shape[0] // c):
        rows = slice(ch * c, (ch + 1) * c)
        la = la_ref[rows, :]
        la_hi = la.astype(BF16)
        la_lo = (la - la_hi.astype(F32)).astype(BF16)
        b = jnp.dot(tril, la_hi, preferred_element_type=F32) + jnp.dot(tril, la_lo, preferred_element_type=F32)
        b_last = b[c - 1:c, :]
        q = q_ref[rows, :] * (GLA_DK ** -0.5)
        k = k_ref[rows, :]
        qe = q * jnp.exp(b)
        ki = k * jnp.exp(-b)
        ke = k * jnp.exp(b_last - b)
        a_last = jnp.exp(b_last)
        for h in range(GLA_HEADS):
            ks = slice(h * GLA_DK, (h + 1) * GLA_DK)
            vs = slice(h * GLA_DV, (h + 1) * GLA_DV)
            v = v_ref[rows, vs].astype(BF16)
            st = st_ref[h]
            qeh = qe[:, ks].astype(BF16)
            attn = lax.dot_general(qeh, ki[:, ks].astype(BF16), (((1,), (1,)), ((), ())),
                                   preferred_element_type=F32)
            attn = jnp.where(causal, attn, 0.0).astype(BF16)
            o = jnp.dot(attn, v, preferred_element_type=F32)
            o = o + lax.dot_general(qeh, st.astype(BF16), (((1,), (1,)), ((), ())), preferred_element_type=F32)
            st_ref[h] = st * a_last[:, ks] + lax.dot_general(v, ke[:, ks].astype(BF16), (((0,), (0,)), ((), ())),
                                                             preferred_element_type=F32)
            g = g_ref[rows, vs]
            o_ref[rows, vs] = _rms(o, gn) * (g * jax.nn.sigmoid(g))


def _gla(qg, kg, la, vg, gg, gn):
    t = qg.shape[0]
    tb = min(GLA_TILE, t)
    row = lambda w: pl.BlockSpec((tb, w), lambda i: (i, 0))
    return pl.pallas_call(
        _gla_kernel,
        grid=(t // tb,),
        in_specs=[row(GLA_QK), row(GLA_QK), row(GLA_QK), row(GLA_V), row(GLA_V), _const_spec(gn.shape)],
        out_specs=row(GLA_V),
        out_shape=jax.ShapeDtypeStruct((t, GLA_V), F32),
        scratch_shapes=[pltpu.VMEM((GLA_HEADS, GLA_DV, GLA_DK), F32)],
        compiler_params=pltpu.CompilerParams(dimension_semantics=("arbitrary",), vmem_limit_bytes=VMEM_LIMIT),
        name="gla",
    )(qg, kg, la, vg, gg, gn)


def _moba_bias_kernel(tab_ref, o_ref):
    h = pl.program_id(0)
    delta = pl.program_id(1)
    blk = MOBA_BLOCK
    x = lax.broadcasted_iota(jnp.int32, (8, 2 * blk), 1)
    dist = delta * blk + x - blk
    n = jnp.maximum(dist, 0)
    max_exact = REL_BUCKETS // 2
    nf = jnp.maximum(n, 1).astype(F32)
    large = max_exact + (jnp.log(nf / max_exact) / math.log(REL_MAX_DIST / max_exact)
                         * (REL_BUCKETS - max_exact)).astype(jnp.int32)
    large = jnp.minimum(large, REL_BUCKETS - 1)
    bucket = jnp.where(n < max_exact, n, large)
    bias = jnp.zeros(dist.shape, F32)
    for b in range(REL_BUCKETS):
        bias = jnp.where(bucket == b, tab_ref[b * MOBA_HEADS + h], bias)
    by_dist = jnp.where(dist >= 0, bias * LOG2E, NEG)
    rows = jnp.broadcast_to(by_dist[0:1], (blk, 2 * blk))
    o_ref[0, 0] = pltpu.roll(rows, 0, 1, stride=1, stride_axis=0)[:, blk:].astype(o_ref.dtype)


def _moba_bias(rel_bias):
    blk = MOBA_BLOCK
    return pl.pallas_call(
        _moba_bias_kernel,
        grid=(MOBA_HEADS, BIAS_TILES),
        in_specs=[pl.BlockSpec(memory_space=pltpu.SMEM)],
        out_specs=pl.BlockSpec((1, 1, blk, blk), lambda h, d: (h, d, 0, 0)),
        out_shape=jax.ShapeDtypeStruct((MOBA_HEADS, BIAS_TILES, blk, blk), BF16),
        compiler_params=pltpu.CompilerParams(dimension_semantics=("parallel", "parallel")),
        name="moba_bias",
    )(rel_bias.reshape(-1))


def _moba_prep_kernel(q_ref, k_ref, v_ref, qt_ref, ka_ref, vt_ref, km_ref):
    t = pl.program_id(1)
    blk = MOBA_BLOCK
    rows = PREP_BLOCKS * blk
    hd = MOBA_HD

    @pl.when(t == 0)
    def _():
        km_ref[...] = jnp.zeros_like(km_ref)

    k = k_ref[...]
    km_ref[pl.ds(pl.multiple_of(t * PREP_BLOCKS, PREP_BLOCKS), PREP_BLOCKS), :] = jnp.mean(
        k.reshape(PREP_BLOCKS, blk, LANES), axis=1)
    km = km_ref[...]
    km_lane = lax.broadcasted_iota(jnp.int32, km.shape, 1)
    q_t = q_ref[...].T
    v_t = v_ref[...].T
    block = lax.broadcasted_iota(jnp.int32, (hd, rows), 0)
    block_f = block.astype(F32)
    own = t * PREP_BLOCKS + lax.broadcasted_iota(jnp.int32, (hd, rows), 1) // blk
    lane = lax.broadcasted_iota(jnp.int32, (rows, LANES), 1)
    own_k = t * PREP_BLOCKS + lax.broadcasted_iota(jnp.int32, (rows, LANES), 0) // blk
    ones_rows = jnp.where(lax.broadcasted_iota(jnp.int32, (V_ROWS - hd, rows), 0) == 0, 1.0, 0.0)
    for a in range(2):
        gate = jnp.dot(jnp.where(km_lane // hd == a, km, 0.0), q_t, precision=lax.Precision.HIGHEST,
                       preferred_element_type=F32)
        gate = jnp.where(block < own, gate, -jnp.inf)
        chosen = jnp.where(block == own, 1.0, 0.0)
        for _ in range(MOBA_TOPK):
            top = jnp.max(gate, axis=0, keepdims=True)
            first = jnp.min(jnp.where(gate == top, block_f, float(hd)), axis=0, keepdims=True)
            hit = block_f == first
            chosen = jnp.where(hit, 1.0, chosen)
            gate = jnp.where(hit, -jnp.inf, gate)
        q_a = q_t[a * hd:(a + 1) * hd] * (hd ** -0.5 * LOG2E)
        q_aug_t = jnp.concatenate([q_a, chosen - 1.0] if a == 0 else [chosen - 1.0, q_a], axis=0).astype(BF16)
        v_aug_t = jnp.concatenate([v_t[a * hd:(a + 1) * hd], ones_rows], axis=0).astype(BF16)
        k_aug = jnp.where(lane // hd == a, k, jnp.where(lane - (1 - a) * hd == own_k, -NEG, 0.0)).astype(BF16)
        for b in range(PREP_BLOCKS):
            qt_ref[a, b] = q_aug_t[:, b * blk:(b + 1) * blk]
            ka_ref[a, b] = k_aug[b * blk:(b + 1) * blk, :]
        group = KV_GROUP * blk
        for g in range(PREP_BLOCKS // KV_GROUP):
            vt_ref[a, g] = v_aug_t[:, g * group:(g + 1) * group]


def _moba_prep(qm, km, vm):
    t = qm.shape[0]
    blk = MOBA_BLOCK
    nb = t // blk
    assert nb <= MOBA_HD
    rows = PREP_BLOCKS * blk
    pair = pl.BlockSpec((rows, LANES), lambda p, i: (i, p))
    t_spec = pl.BlockSpec((2, PREP_BLOCKS, LANES, blk), lambda p, i: (p, i, 0, 0))
    k_spec = pl.BlockSpec((2, PREP_BLOCKS, blk, LANES), lambda p, i: (p, i, 0, 0))
    v_spec = pl.BlockSpec((2, PREP_BLOCKS // KV_GROUP, V_ROWS, KV_GROUP * blk), lambda p, i: (p, i, 0, 0))
    return pl.pallas_call(
        _moba_prep_kernel,
        grid=(MOBA_HEADS // 2, nb // PREP_BLOCKS),
        in_specs=[pair, pair, pair],
        out_specs=[t_spec, k_spec, v_spec],
        out_shape=[jax.ShapeDtypeStruct((MOBA_HEADS, nb, LANES, blk), BF16),
                   jax.ShapeDtypeStruct((MOBA_HEADS, nb, blk, LANES), BF16),
                   jax.ShapeDtypeStruct((MOBA_HEADS, nb // KV_GROUP, V_ROWS, KV_GROUP * blk), BF16)],
        scratch_shapes=[pltpu.VMEM((MOBA_HD, LANES), F32)],
        compiler_params=pltpu.CompilerParams(dimension_semantics=("parallel", "arbitrary"),
                                             vmem_limit_bytes=VMEM_LIMIT),
        name="moba_prep",
    )(qm, km, vm)


def _moba_attn_kernel(qi_ref, gi_ref, qt_ref, ka_ref, vt_ref, bias_ref, o_ref, s0_ref, s1_ref, p0_ref, p1_ref,
                      acc_ref):
    n_pairs = qi_ref.shape[0] - 2
    blk = MOBA_BLOCK
    s_refs = (s0_ref, s1_ref)
    p_refs = (p0_ref, p1_ref)

    def scores(t, slot):
        s_ref = s_refs[slot]
        qi = qi_ref[t]
        gi = gi_ref[t]
        qt = qt_ref[0, qi]
        col_max = None
        for b in range(KV_GROUP):
            j = gi * KV_GROUP + b
            s = jnp.dot(ka_ref[0, j], qt, preferred_element_type=F32).astype(BF16)
            s = s + bias_ref[0, jnp.clip(qi - j, 0, BIAS_TILES - 1)]
            s_ref[b] = s
            cm = jnp.max(s, axis=0, keepdims=True)
            col_max = cm if col_max is None else jnp.maximum(col_max, cm)
        return col_max.astype(F32)

    def softmax(t, slot, m_run, col_max):
        m_prev = jnp.where(gi_ref[t] == 0, -jnp.inf, m_run)
        m = jnp.maximum(m_prev, col_max)
        for b in range(KV_GROUP):
            p_refs[slot][b * blk:(b + 1) * blk, :] = jnp.exp2(s_refs[slot][b] - m.astype(BF16))
        return m, jnp.exp2(m_prev - m)

    def values(t, slot, alpha):
        acc = acc_ref[...] * alpha + jnp.dot(vt_ref[0, gi_ref[t]], p_refs[slot][...], preferred_element_type=F32)
        acc_ref[...] = acc
        o_ref[0, qi_ref[t]] = (acc[0:MOBA_HD] / acc[MOBA_HD:MOBA_HD + 1]).astype(o_ref.dtype)

    acc_ref[...] = jnp.zeros_like(acc_ref)
    cm0 = scores(0, 0)
    cm1 = scores(1, 1)
    m0, alpha0 = softmax(0, 0, jnp.full((1, blk), -jnp.inf, F32), cm0)

    def step(u, carry):
        m, alpha, cm = carry
        for i in range(ATTN_UNROLL):
            t = ATTN_UNROLL * u + i
            cm_next = scores(t + 2, i % 2)
            m, alpha_next = softmax(t + 1, 1 - i % 2, m, cm)
            values(t, i % 2, alpha)
            alpha, cm = alpha_next, cm_next
        return m, alpha, cm

    assert n_pairs % ATTN_UNROLL == 0 and ATTN_UNROLL % 2 == 0
    lax.fori_loop(0, n_pairs // ATTN_UNROLL, step, (m0, alpha0, cm1))


def _moba_attn(qt, ka, vt, bias):
    nb = qt.shape[1]
    blk = MOBA_BLOCK
    pairs = [(qi, g) for qi in range(nb) for g in range(qi // KV_GROUP + 1)] + [(0, 0), (0, 0)]
    qi_tab = jnp.asarray([a for a, _ in pairs], jnp.int32)
    gi_tab = jnp.asarray([b for _, b in pairs], jnp.int32)
    head = lambda shape: pl.BlockSpec((1,) + shape, lambda h, *_: (h,) + (0,) * len(shape))
    return pl.pallas_call(
        _moba_attn_kernel,
        grid_spec=pltpu.PrefetchScalarGridSpec(
            num_scalar_prefetch=2,
            grid=(MOBA_HEADS,),
            in_specs=[head((nb, LANES, blk)), head((nb, blk, LANES)),
                      head((nb // KV_GROUP, V_ROWS, KV_GROUP * blk)), head((BIAS_TILES, blk, blk))],
            out_specs=head((nb, MOBA_HD, blk)),
            scratch_shapes=[pltpu.VMEM((KV_GROUP, blk, blk), BF16), pltpu.VMEM((KV_GROUP, blk, blk), BF16),
                            pltpu.VMEM((KV_GROUP * blk, blk), BF16), pltpu.VMEM((KV_GROUP * blk, blk), BF16),
                            pltpu.VMEM((V_ROWS, blk), F32)]),
        out_shape=jax.ShapeDtypeStruct((MOBA_HEADS, nb, MOBA_HD, blk), BF16),
        compiler_params=pltpu.CompilerParams(dimension_semantics=("parallel",), vmem_limit_bytes=VMEM_LIMIT),
        name="moba_attn",
    )(qi_tab, gi_tab, qt, ka, vt, bias)


def _out_ffn_kernel(final, h_ref, og_ref, om_ref, p_ref, wog_ref, wom_ref, n3_ref, wgu_ref, wd_ref,
                    n4_ref, wpg_ref, wpp_ref, nf_ref, o_ref):
    mix = jnp.dot(og_ref[...].astype(BF16), wog_ref[...], preferred_element_type=F32)
    mix = mix + jnp.dot(om_ref[...], wom_ref[...], preferred_element_type=F32)
    h = h_ref[...] + mix
    h = h + 0.5 * _swiglu(_rms(h, n3_ref[...]).astype(BF16), wgu_ref, wd_ref)
    gate = jax.nn.sigmoid(jnp.dot(_rms(h, n4_ref[...]).astype(BF16), wpg_ref[...], preferred_element_type=F32))
    h = h + gate * jnp.dot(p_ref[...].astype(BF16), wpp_ref[...], preferred_element_type=F32)
    o_ref[...] = _rms(h, nf_ref[...]) if final else h


def _out_ffn(final, h, og, om, p, wog, wom, n3, wgu, wd, n4, wpg, wpp, nf):
    t, d = h.shape
    tm = min(ROW_TILE, t)
    row = lambda w: pl.BlockSpec((tm, w), lambda i: (i, 0))
    consts = (wog, wom, n3, wgu, wd, n4, wpg, wpp, nf)
    return pl.pallas_call(
        functools.partial(_out_ffn_kernel, final),
        grid=(t // tm,),
        in_specs=[row(d), row(GLA_V), row(MOBA_W), row(p.shape[1])]
        + [_const_spec(a.shape) for a in consts],
        out_specs=row(d),
        out_shape=jax.ShapeDtypeStruct((t, d), F32),
        compiler_params=pltpu.CompilerParams(dimension_semantics=("parallel",), vmem_limit_bytes=VMEM_LIMIT),
        name="out_ffn",
    )(h, og, om, p, *consts)


def kernel(x, p, ffn1_norm, ffn1_w_gu, ffn1_w_down, mix_norm, w_in, w_gk_up, b_gk, gla_out_norm, w_out,
           ffn2_norm, ffn2_w_gu, ffn2_w_down, ple_norm, w_ple_gate, w_ple_proj, rel_bias, final_norm):
    batch, t, d = x.shape
    depth = p.shape[0]
    assert batch == 1 and t % (PREP_BLOCKS * MOBA_BLOCK) == 0
    row = lambda a: a.reshape(1, -1)
    bias = _moba_bias(rel_bias)
    lr0 = 2 * GLA_QK + 2 * GLA_V
    h = x[0]
    for i in range(depth):
        wi = w_in[i]
        wa = wi[:, :lr0].astype(BF16)
        wlr = jnp.pad(wi[:, lr0:lr0 + GLA_LOWRANK], ((0, 0), (0, LANES - GLA_LOWRANK))).astype(BF16)
        wm = wi[:, lr0 + GLA_LOWRANK:].astype(BF16)
        wgk = jnp.pad(w_gk_up[i], ((0, LANES - GLA_LOWRANK), (0, 0))).astype(BF16)
        h1, qg, kg, vg, gg, la, qm, km, vm = _ffn_in(
            h, row(ffn1_norm[i]), ffn1_w_gu[i].astype(BF16), ffn1_w_down[i].astype(BF16), row(mix_norm[i]),
            wa, wlr, wm, wgk, row(b_gk[i]))
        og = _gla(qg, kg, la, vg, gg, row(gla_out_norm[i]))
        qt, ka, vt = _moba_prep(qm, km, vm)
        om = _moba_attn(qt, ka, vt, bias).transpose(1, 3, 0, 2).reshape(t, MOBA_W)
        wo = w_out[i].astype(BF16)
        h = _out_ffn(i == depth - 1, h1, og, om, p[i, 0], wo[:GLA_V], wo[GLA_V:], row(ffn2_norm[i]),
                     ffn2_w_gu[i].astype(BF16), ffn2_w_down[i].astype(BF16), row(ple_norm[i]),
                     w_ple_gate[i].astype(BF16), w_ple_proj[i].astype(BF16), row(final_norm))
    return h[None]
```

```python
import functools
import math

import jax
import jax.numpy as jnp
from jax import lax
from jax.experimental import pallas as pl
from jax.experimental.pallas import tpu as pltpu

F32 = jnp.float32
BF16 = jnp.bfloat16

GLA_HEADS = 4
GLA_DK = 64
GLA_DV = 128
GLA_LOWRANK = 16
GLA_TAU = 16.0
GLA_CHUNK = 64
MOBA_HEADS = 8
MOBA_HD = 64
MOBA_BLOCK = 256
MOBA_TOPK = 3
REL_BUCKETS = 32
REL_MAX_DIST = 4096
EPS = 1e-6
NEG = -1e30

GLA_QK = GLA_HEADS * GLA_DK
GLA_V = GLA_HEADS * GLA_DV
MOBA_W = MOBA_HEADS * MOBA_HD

LANES = 128
VMEM_LIMIT = 56 * 1024 * 1024

ROW_TILE = 512
GLA_TILE = 256
PREP_BLOCKS = 8
BIAS_TILES = 14
KV_GROUP = 2
ATTN_UNROLL = 32
V_ROWS = 80
LOG2E = 1.4426950408889634


def _rms(x, g):
    return x * lax.rsqrt(jnp.mean(x * x, axis=-1, keepdims=True) + EPS) * g


def _ff_chunks(d_ff):
    step = 1024
    return [(s, min(step, d_ff - s)) for s in range(0, d_ff, step)]


def _swiglu(xn, wgu_ref, wd_ref):
    d_ff = wd_ref.shape[0]
    acc = None
    for start, size in _ff_chunks(d_ff):
        gate = jnp.dot(xn, wgu_ref[:, start:start + size], preferred_element_type=F32)
        up = jnp.dot(xn, wgu_ref[:, d_ff + start:d_ff + start + size], preferred_element_type=F32)
        act = (gate * jax.nn.sigmoid(gate) * up).astype(BF16)
        part = jnp.dot(act, wd_ref[start:start + size, :], preferred_element_type=F32)
        acc = part if acc is None else acc + part
    return acc


def _const_spec(shape):
    return pl.BlockSpec(shape, lambda *_: (0,) * len(shape))


def _ffn_in_kernel(x_ref, n1_ref, wgu_ref, wd_ref, n2_ref, wa_ref, wlr_ref, wm_ref, wgk_ref, bgk_ref,
                   h_ref, qg_ref, kg_ref, vg_ref, gg_ref, la_ref, qm_ref, km_ref, vm_ref):
    x = x_ref[...]
    h = x + 0.5 * _swiglu(_rms(x, n1_ref[...]).astype(BF16), wgu_ref, wd_ref)
    h_ref[...] = h
    n = _rms(h, n2_ref[...]).astype(BF16)
    ua = jnp.dot(n, wa_ref[...], preferred_element_type=F32)
    qg_ref[...] = ua[:, 0:GLA_QK]
    kg_ref[...] = ua[:, GLA_QK:2 * GLA_QK]
    vg_ref[...] = ua[:, 2 * GLA_QK:2 * GLA_QK + GLA_V]
    gg_ref[...] = ua[:, 2 * GLA_QK + GLA_V:]
    lr = jnp.dot(n, wlr_ref[...], preferred_element_type=F32)
    z = jnp.dot(lr.astype(BF16), wgk_ref[...], preferred_element_type=F32) + bgk_ref[...]
    la_ref[...] = (jnp.minimum(z, 0.0) - jnp.log1p(jnp.exp(-jnp.abs(z)))) * (1.0 / GLA_TAU)
    um = jnp.dot(n, wm_ref[...], preferred_element_type=F32)
    qm_ref[...] = um[:, 0:MOBA_W]
    km_ref[...] = um[:, MOBA_W:2 * MOBA_W]
    vm_ref[...] = um[:, 2 * MOBA_W:]


def _ffn_in(x, n1, wgu, wd, n2, wa, wlr, wm, wgk, bgk):
    t, d = x.shape
    tm = min(ROW_TILE, t)
    row = lambda w: pl.BlockSpec((tm, w), lambda i: (i, 0))
    outs = [(d, F32), (GLA_QK, F32), (GLA_QK, F32), (GLA_V, F32), (GLA_V, F32), (GLA_QK, F32),
            (MOBA_W, F32), (MOBA_W, F32), (MOBA_W, F32)]
    return pl.pallas_call(
        _ffn_in_kernel,
        grid=(t // tm,),
        in_specs=[row(d)] + [_const_spec(a.shape) for a in (n1, wgu, wd, n2, wa, wlr, wm, wgk, bgk)],
        out_specs=[row(w) for w, _ in outs],
        out_shape=[jax.ShapeDtypeStruct((t, w), dt) for w, dt in outs],
        compiler_params=pltpu.CompilerParams(dimension_semantics=("parallel",), vmem_limit_bytes=VMEM_LIMIT),
        name="ffn_in",
    )(x, n1, wgu, wd, n2, wa, wlr, wm, wgk, bgk)


def _gla_kernel(q_ref, k_ref, la_ref, v_ref, g_ref, gn_ref, o_ref, st_ref):
    @pl.when(pl.program_id(0) == 0)
    def _():
        st_ref[...] = jnp.zeros_like(st_ref)

    c = GLA_CHUNK
    ri = lax.broadcasted_iota(jnp.int32, (c, c), 0)
    ci = lax.broadcasted_iota(jnp.int32, (c, c), 1)
    causal = ri >= ci
    tril = jnp.where(causal, 1.0, 0.0).astype(BF16)
    gn = gn_ref[...]
    for ch in range(q_ref.shape[0] // c):
        rows = slice(ch * c, (ch + 1) * c)
        la = la_ref[rows, :]
        la_hi = la.astype(BF16)
        la_lo = (la - la_hi.astype(F32)).astype(BF16)
        b = jnp.dot(tril, la_hi, preferred_element_type=F32) + jnp.dot(tril, la_lo, preferred_element_type=F32)
        b_last = b[c - 1:c, :]
        q = q_ref[rows, :] * (GLA_DK ** -0.5)
        k = k_ref[rows, :]
        qe = q * jnp.exp(b)
        ki = k * jnp.exp(-b)
        ke = k * jnp.exp(b_last - b)
        a_last = jnp.exp(b_last)
        for h in range(GLA_HEADS):
            ks = slice(h * GLA_DK, (h + 1) * GLA_DK)
            vs = slice(h * GLA_DV, (h + 1) * GLA_DV)
            v = v_ref[rows, vs].astype(BF16)
            st = st_ref[h]
            qeh = qe[:, ks].astype(BF16)
            attn = lax.dot_general(qeh, ki[:, ks].astype(BF16), (((1,), (1,)), ((), ())),
                                   preferred_element_type=F32)
            attn = jnp.where(causal, attn, 0.0).astype(BF16)
            o = jnp.dot(attn, v, preferred_element_type=F32)
            o = o + lax.dot_general(qeh, st.astype(BF16), (((1,), (1,)), ((), ())), preferred_element_type=F32)
            st_ref[h] = st * a_last[:, ks] + lax.dot_general(v, ke[:, ks].astype(BF16), (((0,), (0,)), ((), ())),
                                                             preferred_element_type=F32)
            g = g_ref[rows, vs]
            o_ref[rows, vs] = _rms(o, gn) * (g * jax.nn.sigmoid(g))


def _gla(qg, kg, la, vg, gg, gn):
    t = qg.shape[0]
    tb = min(GLA_TILE, t)
    row = lambda w: pl.BlockSpec((tb, w), lambda i: (i, 0))
    return pl.pallas_call(
        _gla_kernel,
        grid=(t // tb,),
        in_specs=[row(GLA_QK), row(GLA_QK), row(GLA_QK), row(GLA_V), row(GLA_V), _const_spec(gn.shape)],
        out_specs=row(GLA_V),
        out_shape=jax.ShapeDtypeStruct((t, GLA_V), F32),
        scratch_shapes=[pltpu.VMEM((GLA_HEADS, GLA_DV, GLA_DK), F32)],
        compiler_params=pltpu.CompilerParams(dimension_semantics=("arbitrary",), vmem_limit_bytes=VMEM_LIMIT),
        name="gla",
    )(qg, kg, la, vg, gg, gn)


def _moba_bias_kernel(tab_ref, o_ref):
    h = pl.program_id(0)
    blk = MOBA_BLOCK
    max_exact = REL_BUCKETS // 2
    x = lax.broadcasted_iota(jnp.int32, (8, 2 * blk), 1)
    for delta in range(BIAS_TILES):
        dist = delta * blk + x - blk
        n = jnp.maximum(dist, 0)
        nf = jnp.maximum(n, 1).astype(F32)
        large = max_exact + (jnp.log(nf / max_exact) / math.log(REL_MAX_DIST / max_exact)
                             * (REL_BUCKETS - max_exact)).astype(jnp.int32)
        large = jnp.minimum(large, REL_BUCKETS - 1)
        bucket = jnp.where(n < max_exact, n, large)
        bias = jnp.zeros(dist.shape, F32)
        for b in range(REL_BUCKETS):
            bias = jnp.where(bucket == b, tab_ref[b * MOBA_HEADS + h], bias)
        by_dist = jnp.where(dist >= 0, bias * LOG2E, NEG)
        rows = jnp.broadcast_to(by_dist[0:1], (blk, 2 * blk))
        o_ref[0, delta] = pltpu.roll(rows, 0, 1, stride=1, stride_axis=0)[:, blk:].astype(o_ref.dtype)


def _moba_bias(rel_bias):
    blk = MOBA_BLOCK
    return pl.pallas_call(
        _moba_bias_kernel,
        grid=(MOBA_HEADS,),
        in_specs=[pl.BlockSpec(memory_space=pltpu.SMEM)],
        out_specs=pl.BlockSpec((1, BIAS_TILES, blk, blk), lambda h: (h, 0, 0, 0)),
        out_shape=jax.ShapeDtypeStruct((MOBA_HEADS, BIAS_TILES, blk, blk), BF16),
        compiler_params=pltpu.CompilerParams(dimension_semantics=("parallel",)),
        name="moba_bias",
    )(rel_bias.reshape(-1))


def _moba_prep_kernel(q_ref, k_ref, v_ref, qt_ref, ka_ref, vt_ref, km_ref):
    t = pl.program_id(1)
    blk = MOBA_BLOCK
    rows = PREP_BLOCKS * blk
    hd = MOBA_HD

    @pl.when(t == 0)
    def _():
        km_ref[...] = jnp.zeros_like(km_ref)

    k = k_ref[...]
    km_ref[pl.ds(pl.multiple_of(t * PREP_BLOCKS, PREP_BLOCKS), PREP_BLOCKS), :] = jnp.mean(
        k.reshape(PREP_BLOCKS, blk, LANES), axis=1)
    km = km_ref[...]
    km_lane = lax.broadcasted_iota(jnp.int32, km.shape, 1)
    q_t = q_ref[...].T
    v_t = v_ref[...].T
    block = lax.broadcasted_iota(jnp.int32, (hd, rows), 0)
    block_f = block.astype(F32)
    own = t * PREP_BLOCKS + lax.broadcasted_iota(jnp.int32, (hd, rows), 1) // blk
    lane = lax.broadcasted_iota(jnp.int32, (rows, LANES), 1)
    own_k = t * PREP_BLOCKS + lax.broadcasted_iota(jnp.int32, (rows, LANES), 0) // blk
    ones_rows = jnp.where(lax.broadcasted_iota(jnp.int32, (V_ROWS - hd, rows), 0) == 0, 1.0, 0.0)
    for a in range(2):
        gate = jnp.dot(jnp.where(km_lane // hd == a, km, 0.0), q_t, precision=lax.Precision.HIGHEST,
                       preferred_element_type=F32)
        gate = jnp.where(block < own, gate, -jnp.inf)
        chosen = jnp.where(block == own, 1.0, 0.0)
        for _ in range(MOBA_TOPK):
            top = jnp.max(gate, axis=0, keepdims=True)
            first = jnp.min(jnp.where(gate == top, block_f, float(hd)), axis=0, keepdims=True)
            hit = block_f == first
            chosen = jnp.where(hit, 1.0, chosen)
            gate = jnp.where(hit, -jnp.inf, gate)
        q_a = q_t[a * hd:(a + 1) * hd] * (hd ** -0.5 * LOG2E)
        q_aug_t = jnp.concatenate([q_a, chosen - 1.0] if a == 0 else [chosen - 1.0, q_a], axis=0).astype(BF16)
        v_aug_t = jnp.concatenate([v_t[a * hd:(a + 1) * hd], ones_rows], axis=0).astype(BF16)
        k_aug = jnp.where(lane // hd == a, k, jnp.where(lane - (1 - a) * hd == own_k, -NEG, 0.0)).astype(BF16)
        for b in range(PREP_BLOCKS):
            qt_ref[a, b] = q_aug_t[:, b * blk:(b + 1) * blk]
            ka_ref[a, b] = k_aug[b * blk:(b + 1) * blk, :]
        group = KV_GROUP * blk
        for g in range(PREP_BLOCKS // KV_GROUP):
            vt_ref[a, g] = v_aug_t[:, g * group:(g + 1) * group]


def _moba_prep(qm, km, vm):
    t = qm.shape[0]
    blk = MOBA_BLOCK
    nb = t // blk
    assert nb <= MOBA_HD
    rows = PREP_BLOCKS * blk
    pair = pl.BlockSpec((rows, LANES), lambda p, i: (i, p))
    t_spec = pl.BlockSpec((2, PREP_BLOCKS, LANES, blk), lambda p, i: (p, i, 0, 0))
    k_spec = pl.BlockSpec((2, PREP_BLOCKS, blk, LANES), lambda p, i: (p, i, 0, 0))
    v_spec = pl.BlockSpec((2, PREP_BLOCKS // KV_GROUP, V_ROWS, KV_GROUP * blk), lambda p, i: (p, i, 0, 0))
    return pl.pallas_call(
        _moba_prep_kernel,
        grid=(MOBA_HEADS // 2, nb // PREP_BLOCKS),
        in_specs=[pair, pair, pair],
        out_specs=[t_spec, k_spec, v_spec],
        out_shape=[jax.ShapeDtypeStruct((MOBA_HEADS, nb, LANES, blk), BF16),
                   jax.ShapeDtypeStruct((MOBA_HEADS, nb, blk, LANES), BF16),
                   jax.ShapeDtypeStruct((MOBA_HEADS, nb // KV_GROUP, V_ROWS, KV_GROUP * blk), BF16)],
        scratch_shapes=[pltpu.VMEM((MOBA_HD, LANES), F32)],
        compiler_params=pltpu.CompilerParams(dimension_semantics=("parallel", "arbitrary"),
                                             vmem_limit_bytes=VMEM_LIMIT),
        name="moba_prep",
    )(qm, km, vm)


def _moba_attn_kernel(qi_ref, gi_ref, qt_ref, ka_ref, vt_ref, bias_ref, o_ref, s0_ref, s1_ref, p0_ref, p1_ref,
                      acc_ref):
    n_pairs = qi_ref.shape[0] - 2
    blk = MOBA_BLOCK
    s_refs = (s0_ref, s1_ref)
    p_refs = (p0_ref, p1_ref)

    def scores(t, slot):
        s_ref = s_refs[slot]
        qi = qi_ref[t]
        gi = gi_ref[t]
        qt = qt_ref[0, qi]
        col_max = None
        for b in range(KV_GROUP):
            j = gi * KV_GROUP + b
            s = jnp.dot(ka_ref[0, j], qt, preferred_element_type=F32).astype(BF16)
            s = s + bias_ref[0, jnp.clip(qi - j, 0, BIAS_TILES - 1)]
            s_ref[b] = s
            cm = jnp.max(s, axis=0, keepdims=True)
            col_max = cm if col_max is None else jnp.maximum(col_max, cm)
        return col_max.astype(F32)

    def softmax(t, slot, m_run, col_max):
        m_prev = jnp.where(gi_ref[t] == 0, -jnp.inf, m_run)
        m = jnp.maximum(m_prev, col_max)
        for b in range(KV_GROUP):
            p_refs[slot][b * blk:(b + 1) * blk, :] = jnp.exp2(s_refs[slot][b] - m.astype(BF16))
        return m, jnp.exp2(m_prev - m)

    def values(t, slot, alpha):
        acc = acc_ref[...] * alpha + jnp.dot(vt_ref[0, gi_ref[t]], p_refs[slot][...], preferred_element_type=F32)
        acc_ref[...] = acc
        o_ref[0, qi_ref[t]] = (acc[0:MOBA_HD] / acc[MOBA_HD:MOBA_HD + 1]).astype(o_ref.dtype)

    acc_ref[...] = jnp.zeros_like(acc_ref)
    cm0 = scores(0, 0)
    cm1 = scores(1, 1)
    m0, alpha0 = softmax(0, 0, jnp.full((1, blk), -jnp.inf, F32), cm0)

    unroll = max(u for u in range(2, ATTN_UNROLL + 1, 2) if n_pairs % u == 0)

    def step(u, carry):
        m, alpha, cm = carry
        for i in range(unroll):
            t = unroll * u + i
            cm_next = scores(t + 2, i % 2)
            m, alpha_next = softmax(t + 1, 1 - i % 2, m, cm)
            values(t, i % 2, alpha)
            alpha, cm = alpha_next, cm_next
        return m, alpha, cm

    lax.fori_loop(0, n_pairs // unroll, step, (m0, alpha0, cm1))


def _moba_attn(qt, ka, vt, bias):
    nb = qt.shape[1]
    blk = MOBA_BLOCK
    pairs = [(qi, g) for qi in range(nb) for g in range(qi // KV_GROUP + 1)] + [(0, 0), (0, 0)]
    qi_tab = jnp.asarray([a for a, _ in pairs], jnp.int32)
    gi_tab = jnp.asarray([b for _, b in pairs], jnp.int32)
    head = lambda shape: pl.BlockSpec((1,) + shape, lambda h, *_: (h,) + (0,) * len(shape))
    return pl.pallas_call(
        _moba_attn_kernel,
        grid_spec=pltpu.PrefetchScalarGridSpec(
            num_scalar_prefetch=2,
            grid=(MOBA_HEADS,),
            in_specs=[head((nb, LANES, blk)), head((nb, blk, LANES)),
                      head((nb // KV_GROUP, V_ROWS, KV_GROUP * blk)), head((BIAS_TILES, blk, blk))],
            out_specs=head((nb, MOBA_HD, blk)),
            scratch_shapes=[pltpu.VMEM((KV_GROUP, blk, blk), BF16), pltpu.VMEM((KV_GROUP, blk, blk), BF16),
                            pltpu.VMEM((KV_GROUP * blk, blk), BF16), pltpu.VMEM((KV_GROUP * blk, blk), BF16),
                            pltpu.VMEM((V_ROWS, blk), F32)]),
        out_shape=jax.ShapeDtypeStruct((MOBA_HEADS, nb, MOBA_HD, blk), BF16),
        compiler_params=pltpu.CompilerParams(dimension_semantics=("parallel",), vmem_limit_bytes=VMEM_LIMIT),
        name="moba_attn",
    )(qi_tab, gi_tab, qt, ka, vt, bias)


def _out_ffn_kernel(final, h_ref, og_ref, om_ref, p_ref, wog_ref, wom_ref, n3_ref, wgu_ref, wd_ref,
                    n4_ref, wpg_ref, wpp_ref, nf_ref, o_ref):
    mix = jnp.dot(og_ref[...].astype(BF16), wog_ref[...], preferred_element_type=F32)
    mix = mix + jnp.dot(om_ref[...], wom_ref[...], preferred_element_type=F32)
    h = h_ref[...] + mix
    h = h + 0.5 * _swiglu(_rms(h, n3_ref[...]).astype(BF16), wgu_ref, wd_ref)
    gate = jax.nn.sigmoid(jnp.dot(_rms(h, n4_ref[...]).astype(BF16), wpg_ref[...], preferred_element_type=F32))
    h = h + gate * jnp.dot(p_ref[...].astype(BF16), wpp_ref[...], preferred_element_type=F32)
    o_ref[...] = _rms(h, nf_ref[...]) if final else h


def _out_ffn(final, h, og, om, p, wog, wom, n3, wgu, wd, n4, wpg, wpp, nf):
    t, d = h.shape
    tm = min(ROW_TILE, t)
    row = lambda w: pl.BlockSpec((tm, w), lambda i: (i, 0))
    consts = (wog, wom, n3, wgu, wd, n4, wpg, wpp, nf)
    return pl.pallas_call(
        functools.partial(_out_ffn_kernel, final),
        grid=(t // tm,),
        in_specs=[row(d), row(GLA_V), row(MOBA_W), row(p.shape[1])]
        + [_const_spec(a.shape) for a in consts],
        out_specs=row(d),
        out_shape=jax.ShapeDtypeStruct((t, d), F32),
        compiler_params=pltpu.CompilerParams(dimension_semantics=("parallel",), vmem_limit_bytes=VMEM_LIMIT),
        name="out_ffn",
    )(h, og, om, p, *consts)


def kernel(x, p, ffn1_norm, ffn1_w_gu, ffn1_w_down, mix_norm, w_in, w_gk_up, b_gk, gla_out_norm, w_out,
           ffn2_norm, ffn2_w_gu, ffn2_w_down, ple_norm, w_ple_gate, w_ple_proj, rel_bias, final_norm):
    batch, t, d = x.shape
    depth = p.shape[0]
    assert batch == 1 and t % (PREP_BLOCKS * MOBA_BLOCK) == 0
    row = lambda a: a.reshape(1, -1)
    bias = _moba_bias(rel_bias)
    lr0 = 2 * GLA_QK + 2 * GLA_V
    h = x[0]
    for i in range(depth):
        wi = w_in[i]
        wa = wi[:, :lr0].astype(BF16)
        wlr = jnp.pad(wi[:, lr0:lr0 + GLA_LOWRANK], ((0, 0), (0, LANES - GLA_LOWRANK))).astype(BF16)
        wm = wi[:, lr0 + GLA_LOWRANK:].astype(BF16)
        wgk = jnp.pad(w_gk_up[i], ((0, LANES - GLA_LOWRANK), (0, 0))).astype(BF16)
        h1, qg, kg, vg, gg, la, qm, km, vm = _ffn_in(
            h, row(ffn1_norm[i]), ffn1_w_gu[i].astype(BF16), ffn1_w_down[i].astype(BF16), row(mix_norm[i]),
            wa, wlr, wm, wgk, row(b_gk[i]))
        og = _gla(qg, kg, la, vg, gg, row(gla_out_norm[i]))
        qt, ka, vt = _moba_prep(qm, km, vm)
        om = _moba_attn(qt, ka, vt, bias).transpose(1, 3, 0, 2).reshape(t, MOBA_W)
        wo = w_out[i].astype(BF16)
        h = _out_ffn(i == depth - 1, h1, og, om, p[i, 0], wo[:GLA_V], wo[GLA_V:], row(ffn2_norm[i]),
                     ffn2_w_gu[i].astype(BF16), ffn2_w_down[i].astype(BF16), row(ple_norm[i]),
                     w_ple_gate[i].astype(BF16), w_ple_proj[i].astype(BF16), row(final_norm))
    return h[None]
```

```python
import functools
import math

import jax
import jax.numpy as jnp
import numpy as np
from jax import lax
from jax.experimental import pallas as pl
from jax.experimental.pallas import tpu as pltpu

F32 = jnp.float32
BF16 = jnp.bfloat16

GLA_HEADS = 4
GLA_DK = 64
GLA_DV = 128
GLA_LOWRANK = 16
GLA_TAU = 16.0
MOBA_HEADS = 8
MOBA_HD = 64
MOBA_BLOCK = 256
MOBA_TOPK = 3
REL_BUCKETS = 32
REL_MAX_DIST = 4096
EPS = 1e-6
NEG = -1e30

GLA_QK = GLA_HEADS * GLA_DK
GLA_V = GLA_HEADS * GLA_DV
MOBA_W = MOBA_HEADS * MOBA_HD

LANES = 128
VMEM_LIMIT = 56 * 1024 * 1024

ROW_TILE = 512
GLA_TILE = 256
GLA_LEVELS = (128, 64, 32, 16, 8, 4, 2, 1)
PREP_BLOCKS = 8
BIAS_TILES = 14
KV_GROUP = 2
ATTN_UNROLL = 32
V_ROWS = 80
LOG2E = 1.4426950408889634


def _rms(x, g):
    return x * lax.rsqrt(jnp.mean(x * x, axis=-1, keepdims=True) + EPS) * g


def _ff_chunks(d_ff):
    step = 1024
    return [(s, min(step, d_ff - s)) for s in range(0, d_ff, step)]


def _swiglu(xn, wgu_ref, wd_ref):
    d_ff = wd_ref.shape[0]
    acc = None
    for start, size in _ff_chunks(d_ff):
        gate = jnp.dot(xn, wgu_ref[:, start:start + size], preferred_element_type=F32)
        up = jnp.dot(xn, wgu_ref[:, d_ff + start:d_ff + start + size], preferred_element_type=F32)
        act = (gate * jax.nn.sigmoid(gate) * up).astype(BF16)
        part = jnp.dot(act, wd_ref[start:start + size, :], preferred_element_type=F32)
        acc = part if acc is None else acc + part
    return acc


def _const_spec(shape):
    return pl.BlockSpec(shape, lambda *_: (0,) * len(shape))


def _ffn_in_kernel(x_ref, n1_ref, wgu_ref, wd_ref, n2_ref, wa_ref, wlr_ref, wm_ref, wgk_ref, bgk_ref,
                   h_ref, qg_ref, kg_ref, vg_ref, gg_ref, la_ref, qm_ref, km_ref, vm_ref):
    x = x_ref[...]
    h = x + 0.5 * _swiglu(_rms(x, n1_ref[...]).astype(BF16), wgu_ref, wd_ref)
    h_ref[...] = h
    n = _rms(h, n2_ref[...]).astype(BF16)
    ua = jnp.dot(n, wa_ref[...], preferred_element_type=F32)
    qg_ref[...] = ua[:, 0:GLA_QK]
    kg_ref[...] = ua[:, GLA_QK:2 * GLA_QK]
    vg_ref[...] = ua[:, 2 * GLA_QK:2 * GLA_QK + GLA_V]
    gg_ref[...] = ua[:, 2 * GLA_QK + GLA_V:]
    lr = jnp.dot(n, wlr_ref[...], preferred_element_type=F32)
    z = jnp.dot(lr.astype(BF16), wgk_ref[...], preferred_element_type=F32) + bgk_ref[...]
    la_ref[...] = (jnp.minimum(z, 0.0) - jnp.log1p(jnp.exp(-jnp.abs(z)))) * (1.0 / GLA_TAU)
    um = jnp.dot(n, wm_ref[...], preferred_element_type=F32)
    qm_ref[...] = um[:, 0:MOBA_W]
    km_ref[...] = um[:, MOBA_W:2 * MOBA_W]
    vm_ref[...] = um[:, 2 * MOBA_W:]


def _ffn_in(x, n1, wgu, wd, n2, wa, wlr, wm, wgk, bgk):
    t, d = x.shape
    tm = min(ROW_TILE, t)
    row = lambda w: pl.BlockSpec((tm, w), lambda i: (i, 0))
    outs = [(d, F32), (GLA_QK, F32), (GLA_QK, F32), (GLA_V, F32), (GLA_V, F32), (GLA_QK, F32),
            (MOBA_W, F32), (MOBA_W, F32), (MOBA_W, F32)]
    return pl.pallas_call(
        _ffn_in_kernel,
        grid=(t // tm,),
        in_specs=[row(d)] + [_const_spec(a.shape) for a in (n1, wgu, wd, n2, wa, wlr, wm, wgk, bgk)],
        out_specs=[row(w) for w, _ in outs],
        out_shape=[jax.ShapeDtypeStruct((t, w), dt) for w, dt in outs],
        compiler_params=pltpu.CompilerParams(dimension_semantics=("parallel",), vmem_limit_bytes=VMEM_LIMIT),
        name="ffn_in",
    )(x, n1, wgu, wd, n2, wa, wlr, wm, wgk, bgk)


def _gla_span_matrix(tile):
    i = np.arange(tile)[:, None]
    t = np.arange(tile)[None, :]
    mats = [t <= i, t > i]
    for s in GLA_LEVELS:
        split = (i // (2 * s)) * (2 * s) + s - 1
        mats.append(np.where((i & s) != 0, (t > split) & (t <= i), (t > i) & (t <= split)))
    return jnp.asarray(np.concatenate(mats, axis=0), BF16)


def _gla_kernel(q_ref, k_ref, la_ref, v_ref, g_ref, gn_ref, span_ref, o_ref, st_ref):
    @pl.when(pl.program_id(0) == 0)
    def _():
        st_ref[...] = jnp.zeros_like(st_ref)

    tb = q_ref.shape[0]
    half = tb // 2
    la = la_ref[...]
    la_hi = la.astype(BF16)
    la_lo = (la - la_hi.astype(F32)).astype(BF16)

    def span_sum(r):
        m = span_ref[r * tb:(r + 1) * tb, :]
        return jnp.dot(m, la_hi, preferred_element_type=F32) + jnp.dot(m, la_lo, preferred_element_type=F32)

    b = span_sum(0)
    q = q_ref[...] * (GLA_DK ** -0.5)
    k = k_ref[...]
    qe = q * jnp.exp(b)
    ke = k * jnp.exp(span_sum(1))
    a_last = jnp.exp(b[tb - 1:tb, :])
    row = lax.broadcasted_iota(jnp.int32, (tb, GLA_QK), 0)
    q_lvl, k_lvl = [], []
    for idx, s in enumerate(GLA_LEVELS):
        e = jnp.exp(span_sum(2 + idx))
        upper = (row & s) != 0
        q_lvl.append(jnp.where(upper, q * e, 0.0))
        k_lvl.append(jnp.where(upper, 0.0, k * e))
    ri = lax.broadcasted_iota(jnp.int32, (half, half), 0)
    ci = lax.broadcasted_iota(jnp.int32, (half, half), 1)
    same_block = [ri // (2 * s) == ci // (2 * s) for s in GLA_LEVELS[1:]]
    diagonal = ri == ci
    gn = gn_ref[...]
    nt = (((1,), (1,)), ((), ()))
    for h in range(GLA_HEADS):
        ks = slice(h * GLA_DK, (h + 1) * GLA_DK)
        vs = slice(h * GLA_DV, (h + 1) * GLA_DV)

        def pair(x, y, r0, c0):
            return lax.dot_general(x[r0:r0 + half, ks].astype(BF16), y[c0:c0 + half, ks].astype(BF16), nt,
                                   preferred_element_type=F32)

        diag_blocks = []
        for c in (0, half):
            blk = jnp.where(diagonal, pair(q, k, c, c), 0.0)
            for idx in range(1, len(GLA_LEVELS)):
                blk = blk + jnp.where(same_block[idx - 1], pair(q_lvl[idx], k_lvl[idx], c, c), 0.0)
            diag_blocks.append(blk)
        low = pair(q_lvl[0], k_lvl[0], half, 0)
        attn = jnp.concatenate([jnp.concatenate([diag_blocks[0], jnp.zeros_like(low)], axis=1),
                                jnp.concatenate([low, diag_blocks[1]], axis=1)], axis=0).astype(BF16)
        v = v_ref[:, vs].astype(BF16)
        st = st_ref[h]
        o = jnp.dot(attn, v, preferred_element_type=F32)
        o = o + lax.dot_general(qe[:, ks].astype(BF16), st.astype(BF16), nt, preferred_element_type=F32)
        st_ref[h] = st * a_last[:, ks] + lax.dot_general(v, ke[:, ks].astype(BF16), (((0,), (0,)), ((), ())),
                                                         preferred_element_type=F32)
        g = g_ref[:, vs]
        o_ref[:, vs] = _rms(o, gn) * (g * jax.nn.sigmoid(g))


def _gla(qg, kg, la, vg, gg, gn):
    t = qg.shape[0]
    tb = GLA_TILE
    span = _gla_span_matrix(tb)
    row = lambda w: pl.BlockSpec((tb, w), lambda i: (i, 0))
    return pl.pallas_call(
        _gla_kernel,
        grid=(t // tb,),
        in_specs=[row(GLA_QK), row(GLA_QK), row(GLA_QK), row(GLA_V), row(GLA_V), _const_spec(gn.shape),
                  _const_spec(span.shape)],
        out_specs=row(GLA_V),
        out_shape=jax.ShapeDtypeStruct((t, GLA_V), F32),
        scratch_shapes=[pltpu.VMEM((GLA_HEADS, GLA_DV, GLA_DK), F32)],
        compiler_params=pltpu.CompilerParams(dimension_semantics=("arbitrary",), vmem_limit_bytes=VMEM_LIMIT),
        name="gla",
    )(qg, kg, la, vg, gg, gn, span)


def _moba_bias_kernel(tab_ref, o_ref):
    h = pl.program_id(0)
    blk = MOBA_BLOCK
    max_exact = REL_BUCKETS // 2
    x = lax.broadcasted_iota(jnp.int32, (8, 2 * blk), 1)
    for delta in range(BIAS_TILES):
        dist = delta * blk + x - blk
        n = jnp.maximum(dist, 0)
        nf = jnp.maximum(n, 1).astype(F32)
        large = max_exact + (jnp.log(nf / max_exact) / math.log(REL_MAX_DIST / max_exact)
                             * (REL_BUCKETS - max_exact)).astype(jnp.int32)
        large = jnp.minimum(large, REL_BUCKETS - 1)
        bucket = jnp.where(n < max_exact, n, large)
        bias = jnp.zeros(dist.shape, F32)
        for b in range(REL_BUCKETS):
            bias = jnp.where(bucket == b, tab_ref[b * MOBA_HEADS + h], bias)
        by_dist = jnp.where(dist >= 0, bias * LOG2E, NEG)
        rows = jnp.broadcast_to(by_dist[0:1], (blk, 2 * blk))
        o_ref[0, delta] = pltpu.roll(rows, 0, 1, stride=1, stride_axis=0)[:, blk:].astype(o_ref.dtype)


def _moba_bias(rel_bias):
    blk = MOBA_BLOCK
    return pl.pallas_call(
        _moba_bias_kernel,
        grid=(MOBA_HEADS,),
        in_specs=[pl.BlockSpec(memory_space=pltpu.SMEM)],
        out_specs=pl.BlockSpec((1, BIAS_TILES, blk, blk), lambda h: (h, 0, 0, 0)),
        out_shape=jax.ShapeDtypeStruct((MOBA_HEADS, BIAS_TILES, blk, blk), BF16),
        compiler_params=pltpu.CompilerParams(dimension_semantics=("parallel",)),
        name="moba_bias",
    )(rel_bias.reshape(-1))


def _moba_prep_kernel(q_ref, k_ref, v_ref, qt_ref, ka_ref, vt_ref, km_ref):
    t = pl.program_id(1)
    blk = MOBA_BLOCK
    rows = PREP_BLOCKS * blk
    hd = MOBA_HD

    @pl.when(t == 0)
    def _():
        km_ref[...] = jnp.zeros_like(km_ref)

    k = k_ref[...]
    km_ref[pl.ds(pl.multiple_of(t * PREP_BLOCKS, PREP_BLOCKS), PREP_BLOCKS), :] = jnp.mean(
        k.reshape(PREP_BLOCKS, blk, LANES), axis=1)
    km = km_ref[...]
    km_lane = lax.broadcasted_iota(jnp.int32, km.shape, 1)
    q_t = q_ref[...].T
    v_t = v_ref[...].T
    block = lax.broadcasted_iota(jnp.int32, (hd, rows), 0)
    block_f = block.astype(F32)
    own = t * PREP_BLOCKS + lax.broadcasted_iota(jnp.int32, (hd, rows), 1) // blk
    lane = lax.broadcasted_iota(jnp.int32, (rows, LANES), 1)
    own_k = t * PREP_BLOCKS + lax.broadcasted_iota(jnp.int32, (rows, LANES), 0) // blk
    ones_rows = jnp.where(lax.broadcasted_iota(jnp.int32, (V_ROWS - hd, rows), 0) == 0, 1.0, 0.0)
    for a in range(2):
        gate = jnp.dot(jnp.where(km_lane // hd == a, km, 0.0), q_t, precision=lax.Precision.HIGHEST,
                       preferred_element_type=F32)
        gate = jnp.where(block < own, gate, -jnp.inf)
        chosen = jnp.where(block == own, 1.0, 0.0)
        for _ in range(MOBA_TOPK):
            top = jnp.max(gate, axis=0, keepdims=True)
            first = jnp.min(jnp.where(gate == top, block_f, float(hd)), axis=0, keepdims=True)
            hit = block_f == first
            chosen = jnp.where(hit, 1.0, chosen)
            gate = jnp.where(hit, -jnp.inf, gate)
        q_a = q_t[a * hd:(a + 1) * hd] * (hd ** -0.5 * LOG2E)
        q_aug_t = jnp.concatenate([q_a, chosen - 1.0] if a == 0 else [chosen - 1.0, q_a], axis=0).astype(BF16)
        v_aug_t = jnp.concatenate([v_t[a * hd:(a + 1) * hd], ones_rows], axis=0).astype(BF16)
        k_aug = jnp.where(lane // hd == a, k, jnp.where(lane - (1 - a) * hd == own_k, -NEG, 0.0)).astype(BF16)
        for b in range(PREP_BLOCKS):
            qt_ref[a, b] = q_aug_t[:, b * blk:(b + 1) * blk]
            ka_ref[a, b] = k_aug[b * blk:(b + 1) * blk, :]
        group = KV_GROUP * blk
        for g in range(PREP_BLOCKS // KV_GROUP):
            vt_ref[a, g] = v_aug_t[:, g * group:(g + 1) * group]


def _moba_prep(qm, km, vm):
    t = qm.shape[0]
    blk = MOBA_BLOCK
    nb = t // blk
    assert nb <= MOBA_HD
    rows = PREP_BLOCKS * blk
    pair = pl.BlockSpec((rows, LANES), lambda p, i: (i, p))
    t_spec = pl.BlockSpec((2, PREP_BLOCKS, LANES, blk), lambda p, i: (p, i, 0, 0))
    k_spec = pl.BlockSpec((2, PREP_BLOCKS, blk, LANES), lambda p, i: (p, i, 0, 0))
    v_spec = pl.BlockSpec((2, PREP_BLOCKS // KV_GROUP, V_ROWS, KV_GROUP * blk), lambda p, i: (p, i, 0, 0))
    return pl.pallas_call(
        _moba_prep_kernel,
        grid=(MOBA_HEADS // 2, nb // PREP_BLOCKS),
        in_specs=[pair, pair, pair],
        out_specs=[t_spec, k_spec, v_spec],
        out_shape=[jax.ShapeDtypeStruct((MOBA_HEADS, nb, LANES, blk), BF16),
                   jax.ShapeDtypeStruct((MOBA_HEADS, nb, blk, LANES), BF16),
                   jax.ShapeDtypeStruct((MOBA_HEADS, nb // KV_GROUP, V_ROWS, KV_GROUP * blk), BF16)],
        scratch_shapes=[pltpu.VMEM((MOBA_HD, LANES), F32)],
        compiler_params=pltpu.CompilerParams(dimension_semantics=("parallel", "arbitrary"),
                                             vmem_limit_bytes=VMEM_LIMIT),
        name="moba_prep",
    )(qm, km, vm)


def _moba_attn_kernel(qi_ref, gi_ref, qt_ref, ka_ref, vt_ref, bias_ref, o_ref, s0_ref, s1_ref, p0_ref, p1_ref,
                      acc_ref):
    n_pairs = qi_ref.shape[0] - 2
    blk = MOBA_BLOCK
    s_refs = (s0_ref, s1_ref)
    p_refs = (p0_ref, p1_ref)

    def scores(t, slot):
        s_ref = s_refs[slot]
        qi = qi_ref[t]
        gi = gi_ref[t]
        qt = qt_ref[0, qi]
        col_max = None
        for b in range(KV_GROUP):
            j = gi * KV_GROUP + b
            s = jnp.dot(ka_ref[0, j], qt, preferred_element_type=F32).astype(BF16)
            s = s + bias_ref[0, jnp.clip(qi - j, 0, BIAS_TILES - 1)]
            s_ref[b] = s
            cm = jnp.max(s, axis=0, keepdims=True)
            col_max = cm if col_max is None else jnp.maximum(col_max, cm)
        return col_max.astype(F32)

    def softmax(t, slot, m_run, col_max):
        m_prev = jnp.where(gi_ref[t] == 0, -jnp.inf, m_run)
        m = jnp.maximum(m_prev, col_max)
        for b in range(KV_GROUP):
            p_refs[slot][b * blk:(b + 1) * blk, :] = jnp.exp2(s_refs[slot][b] - m.astype(BF16))
        return m, jnp.exp2(m_prev - m)

    def values(t, slot, alpha):
        acc = acc_ref[...] * alpha + jnp.dot(vt_ref[0, gi_ref[t]], p_refs[slot][...], preferred_element_type=F32)
        acc_ref[...] = acc
        o_ref[0, qi_ref[t]] = (acc[0:MOBA_HD] / acc[MOBA_HD:MOBA_HD + 1]).astype(o_ref.dtype)

    acc_ref[...] = jnp.zeros_like(acc_ref)
    cm0 = scores(0, 0)
    cm1 = scores(1, 1)
    m0, alpha0 = softmax(0, 0, jnp.full((1, blk), -jnp.inf, F32), cm0)

    unroll = max(u for u in range(2, ATTN_UNROLL + 1, 2) if n_pairs % u == 0)

    def step(u, carry):
        m, alpha, cm = carry
        for i in range(unroll):
            t = unroll * u + i
            cm_next = scores(t + 2, i % 2)
            m, alpha_next = softmax(t + 1, 1 - i % 2, m, cm)
            values(t, i % 2, alpha)
            alpha, cm = alpha_next, cm_next
        return m, alpha, cm

    lax.fori_loop(0, n_pairs // unroll, step, (m0, alpha0, cm1))


def _moba_attn(qt, ka, vt, bias):
    nb = qt.shape[1]
    blk = MOBA_BLOCK
    pairs = [(qi, g) for qi in range(nb) for g in range(qi // KV_GROUP + 1)] + [(0, 0), (0, 0)]
    qi_tab = jnp.asarray([a for a, _ in pairs], jnp.int32)
    gi_tab = jnp.asarray([b for _, b in pairs], jnp.int32)
    head = lambda shape: pl.BlockSpec((1,) + shape, lambda h, *_: (h,) + (0,) * len(shape))
    return pl.pallas_call(
        _moba_attn_kernel,
        grid_spec=pltpu.PrefetchScalarGridSpec(
            num_scalar_prefetch=2,
            grid=(MOBA_HEADS,),
            in_specs=[head((nb, LANES, blk)), head((nb, blk, LANES)),
                      head((nb // KV_GROUP, V_ROWS, KV_GROUP * blk)), head((BIAS_TILES, blk, blk))],
            out_specs=head((nb, MOBA_HD, blk)),
            scratch_shapes=[pltpu.VMEM((KV_GROUP, blk, blk), BF16), pltpu.VMEM((KV_GROUP, blk, blk), BF16),
                            pltpu.VMEM((KV_GROUP * blk, blk), BF16), pltpu.VMEM((KV_GROUP * blk, blk), BF16),
                            pltpu.VMEM((V_ROWS, blk), F32)]),
        out_shape=jax.ShapeDtypeStruct((MOBA_HEADS, nb, MOBA_HD, blk), BF16),
        compiler_params=pltpu.CompilerParams(dimension_semantics=("parallel",), vmem_limit_bytes=VMEM_LIMIT),
        name="moba_attn",
    )(qi_tab, gi_tab, qt, ka, vt, bias)


def _out_ffn_kernel(final, h_ref, og_ref, om_ref, p_ref, wog_ref, wom_ref, n3_ref, wgu_ref, wd_ref,
                    n4_ref, wpg_ref, wpp_ref, nf_ref, o_ref):
    mix = jnp.dot(og_ref[...].astype(BF16), wog_ref[...], preferred_element_type=F32)
    mix = mix + jnp.dot(om_ref[...], wom_ref[...], preferred_element_type=F32)
    h = h_ref[...] + mix
    h = h + 0.5 * _swiglu(_rms(h, n3_ref[...]).astype(BF16), wgu_ref, wd_ref)
    gate = jax.nn.sigmoid(jnp.dot(_rms(h, n4_ref[...]).astype(BF16), wpg_ref[...], preferred_element_type=F32))
    h = h + gate * jnp.dot(p_ref[...].astype(BF16), wpp_ref[...], preferred_element_type=F32)
    o_ref[...] = _rms(h, nf_ref[...]) if final else h


def _out_ffn(final, h, og, om, p, wog, wom, n3, wgu, wd, n4, wpg, wpp, nf):
    t, d = h.shape
    tm = min(ROW_TILE, t)
    row = lambda w: pl.BlockSpec((tm, w), lambda i: (i, 0))
    consts = (wog, wom, n3, wgu, wd, n4, wpg, wpp, nf)
    return pl.pallas_call(
        functools.partial(_out_ffn_kernel, final),
        grid=(t // tm,),
        in_specs=[row(d), row(GLA_V), row(MOBA_W), row(p.shape[1])]
        + [_const_spec(a.shape) for a in consts],
        out_specs=row(d),
        out_shape=jax.ShapeDtypeStruct((t, d), F32),
        compiler_params=pltpu.CompilerParams(dimension_semantics=("parallel",), vmem_limit_bytes=VMEM_LIMIT),
        name="out_ffn",
    )(h, og, om, p, *consts)


def kernel(x, p, ffn1_norm, ffn1_w_gu, ffn1_w_down, mix_norm, w_in, w_gk_up, b_gk, gla_out_norm, w_out,
           ffn2_norm, ffn2_w_gu, ffn2_w_down, ple_norm, w_ple_gate, w_ple_proj, rel_bias, final_norm):
    batch, t, d = x.shape
    depth = p.shape[0]
    assert batch == 1 and t % (PREP_BLOCKS * MOBA_BLOCK) == 0
    row = lambda a: a.reshape(1, -1)
    bias = _moba_bias(rel_bias)
    lr0 = 2 * GLA_QK + 2 * GLA_V
    h = x[0]
    for i in range(depth):
        wi = w_in[i]
        wa = wi[:, :lr0].astype(BF16)
        wlr = jnp.pad(wi[:, lr0:lr0 + GLA_LOWRANK], ((0, 0), (0, LANES - GLA_LOWRANK))).astype(BF16)
        wm = wi[:, lr0 + GLA_LOWRANK:].astype(BF16)
        wgk = jnp.pad(w_gk_up[i], ((0, LANES - GLA_LOWRANK), (0, 0))).astype(BF16)
        h1, qg, kg, vg, gg, la, qm, km, vm = _ffn_in(
            h, row(ffn1_norm[i]), ffn1_w_gu[i].astype(BF16), ffn1_w_down[i].astype(BF16), row(mix_norm[i]),
            wa, wlr, wm, wgk, row(b_gk[i]))
        og = _gla(qg, kg, la, vg, gg, row(gla_out_norm[i]))
        qt, ka, vt = _moba_prep(qm, km, vm)
        om = _moba_attn(qt, ka, vt, bias).transpose(1, 3, 0, 2).reshape(t, MOBA_W)
        wo = w_out[i].astype(BF16)
        h = _out_ffn(i == depth - 1, h1, og, om, p[i, 0], wo[:GLA_V], wo[GLA_V:], row(ffn2_norm[i]),
                     ffn2_w_gu[i].astype(BF16), ffn2_w_down[i].astype(BF16), row(ple_norm[i]),
                     w_ple_gate[i].astype(BF16), w_ple_proj[i].astype(BF16), row(final_norm))
    return h[None]
```

```python
import functools
import math

import jax
import jax.numpy as jnp
import numpy as np
from jax import lax
from jax.experimental import pallas as pl
from jax.experimental.pallas import tpu as pltpu

F32 = jnp.float32
BF16 = jnp.bfloat16

GLA_HEADS = 4
GLA_DK = 64
GLA_DV = 128
GLA_LOWRANK = 16
GLA_TAU = 16.0
MOBA_HEADS = 8
MOBA_HD = 64
MOBA_BLOCK = 256
MOBA_TOPK = 3
REL_BUCKETS = 32
REL_MAX_DIST = 4096
EPS = 1e-6
NEG = -1e30

GLA_QK = GLA_HEADS * GLA_DK
GLA_V = GLA_HEADS * GLA_DV
MOBA_W = MOBA_HEADS * MOBA_HD

LANES = 128
VMEM_LIMIT = 56 * 1024 * 1024

ROW_TILE = 512
GLA_TILE = 256
GLA_LEVELS = (128, 64, 32, 16, 8, 4, 2, 1)
PREP_BLOCKS = 8
BIAS_TILES = 14
KV_GROUP = 2
ATTN_UNROLL = 32
V_ROWS = 80
LOG2E = 1.4426950408889634


def _rms(x, g):
    return x * lax.rsqrt(jnp.mean(x * x, axis=-1, keepdims=True) + EPS) * g


def _ff_chunks(d_ff):
    step = 1024
    return [(s, min(step, d_ff - s)) for s in range(0, d_ff, step)]


def _swiglu(xn, wgu_ref, wd_ref):
    d_ff = wd_ref.shape[0]
    acc = None
    for start, size in _ff_chunks(d_ff):
        gate = jnp.dot(xn, wgu_ref[:, start:start + size], preferred_element_type=F32)
        up = jnp.dot(xn, wgu_ref[:, d_ff + start:d_ff + start + size], preferred_element_type=F32)
        act = (gate * jax.nn.sigmoid(gate) * up).astype(BF16)
        part = jnp.dot(act, wd_ref[start:start + size, :], preferred_element_type=F32)
        acc = part if acc is None else acc + part
    return acc


def _const_spec(shape):
    return pl.BlockSpec(shape, lambda *_: (0,) * len(shape))


def _ffn_in_kernel(x_ref, n1_ref, wgu_ref, wd_ref, n2_ref, wa_ref, wlr_ref, wm_ref, wgk_ref, bgk_ref,
                   h_ref, qg_ref, kg_ref, vg_ref, gg_ref, la_ref, qm_ref, km_ref, vm_ref):
    x = x_ref[...]
    h = x + 0.5 * _swiglu(_rms(x, n1_ref[...]).astype(BF16), wgu_ref, wd_ref)
    h_ref[...] = h
    n = _rms(h, n2_ref[...]).astype(BF16)
    ua = jnp.dot(n, wa_ref[...], preferred_element_type=F32)
    qg_ref[...] = ua[:, 0:GLA_QK]
    kg_ref[...] = ua[:, GLA_QK:2 * GLA_QK]
    vg_ref[...] = ua[:, 2 * GLA_QK:2 * GLA_QK + GLA_V]
    gg_ref[...] = ua[:, 2 * GLA_QK + GLA_V:]
    lr = jnp.dot(n, wlr_ref[...], preferred_element_type=F32)
    z = jnp.dot(lr.astype(BF16), wgk_ref[...], preferred_element_type=F32) + bgk_ref[...]
    la_ref[...] = (jnp.minimum(z, 0.0) - jnp.log1p(jnp.exp(-jnp.abs(z)))) * (1.0 / GLA_TAU)
    um = jnp.dot(n, wm_ref[...], preferred_element_type=F32)
    qm_ref[...] = um[:, 0:MOBA_W]
    km_ref[...] = um[:, MOBA_W:2 * MOBA_W]
    vm_ref[...] = um[:, 2 * MOBA_W:]


def _ffn_in(x, n1, wgu, wd, n2, wa, wlr, wm, wgk, bgk):
    t, d = x.shape
    tm = min(ROW_TILE, t)
    row = lambda w: pl.BlockSpec((tm, w), lambda i: (i, 0))
    outs = [(d, F32), (GLA_QK, F32), (GLA_QK, F32), (GLA_V, F32), (GLA_V, F32), (GLA_QK, F32),
            (MOBA_W, F32), (MOBA_W, F32), (MOBA_W, F32)]
    return pl.pallas_call(
        _ffn_in_kernel,
        grid=(t // tm,),
        in_specs=[row(d)] + [_const_spec(a.shape) for a in (n1, wgu, wd, n2, wa, wlr, wm, wgk, bgk)],
        out_specs=[row(w) for w, _ in outs],
        out_shape=[jax.ShapeDtypeStruct((t, w), dt) for w, dt in outs],
        compiler_params=pltpu.CompilerParams(dimension_semantics=("parallel",), vmem_limit_bytes=VMEM_LIMIT),
        name="ffn_in",
    )(x, n1, wgu, wd, n2, wa, wlr, wm, wgk, bgk)


def _gla_span_matrix(tile):
    i = np.arange(tile)[:, None]
    t = np.arange(tile)[None, :]
    mats = [t <= i, t > i]
    for s in GLA_LEVELS:
        split = (i // (2 * s)) * (2 * s) + s - 1
        mats.append(np.where((i & s) != 0, (t > split) & (t <= i), (t > i) & (t <= split)))
    return jnp.asarray(np.concatenate(mats, axis=0), BF16)


def _gla_kernel(q_ref, k_ref, la_ref, v_ref, g_ref, gn_ref, span_ref, o_ref, st_ref):
    @pl.when(pl.program_id(0) == 0)
    def _():
        st_ref[...] = jnp.zeros_like(st_ref)

    tb = q_ref.shape[0]
    half = tb // 2
    la = la_ref[...]
    la_hi = la.astype(BF16)
    la_lo = (la - la_hi.astype(F32)).astype(BF16)

    def span_sum(r):
        m = span_ref[r * tb:(r + 1) * tb, :]
        return jnp.dot(m, la_hi, preferred_element_type=F32) + jnp.dot(m, la_lo, preferred_element_type=F32)

    b = span_sum(0)
    q = q_ref[...] * (GLA_DK ** -0.5)
    k = k_ref[...]
    qe = q * jnp.exp(b)
    ke = k * jnp.exp(span_sum(1))
    a_last = jnp.exp(b[tb - 1:tb, :])
    row = lax.broadcasted_iota(jnp.int32, (tb, GLA_QK), 0)
    q_lvl, k_lvl = [], []
    for idx, s in enumerate(GLA_LEVELS):
        e = jnp.exp(span_sum(2 + idx))
        upper = (row & s) != 0
        q_lvl.append(jnp.where(upper, q * e, 0.0))
        k_lvl.append(jnp.where(upper, 0.0, k * e))
    ri = lax.broadcasted_iota(jnp.int32, (half, 2 * half), 0)
    ci = lax.broadcasted_iota(jnp.int32, (half, 2 * half), 1) % half
    same_block = [ri // (2 * s) == ci // (2 * s) for s in GLA_LEVELS[1:]]
    diagonal = ri == ci
    first_head = lax.broadcasted_iota(jnp.int32, (half, 2 * GLA_DK), 1) < GLA_DK
    gn = gn_ref[...]
    nt = (((1,), (1,)), ((), ()))

    def pair(x, y, r0, c0, lanes):
        keys = y[c0:c0 + half, lanes]
        keys = jnp.concatenate([jnp.where(first_head, keys, 0.0), jnp.where(first_head, 0.0, keys)], axis=0)
        return lax.dot_general(x[r0:r0 + half, lanes].astype(BF16), keys.astype(BF16), nt,
                               preferred_element_type=F32)

    intra = []
    for hp in range(GLA_HEADS // 2):
        lanes = slice(2 * hp * GLA_DK, 2 * (hp + 1) * GLA_DK)
        diag_blocks = []
        for c in (0, half):
            blk = jnp.where(diagonal, pair(q, k, c, c, lanes), 0.0)
            for idx in range(1, len(GLA_LEVELS)):
                blk = blk + jnp.where(same_block[idx - 1], pair(q_lvl[idx], k_lvl[idx], c, c, lanes), 0.0)
            diag_blocks.append(blk)
        low = pair(q_lvl[0], k_lvl[0], half, 0, lanes)
        for a in range(2):
            cols = slice(a * half, (a + 1) * half)
            intra.append(jnp.concatenate(
                [jnp.concatenate([diag_blocks[0][:, cols], jnp.zeros((half, half), F32)], axis=1),
                 jnp.concatenate([low[:, cols], diag_blocks[1][:, cols]], axis=1)], axis=0).astype(BF16))
    for h in range(GLA_HEADS):
        ks = slice(h * GLA_DK, (h + 1) * GLA_DK)
        vs = slice(h * GLA_DV, (h + 1) * GLA_DV)
        attn = intra[h]
        v = v_ref[:, vs].astype(BF16)
        st = st_ref[h]
        o = jnp.dot(attn, v, preferred_element_type=F32)
        o = o + lax.dot_general(qe[:, ks].astype(BF16), st.astype(BF16), nt, preferred_element_type=F32)
        st_ref[h] = st * a_last[:, ks] + lax.dot_general(v, ke[:, ks].astype(BF16), (((0,), (0,)), ((), ())),
                                                         preferred_element_type=F32)
        g = g_ref[:, vs]
        o_ref[:, vs] = _rms(o, gn) * (g * jax.nn.sigmoid(g))


def _gla(qg, kg, la, vg, gg, gn):
    t = qg.shape[0]
    tb = GLA_TILE
    span = _gla_span_matrix(tb)
    row = lambda w: pl.BlockSpec((tb, w), lambda i: (i, 0))
    return pl.pallas_call(
        _gla_kernel,
        grid=(t // tb,),
        in_specs=[row(GLA_QK), row(GLA_QK), row(GLA_QK), row(GLA_V), row(GLA_V), _const_spec(gn.shape),
                  _const_spec(span.shape)],
        out_specs=row(GLA_V),
        out_shape=jax.ShapeDtypeStruct((t, GLA_V), F32),
        scratch_shapes=[pltpu.VMEM((GLA_HEADS, GLA_DV, GLA_DK), F32)],
        compiler_params=pltpu.CompilerParams(dimension_semantics=("arbitrary",), vmem_limit_bytes=VMEM_LIMIT),
        name="gla",
    )(qg, kg, la, vg, gg, gn, span)


def _moba_bias_kernel(tab_ref, o_ref):
    h = pl.program_id(0)
    blk = MOBA_BLOCK
    max_exact = REL_BUCKETS // 2
    x = lax.broadcasted_iota(jnp.int32, (8, 2 * blk), 1)
    for delta in range(BIAS_TILES):
        dist = delta * blk + x - blk
        n = jnp.maximum(dist, 0)
        nf = jnp.maximum(n, 1).astype(F32)
        large = max_exact + (jnp.log(nf / max_exact) / math.log(REL_MAX_DIST / max_exact)
                             * (REL_BUCKETS - max_exact)).astype(jnp.int32)
        large = jnp.minimum(large, REL_BUCKETS - 1)
        bucket = jnp.where(n < max_exact, n, large)
        bias = jnp.zeros(dist.shape, F32)
        for b in range(REL_BUCKETS):
            bias = jnp.where(bucket == b, tab_ref[b * MOBA_HEADS + h], bias)
        by_dist = jnp.where(dist >= 0, bias * LOG2E, NEG)
        rows = jnp.broadcast_to(by_dist[0:1], (blk, 2 * blk))
        o_ref[0, delta] = pltpu.roll(rows, 0, 1, stride=1, stride_axis=0)[:, blk:].astype(o_ref.dtype)


def _moba_bias(rel_bias):
    blk = MOBA_BLOCK
    return pl.pallas_call(
        _moba_bias_kernel,
        grid=(MOBA_HEADS,),
        in_specs=[pl.BlockSpec(memory_space=pltpu.SMEM)],
        out_specs=pl.BlockSpec((1, BIAS_TILES, blk, blk), lambda h: (h, 0, 0, 0)),
        out_shape=jax.ShapeDtypeStruct((MOBA_HEADS, BIAS_TILES, blk, blk), BF16),
        compiler_params=pltpu.CompilerParams(dimension_semantics=("parallel",)),
        name="moba_bias",
    )(rel_bias.reshape(-1))


def _moba_prep_kernel(q_ref, k_ref, v_ref, qt_ref, ka_ref, vt_ref, km_ref):
    t = pl.program_id(1)
    blk = MOBA_BLOCK
    rows = PREP_BLOCKS * blk
    hd = MOBA_HD

    @pl.when(t == 0)
    def _():
        km_ref[...] = jnp.zeros_like(km_ref)

    k = k_ref[...]
    km_ref[pl.ds(pl.multiple_of(t * PREP_BLOCKS, PREP_BLOCKS), PREP_BLOCKS), :] = jnp.mean(
        k.reshape(PREP_BLOCKS, blk, LANES), axis=1)
    km = km_ref[...]
    km_lane = lax.broadcasted_iota(jnp.int32, km.shape, 1)
    q_t = q_ref[...].T
    v_t = v_ref[...].T
    block = lax.broadcasted_iota(jnp.int32, (hd, rows), 0)
    block_f = block.astype(F32)
    own = t * PREP_BLOCKS + lax.broadcasted_iota(jnp.int32, (hd, rows), 1) // blk
    lane = lax.broadcasted_iota(jnp.int32, (rows, LANES), 1)
    own_k = t * PREP_BLOCKS + lax.broadcasted_iota(jnp.int32, (rows, LANES), 0) // blk
    ones_rows = jnp.where(lax.broadcasted_iota(jnp.int32, (V_ROWS - hd, rows), 0) == 0, 1.0, 0.0)
    for a in range(2):
        gate = jnp.dot(jnp.where(km_lane // hd == a, km, 0.0), q_t, precision=lax.Precision.HIGHEST,
                       preferred_element_type=F32)
        gate = jnp.where(block < own, gate, -jnp.inf)
        chosen = jnp.where(block == own, 1.0, 0.0)
        for _ in range(MOBA_TOPK):
            top = jnp.max(gate, axis=0, keepdims=True)
            first = jnp.min(jnp.where(gate == top, block_f, float(hd)), axis=0, keepdims=True)
            hit = block_f == first
            chosen = jnp.where(hit, 1.0, chosen)
            gate = jnp.where(hit, -jnp.inf, gate)
        q_a = q_t[a * hd:(a + 1) * hd] * (hd ** -0.5 * LOG2E)
        q_aug_t = jnp.concatenate([q_a, chosen - 1.0] if a == 0 else [chosen - 1.0, q_a], axis=0).astype(BF16)
        v_aug_t = jnp.concatenate([v_t[a * hd:(a + 1) * hd], ones_rows], axis=0).astype(BF16)
        k_aug = jnp.where(lane // hd == a, k, jnp.where(lane - (1 - a) * hd == own_k, -NEG, 0.0)).astype(BF16)
        for b in range(PREP_BLOCKS):
            qt_ref[a, b] = q_aug_t[:, b * blk:(b + 1) * blk]
            ka_ref[a, b] = k_aug[b * blk:(b + 1) * blk, :]
        group = KV_GROUP * blk
        for g in range(PREP_BLOCKS // KV_GROUP):
            vt_ref[a, g] = v_aug_t[:, g * group:(g + 1) * group]


def _moba_prep(qm, km, vm):
    t = qm.shape[0]
    blk = MOBA_BLOCK
    nb = t // blk
    assert nb <= MOBA_HD
    rows = PREP_BLOCKS * blk
    pair = pl.BlockSpec((rows, LANES), lambda p, i: (i, p))
    t_spec = pl.BlockSpec((2, PREP_BLOCKS, LANES, blk), lambda p, i: (p, i, 0, 0))
    k_spec = pl.BlockSpec((2, PREP_BLOCKS, blk, LANES), lambda p, i: (p, i, 0, 0))
    v_spec = pl.BlockSpec((2, PREP_BLOCKS // KV_GROUP, V_ROWS, KV_GROUP * blk), lambda p, i: (p, i, 0, 0))
    return pl.pallas_call(
        _moba_prep_kernel,
        grid=(MOBA_HEADS // 2, nb // PREP_BLOCKS),
        in_specs=[pair, pair, pair],
        out_specs=[t_spec, k_spec, v_spec],
        out_shape=[jax.ShapeDtypeStruct((MOBA_HEADS, nb, LANES, blk), BF16),
                   jax.ShapeDtypeStruct((MOBA_HEADS, nb, blk, LANES), BF16),
                   jax.ShapeDtypeStruct((MOBA_HEADS, nb // KV_GROUP, V_ROWS, KV_GROUP * blk), BF16)],
        scratch_shapes=[pltpu.VMEM((MOBA_HD, LANES), F32)],
        compiler_params=pltpu.CompilerParams(dimension_semantics=("parallel", "arbitrary"),
                                             vmem_limit_bytes=VMEM_LIMIT),
        name="moba_prep",
    )(qm, km, vm)


def _moba_attn_kernel(qi_ref, gi_ref, qt_ref, ka_ref, vt_ref, bias_ref, o_ref, s0_ref, s1_ref, p0_ref, p1_ref,
                      acc_ref):
    n_pairs = qi_ref.shape[0] - 2
    blk = MOBA_BLOCK
    s_refs = (s0_ref, s1_ref)
    p_refs = (p0_ref, p1_ref)

    def scores(t, slot):
        s_ref = s_refs[slot]
        qi = qi_ref[t]
        gi = gi_ref[t]
        qt = qt_ref[0, qi]
        col_max = None
        for b in range(KV_GROUP):
            j = gi * KV_GROUP + b
            s = jnp.dot(ka_ref[0, j], qt, preferred_element_type=F32).astype(BF16)
            s = s + bias_ref[0, jnp.clip(qi - j, 0, BIAS_TILES - 1)]
            s_ref[b] = s
            cm = jnp.max(s, axis=0, keepdims=True)
            col_max = cm if col_max is None else jnp.maximum(col_max, cm)
        return col_max.astype(F32)

    def softmax(t, slot, m_run, col_max):
        m_prev = jnp.where(gi_ref[t] == 0, -jnp.inf, m_run)
        m = jnp.maximum(m_prev, col_max)
        for b in range(KV_GROUP):
            p_refs[slot][b * blk:(b + 1) * blk, :] = jnp.exp2(s_refs[slot][b] - m.astype(BF16))
        return m, jnp.exp2(m_prev - m)

    def values(t, slot, alpha):
        acc = acc_ref[...] * alpha + jnp.dot(vt_ref[0, gi_ref[t]], p_refs[slot][...], preferred_element_type=F32)
        acc_ref[...] = acc
        o_ref[0, qi_ref[t]] = (acc[0:MOBA_HD] / acc[MOBA_HD:MOBA_HD + 1]).astype(o_ref.dtype)

    acc_ref[...] = jnp.zeros_like(acc_ref)
    cm0 = scores(0, 0)
    cm1 = scores(1, 1)
    m0, alpha0 = softmax(0, 0, jnp.full((1, blk), -jnp.inf, F32), cm0)

    unroll = max(u for u in range(2, ATTN_UNROLL + 1, 2) if n_pairs % u == 0)

    def step(u, carry):
        m, alpha, cm = carry
        for i in range(unroll):
            t = unroll * u + i
            cm_next = scores(t + 2, i % 2)
            m, alpha_next = softmax(t + 1, 1 - i % 2, m, cm)
            values(t, i % 2, alpha)
            alpha, cm = alpha_next, cm_next
        return m, alpha, cm

    lax.fori_loop(0, n_pairs // unroll, step, (m0, alpha0, cm1))


def _moba_attn(qt, ka, vt, bias):
    nb = qt.shape[1]
    blk = MOBA_BLOCK
    pairs = [(qi, g) for qi in range(nb) for g in range(qi // KV_GROUP + 1)] + [(0, 0), (0, 0)]
    qi_tab = jnp.asarray([a for a, _ in pairs], jnp.int32)
    gi_tab = jnp.asarray([b for _, b in pairs], jnp.int32)
    head = lambda shape: pl.BlockSpec((1,) + shape, lambda h, *_: (h,) + (0,) * len(shape))
    return pl.pallas_call(
        _moba_attn_kernel,
        grid_spec=pltpu.PrefetchScalarGridSpec(
            num_scalar_prefetch=2,
            grid=(MOBA_HEADS,),
            in_specs=[head((nb, LANES, blk)), head((nb, blk, LANES)),
                      head((nb // KV_GROUP, V_ROWS, KV_GROUP * blk)), head((BIAS_TILES, blk, blk))],
            out_specs=head((nb, MOBA_HD, blk)),
            scratch_shapes=[pltpu.VMEM((KV_GROUP, blk, blk), BF16), pltpu.VMEM((KV_GROUP, blk, blk), BF16),
                            pltpu.VMEM((KV_GROUP * blk, blk), BF16), pltpu.VMEM((KV_GROUP * blk, blk), BF16),
                            pltpu.VMEM((V_ROWS, blk), F32)]),
        out_shape=jax.ShapeDtypeStruct((MOBA_HEADS, nb, MOBA_HD, blk), BF16),
        compiler_params=pltpu.CompilerParams(dimension_semantics=("parallel",), vmem_limit_bytes=VMEM_LIMIT),
        name="moba_attn",
    )(qi_tab, gi_tab, qt, ka, vt, bias)


def _out_ffn_kernel(final, h_ref, og_ref, om_ref, p_ref, wog_ref, wom_ref, n3_ref, wgu_ref, wd_ref,
                    n4_ref, wpg_ref, wpp_ref, nf_ref, o_ref):
    mix = jnp.dot(og_ref[...].astype(BF16), wog_ref[...], preferred_element_type=F32)
    om = [om_ref[:, b].reshape(MOBA_W, MOBA_BLOCK).astype(F32).T.astype(BF16) for b in range(om_ref.shape[1])]
    mix = mix + jnp.dot(jnp.concatenate(om, axis=0), wom_ref[...], preferred_element_type=F32)
    h = h_ref[...] + mix
    h = h + 0.5 * _swiglu(_rms(h, n3_ref[...]).astype(BF16), wgu_ref, wd_ref)
    gate = jax.nn.sigmoid(jnp.dot(_rms(h, n4_ref[...]).astype(BF16), wpg_ref[...], preferred_element_type=F32))
    h = h + gate * jnp.dot(p_ref[...].astype(BF16), wpp_ref[...], preferred_element_type=F32)
    o_ref[...] = _rms(h, nf_ref[...]) if final else h


def _out_ffn(final, h, og, om, p, wog, wom, n3, wgu, wd, n4, wpg, wpp, nf):
    t, d = h.shape
    tm = min(ROW_TILE, t)
    row = lambda w: pl.BlockSpec((tm, w), lambda i: (i, 0))
    consts = (wog, wom, n3, wgu, wd, n4, wpg, wpp, nf)
    return pl.pallas_call(
        functools.partial(_out_ffn_kernel, final),
        grid=(t // tm,),
        in_specs=[row(d), row(GLA_V),
                  pl.BlockSpec((MOBA_HEADS, tm // MOBA_BLOCK, MOBA_HD, MOBA_BLOCK), lambda i: (0, i, 0, 0)),
                  row(p.shape[1])]
        + [_const_spec(a.shape) for a in consts],
        out_specs=row(d),
        out_shape=jax.ShapeDtypeStruct((t, d), F32),
        compiler_params=pltpu.CompilerParams(dimension_semantics=("parallel",), vmem_limit_bytes=VMEM_LIMIT),
        name="out_ffn",
    )(h, og, om, p, *consts)


def kernel(x, p, ffn1_norm, ffn1_w_gu, ffn1_w_down, mix_norm, w_in, w_gk_up, b_gk, gla_out_norm, w_out,
           ffn2_norm, ffn2_w_gu, ffn2_w_down, ple_norm, w_ple_gate, w_ple_proj, rel_bias, final_norm):
    batch, t, d = x.shape
    depth = p.shape[0]
    assert batch == 1 and t % (PREP_BLOCKS * MOBA_BLOCK) == 0
    row = lambda a: a.reshape(1, -1)
    bias = _moba_bias(rel_bias)
    lr0 = 2 * GLA_QK + 2 * GLA_V
    h = x[0]
    for i in range(depth):
        wi = w_in[i]
        wa = wi[:, :lr0].astype(BF16)
        wlr = jnp.pad(wi[:, lr0:lr0 + GLA_LOWRANK], ((0, 0), (0, LANES - GLA_LOWRANK))).astype(BF16)
        wm = wi[:, lr0 + GLA_LOWRANK:].astype(BF16)
        wgk = jnp.pad(w_gk_up[i], ((0, LANES - GLA_LOWRANK), (0, 0))).astype(BF16)
        h1, qg, kg, vg, gg, la, qm, km, vm = _ffn_in(
            h, row(ffn1_norm[i]), ffn1_w_gu[i].astype(BF16), ffn1_w_down[i].astype(BF16), row(mix_norm[i]),
            wa, wlr, wm, wgk, row(b_gk[i]))
        og = _gla(qg, kg, la, vg, gg, row(gla_out_norm[i]))
        qt, ka, vt = _moba_prep(qm, km, vm)
        om = _moba_attn(qt, ka, vt, bias)
        wo = w_out[i].astype(BF16)
        h = _out_ffn(i == depth - 1, h1, og, om, p[i, 0], wo[:GLA_V], wo[GLA_V:], row(ffn2_norm[i]),
                     ffn2_w_gu[i].astype(BF16), ffn2_w_down[i].astype(BF16), row(ple_norm[i]),
                     w_ple_gate[i].astype(BF16), w_ple_proj[i].astype(BF16), row(final_norm))
    return h[None]
```

```python
import functools
import math

import jax
import jax.numpy as jnp
import numpy as np
from jax import lax
from jax.experimental import pallas as pl
from jax.experimental.pallas import tpu as pltpu

F32 = jnp.float32
BF16 = jnp.bfloat16

GLA_HEADS = 4
GLA_DK = 64
GLA_DV = 128
GLA_LOWRANK = 16
GLA_TAU = 16.0
MOBA_HEADS = 8
MOBA_HD = 64
MOBA_BLOCK = 256
MOBA_TOPK = 3
REL_BUCKETS = 32
REL_MAX_DIST = 4096
EPS = 1e-6
NEG = -1e30

GLA_QK = GLA_HEADS * GLA_DK
GLA_V = GLA_HEADS * GLA_DV
MOBA_W = MOBA_HEADS * MOBA_HD

LANES = 128
VMEM_LIMIT = 56 * 1024 * 1024

ROW_TILE = 512
GLA_TILE = 256
GLA_LEVELS = tuple(GLA_TILE >> (i + 1) for i in range(GLA_TILE.bit_length() - 1))
PREP_BLOCKS = 8
_FAR_DIST = math.ceil((REL_BUCKETS // 2) * (REL_MAX_DIST / (REL_BUCKETS // 2))
                      ** ((REL_BUCKETS // 2 - 1) / (REL_BUCKETS // 2)))
_FAR_BLOCKS = -(-(_FAR_DIST - 1) // MOBA_BLOCK) + 1
BIAS_TILES = _FAR_BLOCKS + 1
KV_GROUP = 2
ATTN_UNROLL = 32
V_ROWS = 80
LOG2E = 1.4426950408889634


def _rms(x, g):
    return x * lax.rsqrt(jnp.mean(x * x, axis=-1, keepdims=True) + EPS) * g


def _ff_chunks(d_ff):
    step = 1024
    return [(s, min(step, d_ff - s)) for s in range(0, d_ff, step)]


def _swiglu(xn, wgu_ref, wd_ref):
    d_ff = wd_ref.shape[0]
    acc = None
    for start, size in _ff_chunks(d_ff):
        gate = jnp.dot(xn, wgu_ref[:, start:start + size], preferred_element_type=F32)
        up = jnp.dot(xn, wgu_ref[:, d_ff + start:d_ff + start + size], preferred_element_type=F32)
        act = (gate * jax.nn.sigmoid(gate) * up).astype(BF16)
        part = jnp.dot(act, wd_ref[start:start + size, :], preferred_element_type=F32)
        acc = part if acc is None else acc + part
    return acc


def _const_spec(shape):
    return pl.BlockSpec(shape, lambda *_: (0,) * len(shape))


def _ffn_in_kernel(x_ref, n1_ref, wgu_ref, wd_ref, n2_ref, wa_ref, wlr_ref, wm_ref, wgk_ref, bgk_ref,
                   h_ref, qg_ref, kg_ref, vg_ref, gg_ref, la_ref, qm_ref, km_ref, vm_ref):
    x = x_ref[...]
    h = x + 0.5 * _swiglu(_rms(x, n1_ref[...]).astype(BF16), wgu_ref, wd_ref)
    h_ref[...] = h
    n = _rms(h, n2_ref[...]).astype(BF16)
    ua = jnp.dot(n, wa_ref[...], preferred_element_type=F32)
    qg_ref[...] = ua[:, 0:GLA_QK]
    kg_ref[...] = ua[:, GLA_QK:2 * GLA_QK]
    vg_ref[...] = ua[:, 2 * GLA_QK:2 * GLA_QK + GLA_V]
    gg_ref[...] = ua[:, 2 * GLA_QK + GLA_V:]
    lr = jnp.dot(n, wlr_ref[...], preferred_element_type=F32)
    z = jnp.dot(lr.astype(BF16), wgk_ref[...], preferred_element_type=F32) + bgk_ref[...]
    la_ref[...] = (jnp.minimum(z, 0.0) - jnp.log1p(jnp.exp(-jnp.abs(z)))) * (1.0 / GLA_TAU)
    um = jnp.dot(n, wm_ref[...], preferred_element_type=F32)
    qm_ref[...] = um[:, 0:MOBA_W]
    km_ref[...] = um[:, MOBA_W:2 * MOBA_W]
    vm_ref[...] = um[:, 2 * MOBA_W:]


def _ffn_in(x, n1, wgu, wd, n2, wa, wlr, wm, wgk, bgk):
    t, d = x.shape
    tm = min(ROW_TILE, t)
    row = lambda w: pl.BlockSpec((tm, w), lambda i: (i, 0))
    outs = [(d, F32), (GLA_QK, F32), (GLA_QK, F32), (GLA_V, F32), (GLA_V, F32), (GLA_QK, F32),
            (MOBA_W, F32), (MOBA_W, F32), (MOBA_W, F32)]
    return pl.pallas_call(
        _ffn_in_kernel,
        grid=(t // tm,),
        in_specs=[row(d)] + [_const_spec(a.shape) for a in (n1, wgu, wd, n2, wa, wlr, wm, wgk, bgk)],
        out_specs=[row(w) for w, _ in outs],
        out_shape=[jax.ShapeDtypeStruct((t, w), dt) for w, dt in outs],
        compiler_params=pltpu.CompilerParams(dimension_semantics=("parallel",), vmem_limit_bytes=VMEM_LIMIT),
        name="ffn_in",
    )(x, n1, wgu, wd, n2, wa, wlr, wm, wgk, bgk)


def _gla_span_matrix(tile):
    i = np.arange(tile)[:, None]
    t = np.arange(tile)[None, :]
    mats = [t <= i, t > i]
    for s in GLA_LEVELS:
        split = (i // (2 * s)) * (2 * s) + s - 1
        mats.append(np.where((i & s) != 0, (t > split) & (t <= i), (t > i) & (t <= split)))
    return jnp.asarray(np.concatenate(mats, axis=0), BF16)


def _gla_kernel(q_ref, k_ref, la_ref, v_ref, g_ref, gn_ref, span_ref, o_ref, st_ref):
    @pl.when(pl.program_id(0) == 0)
    def _():
        st_ref[...] = jnp.zeros_like(st_ref)

    tb = q_ref.shape[0]
    half = tb // 2
    la = la_ref[...]
    la_hi = la.astype(BF16)
    la_lo = (la - la_hi.astype(F32)).astype(BF16)

    def span_sum(r):
        m = span_ref[r * tb:(r + 1) * tb, :]
        return jnp.dot(m, la_hi, preferred_element_type=F32) + jnp.dot(m, la_lo, preferred_element_type=F32)

    b = span_sum(0)
    q = q_ref[...] * (GLA_DK ** -0.5)
    k = k_ref[...]
    qe = q * jnp.exp(b)
    ke = k * jnp.exp(span_sum(1))
    a_last = jnp.exp(b[tb - 1:tb, :])
    row = lax.broadcasted_iota(jnp.int32, (tb, GLA_QK), 0)
    q_lvl, k_lvl = [], []
    for idx, s in enumerate(GLA_LEVELS):
        e = jnp.exp(span_sum(2 + idx))
        upper = (row & s) != 0
        q_lvl.append(jnp.where(upper, q * e, 0.0))
        k_lvl.append(jnp.where(upper, 0.0, k * e))
    ri = lax.broadcasted_iota(jnp.int32, (half, 2 * half), 0)
    ci = lax.broadcasted_iota(jnp.int32, (half, 2 * half), 1) % half
    same_block = [ri // (2 * s) == ci // (2 * s) for s in GLA_LEVELS[1:]]
    diagonal = ri == ci
    first_head = lax.broadcasted_iota(jnp.int32, (half, 2 * GLA_DK), 1) < GLA_DK
    gn = gn_ref[...]
    nt = (((1,), (1,)), ((), ()))

    def pair(x, y, r0, c0, lanes):
        keys = y[c0:c0 + half, lanes]
        keys = jnp.concatenate([jnp.where(first_head, keys, 0.0), jnp.where(first_head, 0.0, keys)], axis=0)
        return lax.dot_general(x[r0:r0 + half, lanes].astype(BF16), keys.astype(BF16), nt,
                               preferred_element_type=F32)

    intra = []
    for hp in range(GLA_HEADS // 2):
        lanes = slice(2 * hp * GLA_DK, 2 * (hp + 1) * GLA_DK)
        diag_blocks = []
        for c in (0, half):
            blk = jnp.where(diagonal, pair(q, k, c, c, lanes), 0.0)
            for idx in range(1, len(GLA_LEVELS)):
                blk = blk + jnp.where(same_block[idx - 1], pair(q_lvl[idx], k_lvl[idx], c, c, lanes), 0.0)
            diag_blocks.append(blk)
        low = pair(q_lvl[0], k_lvl[0], half, 0, lanes)
        for a in range(2):
            cols = slice(a * half, (a + 1) * half)
            intra.append(jnp.concatenate(
                [jnp.concatenate([diag_blocks[0][:, cols], jnp.zeros((half, half), F32)], axis=1),
                 jnp.concatenate([low[:, cols], diag_blocks[1][:, cols]], axis=1)], axis=0).astype(BF16))
    for h in range(GLA_HEADS):
        ks = slice(h * GLA_DK, (h + 1) * GLA_DK)
        vs = slice(h * GLA_DV, (h + 1) * GLA_DV)
        attn = intra[h]
        v = v_ref[:, vs].astype(BF16)
        st = st_ref[h]
        o = jnp.dot(attn, v, preferred_element_type=F32)
        o = o + lax.dot_general(qe[:, ks].astype(BF16), st.astype(BF16), nt, preferred_element_type=F32)
        st_ref[h] = st * a_last[:, ks] + lax.dot_general(v, ke[:, ks].astype(BF16), (((0,), (0,)), ((), ())),
                                                         preferred_element_type=F32)
        g = g_ref[:, vs]
        o_ref[:, vs] = _rms(o, gn) * (g * jax.nn.sigmoid(g))


def _gla(qg, kg, la, vg, gg, gn):
    t = qg.shape[0]
    tb = GLA_TILE
    span = _gla_span_matrix(tb)
    row = lambda w: pl.BlockSpec((tb, w), lambda i: (i, 0))
    return pl.pallas_call(
        _gla_kernel,
        grid=(t // tb,),
        in_specs=[row(GLA_QK), row(GLA_QK), row(GLA_QK), row(GLA_V), row(GLA_V), _const_spec(gn.shape),
                  _const_spec(span.shape)],
        out_specs=row(GLA_V),
        out_shape=jax.ShapeDtypeStruct((t, GLA_V), F32),
        scratch_shapes=[pltpu.VMEM((GLA_HEADS, GLA_DV, GLA_DK), F32)],
        compiler_params=pltpu.CompilerParams(dimension_semantics=("arbitrary",), vmem_limit_bytes=VMEM_LIMIT),
        name="gla",
    )(qg, kg, la, vg, gg, gn, span)


def _moba_bias_kernel(tab_ref, o_ref):
    h = pl.program_id(0)
    blk = MOBA_BLOCK
    max_exact = REL_BUCKETS // 2
    x = lax.broadcasted_iota(jnp.int32, (8, 2 * blk), 1)
    for delta in range(BIAS_TILES):
        dist = delta * blk + x - blk
        n = jnp.maximum(dist, 0)
        nf = jnp.maximum(n, 1).astype(F32)
        large = max_exact + (jnp.log(nf / max_exact) / math.log(REL_MAX_DIST / max_exact)
                             * (REL_BUCKETS - max_exact)).astype(jnp.int32)
        large = jnp.minimum(large, REL_BUCKETS - 1)
        bucket = jnp.where(n < max_exact, n, large)
        bias = jnp.zeros(dist.shape, F32)
        for b in range(REL_BUCKETS):
            bias = jnp.where(bucket == b, tab_ref[b * MOBA_HEADS + h], bias)
        by_dist = jnp.where(dist >= 0, bias * LOG2E, NEG)
        rows = jnp.broadcast_to(by_dist[0:1], (blk, 2 * blk))
        o_ref[0, delta] = pltpu.roll(rows, 0, 1, stride=1, stride_axis=0)[:, blk:].astype(o_ref.dtype)


def _moba_bias(rel_bias):
    blk = MOBA_BLOCK
    return pl.pallas_call(
        _moba_bias_kernel,
        grid=(MOBA_HEADS,),
        in_specs=[pl.BlockSpec(memory_space=pltpu.SMEM)],
        out_specs=pl.BlockSpec((1, BIAS_TILES, blk, blk), lambda h: (h, 0, 0, 0)),
        out_shape=jax.ShapeDtypeStruct((MOBA_HEADS, BIAS_TILES, blk, blk), BF16),
        compiler_params=pltpu.CompilerParams(dimension_semantics=("parallel",)),
        name="moba_bias",
    )(rel_bias.reshape(-1))


def _moba_prep_kernel(q_ref, k_ref, v_ref, qt_ref, ka_ref, vt_ref, km_ref):
    t = pl.program_id(1)
    blk = MOBA_BLOCK
    rows = PREP_BLOCKS * blk
    hd = MOBA_HD

    @pl.when(t == 0)
    def _():
        km_ref[...] = jnp.zeros_like(km_ref)

    k = k_ref[...]
    km_ref[pl.ds(pl.multiple_of(t * PREP_BLOCKS, PREP_BLOCKS), PREP_BLOCKS), :] = jnp.mean(
        k.reshape(PREP_BLOCKS, blk, LANES), axis=1)
    km = km_ref[...]
    km_lane = lax.broadcasted_iota(jnp.int32, km.shape, 1)
    q_t = q_ref[...].T
    v_t = v_ref[...].T
    block = lax.broadcasted_iota(jnp.int32, (hd, rows), 0)
    block_f = block.astype(F32)
    own = t * PREP_BLOCKS + lax.broadcasted_iota(jnp.int32, (hd, rows), 1) // blk
    lane = lax.broadcasted_iota(jnp.int32, (rows, LANES), 1)
    own_k = t * PREP_BLOCKS + lax.broadcasted_iota(jnp.int32, (rows, LANES), 0) // blk
    ones_rows = jnp.where(lax.broadcasted_iota(jnp.int32, (V_ROWS - hd, rows), 0) == 0, 1.0, 0.0)
    for a in range(2):
        gate = jnp.dot(jnp.where(km_lane // hd == a, km, 0.0), q_t, precision=lax.Precision.HIGHEST,
                       preferred_element_type=F32)
        gate = jnp.where(block < own, gate, -jnp.inf)
        chosen = jnp.where(block == own, 1.0, 0.0)
        for _ in range(MOBA_TOPK):
            top = jnp.max(gate, axis=0, keepdims=True)
            first = jnp.min(jnp.where(gate == top, block_f, float(hd)), axis=0, keepdims=True)
            hit = block_f == first
            chosen = jnp.where(hit, 1.0, chosen)
            gate = jnp.where(hit, -jnp.inf, gate)
        q_a = q_t[a * hd:(a + 1) * hd] * (hd ** -0.5 * LOG2E)
        q_aug_t = jnp.concatenate([q_a, chosen - 1.0] if a == 0 else [chosen - 1.0, q_a], axis=0).astype(BF16)
        v_aug_t = jnp.concatenate([v_t[a * hd:(a + 1) * hd], ones_rows], axis=0).astype(BF16)
        k_aug = jnp.where(lane // hd == a, k, jnp.where(lane - (1 - a) * hd == own_k, -NEG, 0.0)).astype(BF16)
        for b in range(PREP_BLOCKS):
            qt_ref[a, b] = q_aug_t[:, b * blk:(b + 1) * blk]
            ka_ref[a, b] = k_aug[b * blk:(b + 1) * blk, :]
        group = KV_GROUP * blk
        for g in range(PREP_BLOCKS // KV_GROUP):
            vt_ref[a, g] = v_aug_t[:, g * group:(g + 1) * group]


def _moba_prep(qm, km, vm):
    t = qm.shape[0]
    blk = MOBA_BLOCK
    nb = t // blk
    assert nb <= MOBA_HD
    rows = PREP_BLOCKS * blk
    pair = pl.BlockSpec((rows, LANES), lambda p, i: (i, p))
    t_spec = pl.BlockSpec((2, PREP_BLOCKS, LANES, blk), lambda p, i: (p, i, 0, 0))
    k_spec = pl.BlockSpec((2, PREP_BLOCKS, blk, LANES), lambda p, i: (p, i, 0, 0))
    v_spec = pl.BlockSpec((2, PREP_BLOCKS // KV_GROUP, V_ROWS, KV_GROUP * blk), lambda p, i: (p, i, 0, 0))
    return pl.pallas_call(
        _moba_prep_kernel,
        grid=(MOBA_HEADS // 2, nb // PREP_BLOCKS),
        in_specs=[pair, pair, pair],
        out_specs=[t_spec, k_spec, v_spec],
        out_shape=[jax.ShapeDtypeStruct((MOBA_HEADS, nb, LANES, blk), BF16),
                   jax.ShapeDtypeStruct((MOBA_HEADS, nb, blk, LANES), BF16),
                   jax.ShapeDtypeStruct((MOBA_HEADS, nb // KV_GROUP, V_ROWS, KV_GROUP * blk), BF16)],
        scratch_shapes=[pltpu.VMEM((MOBA_HD, LANES), F32)],
        compiler_params=pltpu.CompilerParams(dimension_semantics=("parallel", "arbitrary"),
                                             vmem_limit_bytes=VMEM_LIMIT),
        name="moba_prep",
    )(qm, km, vm)


def _moba_attn_kernel(qi_ref, gi_ref, qt_ref, ka_ref, vt_ref, bias_ref, o_ref, s0_ref, s1_ref, p0_ref, p1_ref,
                      acc_ref):
    n_pairs = qi_ref.shape[0] - 2
    blk = MOBA_BLOCK
    s_refs = (s0_ref, s1_ref)
    p_refs = (p0_ref, p1_ref)

    def scores(t, slot):
        s_ref = s_refs[slot]
        qi = qi_ref[t]
        gi = gi_ref[t]
        qt = qt_ref[0, qi]
        col_max = None
        for b in range(KV_GROUP):
            j = gi * KV_GROUP + b
            s = jnp.dot(ka_ref[0, j], qt, preferred_element_type=F32).astype(BF16)
            s = s + bias_ref[0, jnp.clip(qi - j, 0, BIAS_TILES - 1)]
            s_ref[b] = s
            cm = jnp.max(s, axis=0, keepdims=True)
            col_max = cm if col_max is None else jnp.maximum(col_max, cm)
        return col_max.astype(F32)

    def softmax(t, slot, m_run, col_max):
        m_prev = jnp.where(gi_ref[t] == 0, -jnp.inf, m_run)
        m = jnp.maximum(m_prev, col_max)
        for b in range(KV_GROUP):
            p_refs[slot][b * blk:(b + 1) * blk, :] = jnp.exp2(s_refs[slot][b] - m.astype(BF16))
        return m, jnp.exp2(m_prev - m)

    def values(t, slot, alpha):
        acc = acc_ref[...] * alpha + jnp.dot(vt_ref[0, gi_ref[t]], p_refs[slot][...], preferred_element_type=F32)
        acc_ref[...] = acc
        o_ref[0, qi_ref[t]] = (acc[0:MOBA_HD] / acc[MOBA_HD:MOBA_HD + 1]).astype(o_ref.dtype)

    acc_ref[...] = jnp.zeros_like(acc_ref)
    cm0 = scores(0, 0)
    cm1 = scores(1, 1)
    m0, alpha0 = softmax(0, 0, jnp.full((1, blk), -jnp.inf, F32), cm0)

    unroll = max(u for u in range(2, ATTN_UNROLL + 1, 2) if n_pairs % u == 0)

    def step(u, carry):
        m, alpha, cm = carry
        for i in range(unroll):
            t = unroll * u + i
            cm_next = scores(t + 2, i % 2)
            m, alpha_next = softmax(t + 1, 1 - i % 2, m, cm)
            values(t, i % 2, alpha)
            alpha, cm = alpha_next, cm_next
        return m, alpha, cm

    lax.fori_loop(0, n_pairs // unroll, step, (m0, alpha0, cm1))


def _moba_attn(qt, ka, vt, bias):
    nb = qt.shape[1]
    blk = MOBA_BLOCK
    pairs = [(qi, g) for qi in range(nb) for g in range(qi // KV_GROUP + 1)] + [(0, 0), (0, 0)]
    qi_tab = jnp.asarray([a for a, _ in pairs], jnp.int32)
    gi_tab = jnp.asarray([b for _, b in pairs], jnp.int32)
    head = lambda shape: pl.BlockSpec((1,) + shape, lambda h, *_: (h,) + (0,) * len(shape))
    return pl.pallas_call(
        _moba_attn_kernel,
        grid_spec=pltpu.PrefetchScalarGridSpec(
            num_scalar_prefetch=2,
            grid=(MOBA_HEADS,),
            in_specs=[head((nb, LANES, blk)), head((nb, blk, LANES)),
                      head((nb // KV_GROUP, V_ROWS, KV_GROUP * blk)), head((BIAS_TILES, blk, blk))],
            out_specs=head((nb, MOBA_HD, blk)),
            scratch_shapes=[pltpu.VMEM((KV_GROUP, blk, blk), BF16), pltpu.VMEM((KV_GROUP, blk, blk), BF16),
                            pltpu.VMEM((KV_GROUP * blk, blk), BF16), pltpu.VMEM((KV_GROUP * blk, blk), BF16),
                            pltpu.VMEM((V_ROWS, blk), F32)]),
        out_shape=jax.ShapeDtypeStruct((MOBA_HEADS, nb, MOBA_HD, blk), BF16),
        compiler_params=pltpu.CompilerParams(dimension_semantics=("parallel",), vmem_limit_bytes=VMEM_LIMIT),
        name="moba_attn",
    )(qi_tab, gi_tab, qt, ka, vt, bias)


def _out_ffn_kernel(final, h_ref, og_ref, om_ref, p_ref, wog_ref, wom_ref, n3_ref, wgu_ref, wd_ref,
                    n4_ref, wpg_ref, wpp_ref, nf_ref, o_ref):
    mix = jnp.dot(og_ref[...].astype(BF16), wog_ref[...], preferred_element_type=F32)
    om = [om_ref[:, b].reshape(MOBA_W, MOBA_BLOCK).astype(F32).T.astype(BF16) for b in range(om_ref.shape[1])]
    mix = mix + jnp.dot(jnp.concatenate(om, axis=0), wom_ref[...], preferred_element_type=F32)
    h = h_ref[...] + mix
    h = h + 0.5 * _swiglu(_rms(h, n3_ref[...]).astype(BF16), wgu_ref, wd_ref)
    gate = jax.nn.sigmoid(jnp.dot(_rms(h, n4_ref[...]).astype(BF16), wpg_ref[...], preferred_element_type=F32))
    h = h + gate * jnp.dot(p_ref[...].astype(BF16), wpp_ref[...], preferred_element_type=F32)
    o_ref[...] = _rms(h, nf_ref[...]) if final else h


def _out_ffn(final, h, og, om, p, wog, wom, n3, wgu, wd, n4, wpg, wpp, nf):
    t, d = h.shape
    tm = min(ROW_TILE, t)
    row = lambda w: pl.BlockSpec((tm, w), lambda i: (i, 0))
    consts = (wog, wom, n3, wgu, wd, n4, wpg, wpp, nf)
    return pl.pallas_call(
        functools.partial(_out_ffn_kernel, final),
        grid=(t // tm,),
        in_specs=[row(d), row(GLA_V),
                  pl.BlockSpec((MOBA_HEADS, tm // MOBA_BLOCK, MOBA_HD, MOBA_BLOCK), lambda i: (0, i, 0, 0)),
                  row(p.shape[1])]
        + [_const_spec(a.shape) for a in consts],
        out_specs=row(d),
        out_shape=jax.ShapeDtypeStruct((t, d), F32),
        compiler_params=pltpu.CompilerParams(dimension_semantics=("parallel",), vmem_limit_bytes=VMEM_LIMIT),
        name="out_ffn",
    )(h, og, om, p, *consts)


def kernel(x, p, ffn1_norm, ffn1_w_gu, ffn1_w_down, mix_norm, w_in, w_gk_up, b_gk, gla_out_norm, w_out,
           ffn2_norm, ffn2_w_gu, ffn2_w_down, ple_norm, w_ple_gate, w_ple_proj, rel_bias, final_norm):
    batch, t, d = x.shape
    depth = p.shape[0]
    assert batch == 1 and t % (PREP_BLOCKS * MOBA_BLOCK) == 0
    row = lambda a: a.reshape(1, -1)
    bias = _moba_bias(rel_bias)
    lr0 = 2 * GLA_QK + 2 * GLA_V
    h = x[0]
    for i in range(depth):
        wi = w_in[i]
        wa = wi[:, :lr0].astype(BF16)
        wlr = jnp.pad(wi[:, lr0:lr0 + GLA_LOWRANK], ((0, 0), (0, LANES - GLA_LOWRANK))).astype(BF16)
        wm = wi[:, lr0 + GLA_LOWRANK:].astype(BF16)
        wgk = jnp.pad(w_gk_up[i], ((0, LANES - GLA_LOWRANK), (0, 0))).astype(BF16)
        h1, qg, kg, vg, gg, la, qm, km, vm = _ffn_in(
            h, row(ffn1_norm[i]), ffn1_w_gu[i].astype(BF16), ffn1_w_down[i].astype(BF16), row(mix_norm[i]),
            wa, wlr, wm, wgk, row(b_gk[i]))
        og = _gla(qg, kg, la, vg, gg, row(gla_out_norm[i]))
        qt, ka, vt = _moba_prep(qm, km, vm)
        om = _moba_attn(qt, ka, vt, bias)
        wo = w_out[i].astype(BF16)
        h = _out_ffn(i == depth - 1, h1, og, om, p[i, 0], wo[:GLA_V], wo[GLA_V:], row(ffn2_norm[i]),
                     ffn2_w_gu[i].astype(BF16), ffn2_w_down[i].astype(BF16), row(ple_norm[i]),
                     w_ple_gate[i].astype(BF16), w_ple_proj[i].astype(BF16), row(final_norm))
    return h[None]
```

```python
import functools
import math

import jax
import jax.numpy as jnp
import numpy as np
from jax import lax
from jax.experimental import pallas as pl
from jax.experimental.pallas import tpu as pltpu

F32 = jnp.float32
BF16 = jnp.bfloat16

GLA_HEADS = 4
GLA_DK = 64
GLA_DV = 128
GLA_LOWRANK = 16
GLA_TAU = 16.0
MOBA_HEADS = 8
MOBA_HD = 64
MOBA_BLOCK = 256
MOBA_TOPK = 3
REL_BUCKETS = 32
REL_MAX_DIST = 4096
EPS = 1e-6
NEG = -1e30

GLA_QK = GLA_HEADS * GLA_DK
GLA_V = GLA_HEADS * GLA_DV
MOBA_W = MOBA_HEADS * MOBA_HD

LANES = 128
VMEM_LIMIT = 56 * 1024 * 1024

ROW_TILE = 512
GLA_TILE = 256
GLA_LEVELS = tuple(GLA_TILE >> (i + 1) for i in range(GLA_TILE.bit_length() - 1))
_FAR_DIST = math.ceil((REL_BUCKETS // 2) * (REL_MAX_DIST / (REL_BUCKETS // 2))
                      ** ((REL_BUCKETS // 2 - 1) / (REL_BUCKETS // 2)))
_FAR_BLOCKS = -(-(_FAR_DIST - 1) // MOBA_BLOCK) + 1
BIAS_TILES = _FAR_BLOCKS + 1
KV_GROUP = 2
ATTN_UNROLL = 32
V_ROWS = 80
LOG2E = 1.4426950408889634


def _rms(x, g):
    return x * lax.rsqrt(jnp.mean(x * x, axis=-1, keepdims=True) + EPS) * g


def _ff_chunks(d_ff):
    step = 1024
    return [(s, min(step, d_ff - s)) for s in range(0, d_ff, step)]


def _swiglu(xn, wgu_ref, wd_ref):
    d_ff = wd_ref.shape[0]
    acc = None
    for start, size in _ff_chunks(d_ff):
        gate = jnp.dot(xn, wgu_ref[:, start:start + size], preferred_element_type=F32)
        up = jnp.dot(xn, wgu_ref[:, d_ff + start:d_ff + start + size], preferred_element_type=F32)
        act = (gate * jax.nn.sigmoid(gate) * up).astype(BF16)
        part = jnp.dot(act, wd_ref[start:start + size, :], preferred_element_type=F32)
        acc = part if acc is None else acc + part
    return acc


def _const_spec(shape):
    return pl.BlockSpec(shape, lambda *_: (0,) * len(shape))


def _moba_operands(pair, q, k, v, first_block, km_ref, qt_ref, ka_ref, vt_ref):
    blk = MOBA_BLOCK
    rows = q.shape[0]
    n_blocks = rows // blk
    hd = MOBA_HD
    means = jnp.mean(k.reshape(n_blocks, blk, LANES), axis=1)
    km = km_ref[pair]
    km_row = lax.broadcasted_iota(jnp.int32, km.shape, 0)
    for b in range(n_blocks):
        km = jnp.where(km_row == first_block + b, means[b:b + 1], km)
    km_ref[pair] = km
    km_lane = lax.broadcasted_iota(jnp.int32, km.shape, 1)
    q_t = q.T
    v_t = v.T
    block = lax.broadcasted_iota(jnp.int32, (hd, rows), 0)
    block_f = block.astype(F32)
    own = first_block + lax.broadcasted_iota(jnp.int32, (hd, rows), 1) // blk
    lane = lax.broadcasted_iota(jnp.int32, (rows, LANES), 1)
    own_k = first_block + lax.broadcasted_iota(jnp.int32, (rows, LANES), 0) // blk
    ones_rows = jnp.where(lax.broadcasted_iota(jnp.int32, (V_ROWS - hd, rows), 0) == 0, 1.0, 0.0)
    for a in range(2):
        head = 2 * pair + a
        gate = jnp.dot(jnp.where(km_lane // hd == a, km, 0.0), q_t, precision=lax.Precision.HIGHEST,
                       preferred_element_type=F32)
        gate = jnp.where(block < own, gate, -jnp.inf)
        chosen = jnp.where(block == own, 1.0, 0.0)
        for _ in range(MOBA_TOPK):
            top = jnp.max(gate, axis=0, keepdims=True)
            first = jnp.min(jnp.where(gate == top, block_f, float(hd)), axis=0, keepdims=True)
            hit = block_f == first
            chosen = jnp.where(hit, 1.0, chosen)
            gate = jnp.where(hit, -jnp.inf, gate)
        q_a = q_t[a * hd:(a + 1) * hd] * (hd ** -0.5 * LOG2E)
        q_aug_t = jnp.concatenate([q_a, chosen - 1.0] if a == 0 else [chosen - 1.0, q_a], axis=0).astype(BF16)
        v_aug_t = jnp.concatenate([v_t[a * hd:(a + 1) * hd], ones_rows], axis=0).astype(BF16)
        k_aug = jnp.where(lane // hd == a, k, jnp.where(lane - (1 - a) * hd == own_k, -NEG, 0.0)).astype(BF16)
        for b in range(n_blocks):
            qt_ref[head, b] = q_aug_t[:, b * blk:(b + 1) * blk]
            ka_ref[head, b] = k_aug[b * blk:(b + 1) * blk, :]
        group = KV_GROUP * blk
        for g in range(n_blocks // KV_GROUP):
            vt_ref[head, g] = v_aug_t[:, g * group:(g + 1) * group]


def _ffn_in_kernel(x_ref, n1_ref, wgu_ref, wd_ref, n2_ref, wa_ref, wlr_ref, wm_ref, wgk_ref, bgk_ref,
                   h_ref, qg_ref, kg_ref, vg_ref, gg_ref, la_ref, qt_ref, ka_ref, vt_ref, km_ref):
    @pl.when(pl.program_id(0) == 0)
    def _():
        km_ref[...] = jnp.zeros_like(km_ref)

    x = x_ref[...]
    h = x + 0.5 * _swiglu(_rms(x, n1_ref[...]).astype(BF16), wgu_ref, wd_ref)
    h_ref[...] = h
    n = _rms(h, n2_ref[...]).astype(BF16)
    ua = jnp.dot(n, wa_ref[...], preferred_element_type=F32)
    qg_ref[...] = ua[:, 0:GLA_QK]
    kg_ref[...] = ua[:, GLA_QK:2 * GLA_QK]
    vg_ref[...] = ua[:, 2 * GLA_QK:2 * GLA_QK + GLA_V]
    gg_ref[...] = ua[:, 2 * GLA_QK + GLA_V:]
    lr = jnp.dot(n, wlr_ref[...], preferred_element_type=F32)
    z = jnp.dot(lr.astype(BF16), wgk_ref[...], preferred_element_type=F32) + bgk_ref[...]
    la_ref[...] = (jnp.minimum(z, 0.0) - jnp.log1p(jnp.exp(-jnp.abs(z)))) * (1.0 / GLA_TAU)
    um = jnp.dot(n, wm_ref[...], preferred_element_type=F32)
    first_block = pl.program_id(0) * (x.shape[0] // MOBA_BLOCK)
    for pair in range(MOBA_HEADS // 2):
        q, k, v = (um[:, c * MOBA_W + pair * LANES:c * MOBA_W + (pair + 1) * LANES] for c in range(3))
        _moba_operands(pair, q, k, v, first_block, km_ref, qt_ref, ka_ref, vt_ref)


def _ffn_in(x, n1, wgu, wd, n2, wa, wlr, wm, wgk, bgk):
    t, d = x.shape
    tm = min(ROW_TILE, t)
    blk = MOBA_BLOCK
    nb = t // blk
    assert nb <= MOBA_HD
    step_blocks = tm // blk
    row = lambda w: pl.BlockSpec((tm, w), lambda i: (i, 0))
    outs = [(d, F32), (GLA_QK, F32), (GLA_QK, F32), (GLA_V, F32), (GLA_V, F32), (GLA_QK, F32)]
    heads = lambda shape: pl.BlockSpec((MOBA_HEADS,) + shape, lambda i: (0, i, 0, 0))
    return pl.pallas_call(
        _ffn_in_kernel,
        grid=(t // tm,),
        in_specs=[row(d)] + [_const_spec(a.shape) for a in (n1, wgu, wd, n2, wa, wlr, wm, wgk, bgk)],
        out_specs=[row(w) for w, _ in outs] + [heads((step_blocks, LANES, blk)), heads((step_blocks, blk, LANES)),
                                               heads((step_blocks // KV_GROUP, V_ROWS, KV_GROUP * blk))],
        out_shape=[jax.ShapeDtypeStruct((t, w), dt) for w, dt in outs]
        + [jax.ShapeDtypeStruct((MOBA_HEADS, nb, LANES, blk), BF16),
           jax.ShapeDtypeStruct((MOBA_HEADS, nb, blk, LANES), BF16),
           jax.ShapeDtypeStruct((MOBA_HEADS, nb // KV_GROUP, V_ROWS, KV_GROUP * blk), BF16)],
        scratch_shapes=[pltpu.VMEM((MOBA_HEADS // 2, MOBA_HD, LANES), F32)],
        compiler_params=pltpu.CompilerParams(dimension_semantics=("arbitrary",), vmem_limit_bytes=VMEM_LIMIT),
        name="ffn_in",
    )(x, n1, wgu, wd, n2, wa, wlr, wm, wgk, bgk)


def _gla_span_matrix(tile):
    i = np.arange(tile)[:, None]
    t = np.arange(tile)[None, :]
    mats = [t <= i, t > i]
    for s in GLA_LEVELS:
        split = (i // (2 * s)) * (2 * s) + s - 1
        mats.append(np.where((i & s) != 0, (t > split) & (t <= i), (t > i) & (t <= split)))
    return jnp.asarray(np.concatenate(mats, axis=0), BF16)


def _gla_kernel(q_ref, k_ref, la_ref, v_ref, g_ref, gn_ref, span_ref, o_ref, st_ref):
    @pl.when(pl.program_id(0) == 0)
    def _():
        st_ref[...] = jnp.zeros_like(st_ref)

    tb = q_ref.shape[0]
    half = tb // 2
    la = la_ref[...]
    la_hi = la.astype(BF16)
    la_lo = (la - la_hi.astype(F32)).astype(BF16)

    def span_sum(r):
        m = span_ref[r * tb:(r + 1) * tb, :]
        return jnp.dot(m, la_hi, preferred_element_type=F32) + jnp.dot(m, la_lo, preferred_element_type=F32)

    b = span_sum(0)
    q = q_ref[...] * (GLA_DK ** -0.5)
    k = k_ref[...]
    qe = q * jnp.exp(b)
    ke = k * jnp.exp(span_sum(1))
    a_last = jnp.exp(b[tb - 1:tb, :])
    row = lax.broadcasted_iota(jnp.int32, (tb, GLA_QK), 0)
    q_lvl, k_lvl = [], []
    for idx, s in enumerate(GLA_LEVELS):
        e = jnp.exp(span_sum(2 + idx))
        upper = (row & s) != 0
        q_lvl.append(jnp.where(upper, q * e, 0.0))
        k_lvl.append(jnp.where(upper, 0.0, k * e))
    ri = lax.broadcasted_iota(jnp.int32, (half, 2 * half), 0)
    ci = lax.broadcasted_iota(jnp.int32, (half, 2 * half), 1) % half
    same_block = [ri // (2 * s) == ci // (2 * s) for s in GLA_LEVELS[1:]]
    diagonal = ri == ci
    first_head = lax.broadcasted_iota(jnp.int32, (half, 2 * GLA_DK), 1) < GLA_DK
    gn = gn_ref[...]
    nt = (((1,), (1,)), ((), ()))

    def pair(x, y, r0, c0, lanes):
        keys = y[c0:c0 + half, lanes]
        keys = jnp.concatenate([jnp.where(first_head, keys, 0.0), jnp.where(first_head, 0.0, keys)], axis=0)
        return lax.dot_general(x[r0:r0 + half, lanes].astype(BF16), keys.astype(BF16), nt,
                               preferred_element_type=F32)

    intra = []
    for hp in range(GLA_HEADS // 2):
        lanes = slice(2 * hp * GLA_DK, 2 * (hp + 1) * GLA_DK)
        diag_blocks = []
        for c in (0, half):
            blk = jnp.where(diagonal, pair(q, k, c, c, lanes), 0.0)
            for idx in range(1, len(GLA_LEVELS)):
                blk = blk + jnp.where(same_block[idx - 1], pair(q_lvl[idx], k_lvl[idx], c, c, lanes), 0.0)
            diag_blocks.append(blk)
        low = pair(q_lvl[0], k_lvl[0], half, 0, lanes)
        for a in range(2):
            cols = slice(a * half, (a + 1) * half)
            intra.append(jnp.concatenate(
                [jnp.concatenate([diag_blocks[0][:, cols], jnp.zeros((half, half), F32)], axis=1),
                 jnp.concatenate([low[:, cols], diag_blocks[1][:, cols]], axis=1)], axis=0).astype(BF16))
    for h in range(GLA_HEADS):
        ks = slice(h * GLA_DK, (h + 1) * GLA_DK)
        vs = slice(h * GLA_DV, (h + 1) * GLA_DV)
        attn = intra[h]
        v = v_ref[:, vs].astype(BF16)
        st = st_ref[h]
        o = jnp.dot(attn, v, preferred_element_type=F32)
        o = o + lax.dot_general(qe[:, ks].astype(BF16), st.astype(BF16), nt, preferred_element_type=F32)
        st_ref[h] = st * a_last[:, ks] + lax.dot_general(v, ke[:, ks].astype(BF16), (((0,), (0,)), ((), ())),
                                                         preferred_element_type=F32)
        g = g_ref[:, vs]
        o_ref[:, vs] = _rms(o, gn) * (g * jax.nn.sigmoid(g))


def _gla(qg, kg, la, vg, gg, gn):
    t = qg.shape[0]
    tb = GLA_TILE
    span = _gla_span_matrix(tb)
    row = lambda w: pl.BlockSpec((tb, w), lambda i: (i, 0))
    return pl.pallas_call(
        _gla_kernel,
        grid=(t // tb,),
        in_specs=[row(GLA_QK), row(GLA_QK), row(GLA_QK), row(GLA_V), row(GLA_V), _const_spec(gn.shape),
                  _const_spec(span.shape)],
        out_specs=row(GLA_V),
        out_shape=jax.ShapeDtypeStruct((t, GLA_V), F32),
        scratch_shapes=[pltpu.VMEM((GLA_HEADS, GLA_DV, GLA_DK), F32)],
        compiler_params=pltpu.CompilerParams(dimension_semantics=("arbitrary",), vmem_limit_bytes=VMEM_LIMIT),
        name="gla",
    )(qg, kg, la, vg, gg, gn, span)


def _moba_bias_kernel(tab_ref, o_ref):
    h = pl.program_id(0)
    blk = MOBA_BLOCK
    max_exact = REL_BUCKETS // 2
    x = lax.broadcasted_iota(jnp.int32, (8, 2 * blk), 1)
    for delta in range(BIAS_TILES):
        dist = delta * blk + x - blk
        n = jnp.maximum(dist, 0)
        nf = jnp.maximum(n, 1).astype(F32)
        large = max_exact + (jnp.log(nf / max_exact) / math.log(REL_MAX_DIST / max_exact)
                             * (REL_BUCKETS - max_exact)).astype(jnp.int32)
        large = jnp.minimum(large, REL_BUCKETS - 1)
        bucket = jnp.where(n < max_exact, n, large)
        bias = jnp.zeros(dist.shape, F32)
        for b in range(REL_BUCKETS):
            bias = jnp.where(bucket == b, tab_ref[b * MOBA_HEADS + h], bias)
        by_dist = jnp.where(dist >= 0, bias * LOG2E, NEG)
        rows = jnp.broadcast_to(by_dist[0:1], (blk, 2 * blk))
        o_ref[0, delta] = pltpu.roll(rows, 0, 1, stride=1, stride_axis=0)[:, blk:].astype(o_ref.dtype)


def _moba_bias(rel_bias):
    blk = MOBA_BLOCK
    return pl.pallas_call(
        _moba_bias_kernel,
        grid=(MOBA_HEADS,),
        in_specs=[pl.BlockSpec(memory_space=pltpu.SMEM)],
        out_specs=pl.BlockSpec((1, BIAS_TILES, blk, blk), lambda h: (h, 0, 0, 0)),
        out_shape=jax.ShapeDtypeStruct((MOBA_HEADS, BIAS_TILES, blk, blk), BF16),
        compiler_params=pltpu.CompilerParams(dimension_semantics=("parallel",)),
        name="moba_bias",
    )(rel_bias.reshape(-1))


def _moba_attn_kernel(qi_ref, gi_ref, qt_ref, ka_ref, vt_ref, bias_ref, o_ref, s0_ref, s1_ref, p0_ref, p1_ref,
                      acc_ref):
    n_pairs = qi_ref.shape[0] - 2
    blk = MOBA_BLOCK
    s_refs = (s0_ref, s1_ref)
    p_refs = (p0_ref, p1_ref)

    def scores(t, slot):
        s_ref = s_refs[slot]
        qi = qi_ref[t]
        gi = gi_ref[t]
        qt = qt_ref[0, qi]
        col_max = None
        for b in range(KV_GROUP):
            j = gi * KV_GROUP + b
            s = jnp.dot(ka_ref[0, j], qt, preferred_element_type=F32).astype(BF16)
            s = s + bias_ref[0, jnp.clip(qi - j, 0, BIAS_TILES - 1)]
            s_ref[b] = s
            cm = jnp.max(s, axis=0, keepdims=True)
            col_max = cm if col_max is None else jnp.maximum(col_max, cm)
        return col_max.astype(F32)

    def softmax(t, slot, m_run, col_max):
        m_prev = jnp.where(gi_ref[t] == 0, -jnp.inf, m_run)
        m = jnp.maximum(m_prev, col_max)
        for b in range(KV_GROUP):
            p_refs[slot][b * blk:(b + 1) * blk, :] = jnp.exp2(s_refs[slot][b] - m.astype(BF16))
        return m, jnp.exp2(m_prev - m)

    def values(t, slot, alpha):
        acc = acc_ref[...] * alpha + jnp.dot(vt_ref[0, gi_ref[t]], p_refs[slot][...], preferred_element_type=F32)
        acc_ref[...] = acc
        o_ref[0, qi_ref[t]] = (acc[0:MOBA_HD] / acc[MOBA_HD:MOBA_HD + 1]).astype(o_ref.dtype)

    acc_ref[...] = jnp.zeros_like(acc_ref)
    cm0 = scores(0, 0)
    cm1 = scores(1, 1)
    m0, alpha0 = softmax(0, 0, jnp.full((1, blk), -jnp.inf, F32), cm0)

    unroll = max(u for u in range(2, ATTN_UNROLL + 1, 2) if n_pairs % u == 0)

    def step(u, carry):
        m, alpha, cm = carry
        for i in range(unroll):
            t = unroll * u + i
            cm_next = scores(t + 2, i % 2)
            m, alpha_next = softmax(t + 1, 1 - i % 2, m, cm)
            values(t, i % 2, alpha)
            alpha, cm = alpha_next, cm_next
        return m, alpha, cm

    lax.fori_loop(0, n_pairs // unroll, step, (m0, alpha0, cm1))


def _moba_attn(qt, ka, vt, bias):
    nb = qt.shape[1]
    blk = MOBA_BLOCK
    pairs = [(qi, g) for qi in range(nb) for g in range(qi // KV_GROUP + 1)] + [(0, 0), (0, 0)]
    qi_tab = jnp.asarray([a for a, _ in pairs], jnp.int32)
    gi_tab = jnp.asarray([b for _, b in pairs], jnp.int32)
    head = lambda shape: pl.BlockSpec((1,) + shape, lambda h, *_: (h,) + (0,) * len(shape))
    return pl.pallas_call(
        _moba_attn_kernel,
        grid_spec=pltpu.PrefetchScalarGridSpec(
            num_scalar_prefetch=2,
            grid=(MOBA_HEADS,),
            in_specs=[head((nb, LANES, blk)), head((nb, blk, LANES)),
                      head((nb // KV_GROUP, V_ROWS, KV_GROUP * blk)), head((BIAS_TILES, blk, blk))],
            out_specs=head((nb, MOBA_HD, blk)),
            scratch_shapes=[pltpu.VMEM((KV_GROUP, blk, blk), BF16), pltpu.VMEM((KV_GROUP, blk, blk), BF16),
                            pltpu.VMEM((KV_GROUP * blk, blk), BF16), pltpu.VMEM((KV_GROUP * blk, blk), BF16),
                            pltpu.VMEM((V_ROWS, blk), F32)]),
        out_shape=jax.ShapeDtypeStruct((MOBA_HEADS, nb, MOBA_HD, blk), BF16),
        compiler_params=pltpu.CompilerParams(dimension_semantics=("parallel",), vmem_limit_bytes=VMEM_LIMIT),
        name="moba_attn",
    )(qi_tab, gi_tab, qt, ka, vt, bias)


def _out_ffn_kernel(final, h_ref, og_ref, om_ref, p_ref, wog_ref, wom_ref, n3_ref, wgu_ref, wd_ref,
                    n4_ref, wpg_ref, wpp_ref, nf_ref, o_ref):
    mix = jnp.dot(og_ref[...].astype(BF16), wog_ref[...], preferred_element_type=F32)
    om = [om_ref[:, b].reshape(MOBA_W, MOBA_BLOCK).astype(F32).T.astype(BF16) for b in range(om_ref.shape[1])]
    mix = mix + jnp.dot(jnp.concatenate(om, axis=0), wom_ref[...], preferred_element_type=F32)
    h = h_ref[...] + mix
    h = h + 0.5 * _swiglu(_rms(h, n3_ref[...]).astype(BF16), wgu_ref, wd_ref)
    gate = jax.nn.sigmoid(jnp.dot(_rms(h, n4_ref[...]).astype(BF16), wpg_ref[...], preferred_element_type=F32))
    h = h + gate * jnp.dot(p_ref[...].astype(BF16), wpp_ref[...], preferred_element_type=F32)
    o_ref[...] = _rms(h, nf_ref[...]) if final else h


def _out_ffn(final, h, og, om, p, wog, wom, n3, wgu, wd, n4, wpg, wpp, nf):
    t, d = h.shape
    tm = min(ROW_TILE, t)
    row = lambda w: pl.BlockSpec((tm, w), lambda i: (i, 0))
    consts = (wog, wom, n3, wgu, wd, n4, wpg, wpp, nf)
    return pl.pallas_call(
        functools.partial(_out_ffn_kernel, final),
        grid=(t // tm,),
        in_specs=[row(d), row(GLA_V),
                  pl.BlockSpec((MOBA_HEADS, tm // MOBA_BLOCK, MOBA_HD, MOBA_BLOCK), lambda i: (0, i, 0, 0)),
                  row(p.shape[1])]
        + [_const_spec(a.shape) for a in consts],
        out_specs=row(d),
        out_shape=jax.ShapeDtypeStruct((t, d), F32),
        compiler_params=pltpu.CompilerParams(dimension_semantics=("parallel",), vmem_limit_bytes=VMEM_LIMIT),
        name="out_ffn",
    )(h, og, om, p, *consts)


def kernel(x, p, ffn1_norm, ffn1_w_gu, ffn1_w_down, mix_norm, w_in, w_gk_up, b_gk, gla_out_norm, w_out,
           ffn2_norm, ffn2_w_gu, ffn2_w_down, ple_norm, w_ple_gate, w_ple_proj, rel_bias, final_norm):
    batch, t, d = x.shape
    depth = p.shape[0]
    assert batch == 1 and t % ROW_TILE == 0 and ROW_TILE % (KV_GROUP * MOBA_BLOCK) == 0
    row = lambda a: a.reshape(1, -1)
    bias = _moba_bias(rel_bias)
    lr0 = 2 * GLA_QK + 2 * GLA_V
    h = x[0]
    for i in range(depth):
        wi = w_in[i]
        wa = wi[:, :lr0].astype(BF16)
        wlr = jnp.pad(wi[:, lr0:lr0 + GLA_LOWRANK], ((0, 0), (0, LANES - GLA_LOWRANK))).astype(BF16)
        wm = wi[:, lr0 + GLA_LOWRANK:].astype(BF16)
        wgk = jnp.pad(w_gk_up[i], ((0, LANES - GLA_LOWRANK), (0, 0))).astype(BF16)
        h1, qg, kg, vg, gg, la, qt, ka, vt = _ffn_in(
            h, row(ffn1_norm[i]), ffn1_w_gu[i].astype(BF16), ffn1_w_down[i].astype(BF16), row(mix_norm[i]),
            wa, wlr, wm, wgk, row(b_gk[i]))
        og = _gla(qg, kg, la, vg, gg, row(gla_out_norm[i]))
        om = _moba_attn(qt, ka, vt, bias)
        wo = w_out[i].astype(BF16)
        h = _out_ffn(i == depth - 1, h1, og, om, p[i, 0], wo[:GLA_V], wo[GLA_V:], row(ffn2_norm[i]),
                     ffn2_w_gu[i].astype(BF16), ffn2_w_down[i].astype(BF16), row(ple_norm[i]),
                     w_ple_gate[i].astype(BF16), w_ple_proj[i].astype(BF16), row(final_norm))
    return h[None]
```

```python
import functools
import math

import jax
import jax.numpy as jnp
import numpy as np
from jax import lax
from jax.experimental import pallas as pl
from jax.experimental.pallas import tpu as pltpu

F32 = jnp.float32
BF16 = jnp.bfloat16

GLA_HEADS = 4
GLA_DK = 64
GLA_DV = 128
GLA_LOWRANK = 16
GLA_TAU = 16.0
MOBA_HEADS = 8
MOBA_HD = 64
MOBA_BLOCK = 256
MOBA_TOPK = 3
REL_BUCKETS = 32
REL_MAX_DIST = 4096
EPS = 1e-6
NEG = -1e30

GLA_QK = GLA_HEADS * GLA_DK
GLA_V = GLA_HEADS * GLA_DV
MOBA_W = MOBA_HEADS * MOBA_HD

LANES = 128
BF16_SUBLANES = 16
VMEM_LIMIT = 56 * 1024 * 1024

ROW_TILE = 512
GLA_TILE = 256
GLA_LEVELS = tuple(GLA_TILE >> (i + 1) for i in range(GLA_TILE.bit_length() - 1))
_FAR_DIST = math.ceil((REL_BUCKETS // 2) * (REL_MAX_DIST / (REL_BUCKETS // 2))
                      ** ((REL_BUCKETS // 2 - 1) / (REL_BUCKETS // 2)))
_FAR_BLOCKS = -(-(_FAR_DIST - 1) // MOBA_BLOCK) + 1
BIAS_TILES = _FAR_BLOCKS + 1
KV_GROUP = 2
ATTN_UNROLL = 32
V_ROWS = 80
LOG2E = 1.4426950408889634


def _rms(x, g):
    return x * lax.rsqrt(jnp.mean(x * x, axis=-1, keepdims=True) + EPS) * g


def _ff_chunks(d_ff):
    step = 1024
    return [(s, min(step, d_ff - s)) for s in range(0, d_ff, step)]


def _swiglu(xn, wgu_ref, wd_ref):
    d_ff = wd_ref.shape[0]
    acc = None
    for start, size in _ff_chunks(d_ff):
        gate = jnp.dot(xn, wgu_ref[:, start:start + size], preferred_element_type=F32)
        up = jnp.dot(xn, wgu_ref[:, d_ff + start:d_ff + start + size], preferred_element_type=F32)
        act = (gate * jax.nn.sigmoid(gate) * up).astype(BF16)
        part = jnp.dot(act, wd_ref[start:start + size, :], preferred_element_type=F32)
        acc = part if acc is None else acc + part
    return acc


def _const_spec(shape):
    return pl.BlockSpec(shape, lambda *_: (0,) * len(shape))


def _moba_operands(pair, q, k, v, first_block, km_ref, qt_ref, ka_ref, vt_ref):
    blk = MOBA_BLOCK
    rows = q.shape[0]
    n_blocks = rows // blk
    hd = MOBA_HD
    means = jnp.mean(k.reshape(n_blocks, blk, LANES), axis=1)
    km = km_ref[pair]
    km_row = lax.broadcasted_iota(jnp.int32, km.shape, 0)
    for b in range(n_blocks):
        km = jnp.where(km_row == first_block + b, means[b:b + 1], km)
    km_ref[pair] = km
    km_lane = lax.broadcasted_iota(jnp.int32, km.shape, 1)
    q_t = q.T
    v_t = v.T
    block = lax.broadcasted_iota(jnp.int32, (hd, rows), 0)
    block_f = block.astype(F32)
    own = first_block + lax.broadcasted_iota(jnp.int32, (hd, rows), 1) // blk
    lane = lax.broadcasted_iota(jnp.int32, (rows, LANES), 1)
    own_k = first_block + lax.broadcasted_iota(jnp.int32, (rows, LANES), 0) // blk
    ones_rows = jnp.where(lax.broadcasted_iota(jnp.int32, (V_ROWS - hd, rows), 0) == 0, 1.0, 0.0)
    for a in range(2):
        head = 2 * pair + a
        gate = jnp.dot(jnp.where(km_lane // hd == a, km, 0.0), q_t, precision=lax.Precision.HIGHEST,
                       preferred_element_type=F32)
        gate = jnp.where(block < own, gate, -jnp.inf)
        chosen = jnp.where(block == own, 1.0, 0.0)
        for _ in range(MOBA_TOPK):
            top = jnp.max(gate, axis=0, keepdims=True)
            first = jnp.min(jnp.where(gate == top, block_f, float(hd)), axis=0, keepdims=True)
            hit = block_f == first
            chosen = jnp.where(hit, 1.0, chosen)
            gate = jnp.where(hit, -jnp.inf, gate)
        q_a = q_t[a * hd:(a + 1) * hd] * (hd ** -0.5 * LOG2E)
        q_aug_t = jnp.concatenate([q_a, chosen - 1.0] if a == 0 else [chosen - 1.0, q_a], axis=0).astype(BF16)
        v_aug_t = jnp.concatenate([v_t[a * hd:(a + 1) * hd], ones_rows], axis=0).astype(BF16)
        k_aug = jnp.where(lane // hd == a, k, jnp.where(lane - (1 - a) * hd == own_k, -NEG, 0.0)).astype(BF16)
        for b in range(n_blocks):
            qt_ref[head, b] = q_aug_t[:, b * blk:(b + 1) * blk]
            ka_ref[head, b] = k_aug[b * blk:(b + 1) * blk, :]
        group = KV_GROUP * blk
        for g in range(n_blocks // KV_GROUP):
            vt_ref[head, g] = v_aug_t[:, g * group:(g + 1) * group]


def _ffn_in_kernel(x_ref, n1_ref, wgu_ref, wd_ref, n2_ref, wa_ref, wlr_ref, wm_ref, wgk_ref, bgk_ref,
                   h_ref, qg_ref, kg_ref, vg_ref, gg_ref, la_ref, qt_ref, ka_ref, vt_ref, km_ref):
    @pl.when(pl.program_id(0) == 0)
    def _():
        km_ref[...] = jnp.zeros_like(km_ref)

    x = x_ref[...]
    h = x + 0.5 * _swiglu(_rms(x, n1_ref[...]).astype(BF16), wgu_ref, wd_ref)
    h_ref[...] = h
    n = _rms(h, n2_ref[...]).astype(BF16)
    ua = jnp.dot(n, wa_ref[...], preferred_element_type=F32)
    qg_ref[...] = ua[:, 0:GLA_QK]
    kg_ref[...] = ua[:, GLA_QK:2 * GLA_QK]
    vg_ref[...] = ua[:, 2 * GLA_QK:2 * GLA_QK + GLA_V]
    gg_ref[...] = ua[:, 2 * GLA_QK + GLA_V:]
    lr = jnp.dot(n, wlr_ref[...], preferred_element_type=F32)
    z = jnp.dot(lr.astype(BF16), wgk_ref[...], preferred_element_type=F32) + bgk_ref[...]
    la_ref[...] = (jnp.minimum(z, 0.0) - jnp.log1p(jnp.exp(-jnp.abs(z)))) * (1.0 / GLA_TAU)
    um = jnp.dot(n, wm_ref[...], preferred_element_type=F32)
    first_block = pl.program_id(0) * (x.shape[0] // MOBA_BLOCK)
    for pair in range(MOBA_HEADS // 2):
        q, k, v = (um[:, c * MOBA_W + pair * LANES:c * MOBA_W + (pair + 1) * LANES] for c in range(3))
        _moba_operands(pair, q, k, v, first_block, km_ref, qt_ref, ka_ref, vt_ref)


def _ffn_in(x, n1, wgu, wd, n2, wa, wlr, wm, wgk, bgk):
    t, d = x.shape
    tm = min(ROW_TILE, t)
    blk = MOBA_BLOCK
    nb = t // blk
    assert nb <= MOBA_HD
    step_blocks = tm // blk
    row = lambda w: pl.BlockSpec((tm, w), lambda i: (i, 0))
    outs = [(d, F32), (GLA_QK, F32), (GLA_QK, F32), (GLA_V, F32), (GLA_V, F32), (GLA_QK, F32)]
    heads = lambda shape: pl.BlockSpec((MOBA_HEADS,) + shape, lambda i: (0, i, 0, 0))
    return pl.pallas_call(
        _ffn_in_kernel,
        grid=(t // tm,),
        in_specs=[row(d)] + [_const_spec(a.shape) for a in (n1, wgu, wd, n2, wa, wlr, wm, wgk, bgk)],
        out_specs=[row(w) for w, _ in outs] + [heads((step_blocks, LANES, blk)), heads((step_blocks, blk, LANES)),
                                               heads((step_blocks // KV_GROUP, V_ROWS, KV_GROUP * blk))],
        out_shape=[jax.ShapeDtypeStruct((t, w), dt) for w, dt in outs]
        + [jax.ShapeDtypeStruct((MOBA_HEADS, nb, LANES, blk), BF16),
           jax.ShapeDtypeStruct((MOBA_HEADS, nb, blk, LANES), BF16),
           jax.ShapeDtypeStruct((MOBA_HEADS, nb // KV_GROUP, V_ROWS, KV_GROUP * blk), BF16)],
        scratch_shapes=[pltpu.VMEM((MOBA_HEADS // 2, MOBA_HD, LANES), F32)],
        compiler_params=pltpu.CompilerParams(dimension_semantics=("arbitrary",), vmem_limit_bytes=VMEM_LIMIT),
        name="ffn_in",
    )(x, n1, wgu, wd, n2, wa, wlr, wm, wgk, bgk)


def _gla_span_matrix(tile):
    i = np.arange(tile)[:, None]
    t = np.arange(tile)[None, :]
    mats = [t <= i, t > i]
    for s in GLA_LEVELS:
        split = (i // (2 * s)) * (2 * s) + s - 1
        mats.append(np.where((i & s) != 0, (t > split) & (t <= i), (t > i) & (t <= split)))
    return jnp.asarray(np.concatenate(mats, axis=0), BF16)


def _gla_kernel(n_cast, q_ref, k_ref, la_ref, v_ref, g_ref, gn_ref, span_ref, *refs):
    o_ref = refs[n_cast]
    st_ref = refs[2 * n_cast + 1]
    for w_ref, w_bf16_ref in zip(refs[:n_cast], refs[n_cast + 1:2 * n_cast + 1]):
        w_bf16_ref[...] = w_ref[...].astype(BF16)

    @pl.when(pl.program_id(0) == 0)
    def _():
        st_ref[...] = jnp.zeros_like(st_ref)

    tb = q_ref.shape[0]
    half = tb // 2
    la = la_ref[...]
    la_hi = la.astype(BF16)
    la_lo = (la - la_hi.astype(F32)).astype(BF16)

    def span_sum(r):
        m = span_ref[r * tb:(r + 1) * tb, :]
        return jnp.dot(m, la_hi, preferred_element_type=F32) + jnp.dot(m, la_lo, preferred_element_type=F32)

    b = span_sum(0)
    q = q_ref[...] * (GLA_DK ** -0.5)
    k = k_ref[...]
    qe = q * jnp.exp(b)
    ke = k * jnp.exp(span_sum(1))
    a_last = jnp.exp(b[tb - 1:tb, :])
    row = lax.broadcasted_iota(jnp.int32, (tb, GLA_QK), 0)
    q_lvl, k_lvl = [], []
    for idx, s in enumerate(GLA_LEVELS):
        e = jnp.exp(span_sum(2 + idx))
        upper = (row & s) != 0
        q_lvl.append(jnp.where(upper, q * e, 0.0))
        k_lvl.append(jnp.where(upper, 0.0, k * e))
    ri = lax.broadcasted_iota(jnp.int32, (half, 2 * half), 0)
    ci = lax.broadcasted_iota(jnp.int32, (half, 2 * half), 1) % half
    same_block = [ri // (2 * s) == ci // (2 * s) for s in GLA_LEVELS[1:]]
    diagonal = ri == ci
    first_head = lax.broadcasted_iota(jnp.int32, (half, 2 * GLA_DK), 1) < GLA_DK
    gn = gn_ref[...]
    nt = (((1,), (1,)), ((), ()))

    def pair(x, y, r0, c0, lanes):
        keys = y[c0:c0 + half, lanes]
        keys = jnp.concatenate([jnp.where(first_head, keys, 0.0), jnp.where(first_head, 0.0, keys)], axis=0)
        return lax.dot_general(x[r0:r0 + half, lanes].astype(BF16), keys.astype(BF16), nt,
                               preferred_element_type=F32)

    intra = []
    for hp in range(GLA_HEADS // 2):
        lanes = slice(2 * hp * GLA_DK, 2 * (hp + 1) * GLA_DK)
        diag_blocks = []
        for c in (0, half):
            blk = jnp.where(diagonal, pair(q, k, c, c, lanes), 0.0)
            for idx in range(1, len(GLA_LEVELS)):
                blk = blk + jnp.where(same_block[idx - 1], pair(q_lvl[idx], k_lvl[idx], c, c, lanes), 0.0)
            diag_blocks.append(blk)
        low = pair(q_lvl[0], k_lvl[0], half, 0, lanes)
        for a in range(2):
            cols = slice(a * half, (a + 1) * half)
            intra.append(jnp.concatenate(
                [jnp.concatenate([diag_blocks[0][:, cols], jnp.zeros((half, half), F32)], axis=1),
                 jnp.concatenate([low[:, cols], diag_blocks[1][:, cols]], axis=1)], axis=0).astype(BF16))
    for h in range(GLA_HEADS):
        ks = slice(h * GLA_DK, (h + 1) * GLA_DK)
        vs = slice(h * GLA_DV, (h + 1) * GLA_DV)
        attn = intra[h]
        v = v_ref[:, vs].astype(BF16)
        st = st_ref[h]
        o = jnp.dot(attn, v, preferred_element_type=F32)
        o = o + lax.dot_general(qe[:, ks].astype(BF16), st.astype(BF16), nt, preferred_element_type=F32)
        st_ref[h] = st * a_last[:, ks] + lax.dot_general(v, ke[:, ks].astype(BF16), (((0,), (0,)), ((), ())),
                                                         preferred_element_type=F32)
        g = g_ref[:, vs]
        o_ref[:, vs] = _rms(o, gn) * (g * jax.nn.sigmoid(g))


def _cast_spec(shape, steps):
    rows = shape[0]
    hold = next(d for d in range(1, steps + 1) if steps % d == 0 and rows * d % (steps * BF16_SUBLANES) == 0)
    return pl.BlockSpec((rows * hold // steps, shape[1]), lambda i: (i // hold, 0))


def _gla(qg, kg, la, vg, gg, gn, weights):
    t = qg.shape[0]
    tb = GLA_TILE
    steps = t // tb
    span = _gla_span_matrix(tb)
    row = lambda w: pl.BlockSpec((tb, w), lambda i: (i, 0))
    cast_specs = [_cast_spec(w.shape, steps) for w in weights]
    out = pl.pallas_call(
        functools.partial(_gla_kernel, len(weights)),
        grid=(steps,),
        in_specs=[row(GLA_QK), row(GLA_QK), row(GLA_QK), row(GLA_V), row(GLA_V), _const_spec(gn.shape),
                  _const_spec(span.shape)] + cast_specs,
        out_specs=[row(GLA_V)] + cast_specs,
        out_shape=[jax.ShapeDtypeStruct((t, GLA_V), F32)] + [jax.ShapeDtypeStruct(w.shape, BF16) for w in weights],
        scratch_shapes=[pltpu.VMEM((GLA_HEADS, GLA_DV, GLA_DK), F32)],
        compiler_params=pltpu.CompilerParams(dimension_semantics=("arbitrary",), vmem_limit_bytes=VMEM_LIMIT),
        name="gla",
    )(qg, kg, la, vg, gg, gn, span, *weights)
    return out[0], out[1:]


def _moba_bias_kernel(tab_ref, o_ref):
    h = pl.program_id(0)
    blk = MOBA_BLOCK
    max_exact = REL_BUCKETS // 2
    x = lax.broadcasted_iota(jnp.int32, (8, 2 * blk), 1)
    table = jnp.broadcast_to(tab_ref[pl.ds(h, 1), :], (8, LANES))
    for delta in range(BIAS_TILES):
        dist = delta * blk + x - blk
        n = jnp.maximum(dist, 0)
        nf = jnp.maximum(n, 1).astype(F32)
        large = max_exact + (jnp.log(nf / max_exact) / math.log(REL_MAX_DIST / max_exact)
                             * (REL_BUCKETS - max_exact)).astype(jnp.int32)
        large = jnp.minimum(large, REL_BUCKETS - 1)
        bucket = jnp.where(n < max_exact, n, large)
        bias = jnp.concatenate([jnp.take_along_axis(table, bucket[:, c:c + LANES], axis=1)
                                for c in range(0, 2 * blk, LANES)], axis=1)
        by_dist = jnp.where(dist >= 0, bias * LOG2E, NEG)
        rows = jnp.broadcast_to(by_dist[0:1], (blk, 2 * blk))
        o_ref[0, delta] = pltpu.roll(rows, 0, 1, stride=1, stride_axis=0)[:, blk:].astype(o_ref.dtype)


def _moba_bias(rel_bias):
    blk = MOBA_BLOCK
    table = jnp.pad(rel_bias.T, ((0, 0), (0, LANES - REL_BUCKETS)))
    return pl.pallas_call(
        _moba_bias_kernel,
        grid=(MOBA_HEADS,),
        in_specs=[_const_spec(table.shape)],
        out_specs=pl.BlockSpec((1, BIAS_TILES, blk, blk), lambda h: (h, 0, 0, 0)),
        out_shape=jax.ShapeDtypeStruct((MOBA_HEADS, BIAS_TILES, blk, blk), BF16),
        compiler_params=pltpu.CompilerParams(dimension_semantics=("parallel",)),
        name="moba_bias",
    )(table)


def _moba_attn_kernel(qi_ref, gi_ref, qt_ref, ka_ref, vt_ref, bias_ref, o_ref, s0_ref, s1_ref, p0_ref, p1_ref,
                      acc_ref):
    n_pairs = qi_ref.shape[0] - 2
    blk = MOBA_BLOCK
    s_refs = (s0_ref, s1_ref)
    p_refs = (p0_ref, p1_ref)

    def scores(t, slot):
        s_ref = s_refs[slot]
        qi = qi_ref[t]
        gi = gi_ref[t]
        qt = qt_ref[0, qi]
        col_max = None
        for b in range(KV_GROUP):
            j = gi * KV_GROUP + b
            s = jnp.dot(ka_ref[0, j], qt, preferred_element_type=F32).astype(BF16)
            s = s + bias_ref[0, jnp.clip(qi - j, 0, BIAS_TILES - 1)]
            s_ref[b] = s
            cm = jnp.max(s, axis=0, keepdims=True)
            col_max = cm if col_max is None else jnp.maximum(col_max, cm)
        return col_max.astype(F32)

    def softmax(t, slot, m_run, col_max):
        m_prev = jnp.where(gi_ref[t] == 0, -jnp.inf, m_run)
        m = jnp.maximum(m_prev, col_max)
        for b in range(KV_GROUP):
            p_refs[slot][b * blk:(b + 1) * blk, :] = jnp.exp2(s_refs[slot][b] - m.astype(BF16))
        return m, jnp.exp2(m_prev - m)

    def values(t, slot, alpha):
        acc = acc_ref[...] * alpha + jnp.dot(vt_ref[0, gi_ref[t]], p_refs[slot][...], preferred_element_type=F32)
        acc_ref[...] = acc
        o_ref[0, qi_ref[t]] = (acc[0:MOBA_HD] / acc[MOBA_HD:MOBA_HD + 1]).astype(o_ref.dtype)

    acc_ref[...] = jnp.zeros_like(acc_ref)
    cm0 = scores(0, 0)
    cm1 = scores(1, 1)
    m0, alpha0 = softmax(0, 0, jnp.full((1, blk), -jnp.inf, F32), cm0)

    unroll = max(u for u in range(2, ATTN_UNROLL + 1, 2) if n_pairs % u == 0)

    def step(u, carry):
        m, alpha, cm = carry
        for i in range(unroll):
            t = unroll * u + i
            cm_next = scores(t + 2, i % 2)
            m, alpha_next = softmax(t + 1, 1 - i % 2, m, cm)
            values(t, i % 2, alpha)
            alpha, cm = alpha_next, cm_next
        return m, alpha, cm

    lax.fori_loop(0, n_pairs // unroll, step, (m0, alpha0, cm1))


def _moba_attn(qt, ka, vt, bias):
    nb = qt.shape[1]
    blk = MOBA_BLOCK
    pairs = [(qi, g) for qi in range(nb) for g in range(qi // KV_GROUP + 1)] + [(0, 0), (0, 0)]
    qi_tab = jnp.asarray([a for a, _ in pairs], jnp.int32)
    gi_tab = jnp.asarray([b for _, b in pairs], jnp.int32)
    head = lambda shape: pl.BlockSpec((1,) + shape, lambda h, *_: (h,) + (0,) * len(shape))
    return pl.pallas_call(
        _moba_attn_kernel,
        grid_spec=pltpu.PrefetchScalarGridSpec(
            num_scalar_prefetch=2,
            grid=(MOBA_HEADS,),
            in_specs=[head((nb, LANES, blk)), head((nb, blk, LANES)),
                      head((nb // KV_GROUP, V_ROWS, KV_GROUP * blk)), head((BIAS_TILES, blk, blk))],
            out_specs=head((nb, MOBA_HD, blk)),
            scratch_shapes=[pltpu.VMEM((KV_GROUP, blk, blk), BF16), pltpu.VMEM((KV_GROUP, blk, blk), BF16),
                            pltpu.VMEM((KV_GROUP * blk, blk), BF16), pltpu.VMEM((KV_GROUP * blk, blk), BF16),
                            pltpu.VMEM((V_ROWS, blk), F32)]),
        out_shape=jax.ShapeDtypeStruct((MOBA_HEADS, nb, MOBA_HD, blk), BF16),
        compiler_params=pltpu.CompilerParams(dimension_semantics=("parallel",), vmem_limit_bytes=VMEM_LIMIT),
        name="moba_attn",
    )(qi_tab, gi_tab, qt, ka, vt, bias)


def _out_ffn_kernel(final, h_ref, og_ref, om_ref, p_ref, wo_ref, n3_ref, wgu_ref, wd_ref,
                    n4_ref, wpg_ref, wpp_ref, nf_ref, o_ref):
    mix = jnp.dot(og_ref[...].astype(BF16), wo_ref[0:GLA_V, :], preferred_element_type=F32)
    om = [om_ref[:, b].reshape(MOBA_W, MOBA_BLOCK).astype(F32).T.astype(BF16) for b in range(om_ref.shape[1])]
    mix = mix + jnp.dot(jnp.concatenate(om, axis=0), wo_ref[GLA_V:, :], preferred_element_type=F32)
    h = h_ref[...] + mix
    h = h + 0.5 * _swiglu(_rms(h, n3_ref[...]).astype(BF16), wgu_ref, wd_ref)
    gate = jax.nn.sigmoid(jnp.dot(_rms(h, n4_ref[...]).astype(BF16), wpg_ref[...], preferred_element_type=F32))
    h = h + gate * jnp.dot(p_ref[...].astype(BF16), wpp_ref[...], preferred_element_type=F32)
    o_ref[...] = _rms(h, nf_ref[...]) if final else h


def _out_ffn(final, h, og, om, p, wo, n3, wgu, wd, n4, wpg, wpp, nf):
    t, d = h.shape
    tm = min(ROW_TILE, t)
    row = lambda w: pl.BlockSpec((tm, w), lambda i: (i, 0))
    consts = (wo, n3, wgu, wd, n4, wpg, wpp, nf)
    return pl.pallas_call(
        functools.partial(_out_ffn_kernel, final),
        grid=(t // tm,),
        in_specs=[row(d), row(GLA_V),
                  pl.BlockSpec((MOBA_HEADS, tm // MOBA_BLOCK, MOBA_HD, MOBA_BLOCK), lambda i: (0, i, 0, 0)),
                  row(p.shape[1])]
        + [_const_spec(a.shape) for a in consts],
        out_specs=row(d),
        out_shape=jax.ShapeDtypeStruct((t, d), F32),
        compiler_params=pltpu.CompilerParams(dimension_semantics=("parallel",), vmem_limit_bytes=VMEM_LIMIT),
        name="out_ffn",
    )(h, og, om, p, *consts)


def kernel(x, p, ffn1_norm, ffn1_w_gu, ffn1_w_down, mix_norm, w_in, w_gk_up, b_gk, gla_out_norm, w_out,
           ffn2_norm, ffn2_w_gu, ffn2_w_down, ple_norm, w_ple_gate, w_ple_proj, rel_bias, final_norm):
    batch, t, d = x.shape
    depth = p.shape[0]
    assert batch == 1 and t % ROW_TILE == 0 and ROW_TILE % (KV_GROUP * MOBA_BLOCK) == 0
    row = lambda a: a.reshape(1, -1)
    bias = _moba_bias(rel_bias)
    lr0 = 2 * GLA_QK + 2 * GLA_V
    h = x[0]
    for i in range(depth):
        wi = w_in[i]
        wa = wi[:, :lr0].astype(BF16)
        wlr = jnp.pad(wi[:, lr0:lr0 + GLA_LOWRANK], ((0, 0), (0, LANES - GLA_LOWRANK))).astype(BF16)
        wm = wi[:, lr0 + GLA_LOWRANK:].astype(BF16)
        wgk = jnp.pad(w_gk_up[i], ((0, LANES - GLA_LOWRANK), (0, 0))).astype(BF16)
        h1, qg, kg, vg, gg, la, qt, ka, vt = _ffn_in(
            h, row(ffn1_norm[i]), ffn1_w_gu[i].astype(BF16), ffn1_w_down[i].astype(BF16), row(mix_norm[i]),
            wa, wlr, wm, wgk, row(b_gk[i]))
        og, (wo, wgu2, wd2, wpg) = _gla(qg, kg, la, vg, gg, row(gla_out_norm[i]),
                                        [w_out[i], ffn2_w_gu[i], ffn2_w_down[i], w_ple_gate[i]])
        om = _moba_attn(qt, ka, vt, bias)
        h = _out_ffn(i == depth - 1, h1, og, om, p[i, 0], wo, row(ffn2_norm[i]), wgu2, wd2, row(ple_norm[i]),
                     wpg, w_ple_proj[i].astype(BF16), row(final_norm))
    return h[None]
```

```python
import functools
import math

import jax
import jax.numpy as jnp
import numpy as np
from jax import lax
from jax.experimental import pallas as pl
from jax.experimental.pallas import tpu as pltpu

F32 = jnp.float32
BF16 = jnp.bfloat16

GLA_HEADS = 4
GLA_DK = 64
GLA_DV = 128
GLA_LOWRANK = 16
GLA_TAU = 16.0
MOBA_HEADS = 8
MOBA_HD = 64
MOBA_BLOCK = 256
MOBA_TOPK = 3
REL_BUCKETS = 32
REL_MAX_DIST = 4096
EPS = 1e-6
NEG = -1e30

GLA_QK = GLA_HEADS * GLA_DK
GLA_V = GLA_HEADS * GLA_DV
GLA_COLS = 2 * GLA_QK + 2 * GLA_V
MOBA_W = MOBA_HEADS * MOBA_HD

LANES = 128
BF16_SUBLANES = 16
VMEM_LIMIT = 56 * 1024 * 1024

ROW_TILE = 512
GLA_TILE = 256
GLA_LEVELS = tuple(GLA_TILE >> (i + 1) for i in range(GLA_TILE.bit_length() - 1))
_FAR_DIST = math.ceil((REL_BUCKETS // 2) * (REL_MAX_DIST / (REL_BUCKETS // 2))
                      ** ((REL_BUCKETS // 2 - 1) / (REL_BUCKETS // 2)))
_FAR_BLOCKS = -(-(_FAR_DIST - 1) // MOBA_BLOCK) + 1
BIAS_TILES = _FAR_BLOCKS + 1
KV_GROUP = 2
ATTN_UNROLL = 32
V_ROWS = 80
LOG2E = 1.4426950408889634


def _rms(x, g):
    return x * lax.rsqrt(jnp.mean(x * x, axis=-1, keepdims=True) + EPS) * g


def _ff_chunks(d_ff):
    step = 1024
    return [(s, min(step, d_ff - s)) for s in range(0, d_ff, step)]


def _swiglu(xn, wgu_ref, wd_ref):
    d_ff = wd_ref.shape[0]
    acc = None
    for start, size in _ff_chunks(d_ff):
        gate = jnp.dot(xn, wgu_ref[:, start:start + size], preferred_element_type=F32)
        up = jnp.dot(xn, wgu_ref[:, d_ff + start:d_ff + start + size], preferred_element_type=F32)
        act = (gate * jax.nn.sigmoid(gate) * up).astype(BF16)
        part = jnp.dot(act, wd_ref[start:start + size, :], preferred_element_type=F32)
        acc = part if acc is None else acc + part
    return acc


def _const_spec(shape):
    return pl.BlockSpec(shape, lambda *_: (0,) * len(shape))


def _moba_operands(pair, q, k, v, first_block, km_ref, qt_ref, ka_ref, vt_ref):
    blk = MOBA_BLOCK
    rows = q.shape[0]
    n_blocks = rows // blk
    hd = MOBA_HD
    means = jnp.mean(k.reshape(n_blocks, blk, LANES), axis=1)
    km = km_ref[pair]
    km_row = lax.broadcasted_iota(jnp.int32, km.shape, 0)
    for b in range(n_blocks):
        km = jnp.where(km_row == first_block + b, means[b:b + 1], km)
    km_ref[pair] = km
    km_lane = lax.broadcasted_iota(jnp.int32, km.shape, 1)
    q_t = q.T
    v_t = v.T
    block = lax.broadcasted_iota(jnp.int32, (hd, rows), 0)
    block_f = block.astype(F32)
    own = first_block + lax.broadcasted_iota(jnp.int32, (hd, rows), 1) // blk
    lane = lax.broadcasted_iota(jnp.int32, (rows, LANES), 1)
    own_k = first_block + lax.broadcasted_iota(jnp.int32, (rows, LANES), 0) // blk
    ones_rows = jnp.where(lax.broadcasted_iota(jnp.int32, (V_ROWS - hd, rows), 0) == 0, 1.0, 0.0)
    for a in range(2):
        head = 2 * pair + a
        gate = jnp.dot(jnp.where(km_lane // hd == a, km, 0.0), q_t, precision=lax.Precision.HIGHEST,
                       preferred_element_type=F32)
        gate = jnp.where(block < own, gate, -jnp.inf)
        chosen = jnp.where(block == own, 1.0, 0.0)
        for _ in range(MOBA_TOPK):
            top = jnp.max(gate, axis=0, keepdims=True)
            first = jnp.min(jnp.where(gate == top, block_f, float(hd)), axis=0, keepdims=True)
            hit = block_f == first
            chosen = jnp.where(hit, 1.0, chosen)
            gate = jnp.where(hit, -jnp.inf, gate)
        q_a = q_t[a * hd:(a + 1) * hd] * (hd ** -0.5 * LOG2E)
        q_aug_t = jnp.concatenate([q_a, chosen - 1.0] if a == 0 else [chosen - 1.0, q_a], axis=0).astype(BF16)
        v_aug_t = jnp.concatenate([v_t[a * hd:(a + 1) * hd], ones_rows], axis=0).astype(BF16)
        k_aug = jnp.where(lane // hd == a, k, jnp.where(lane - (1 - a) * hd == own_k, -NEG, 0.0)).astype(BF16)
        for b in range(n_blocks):
            qt_ref[head, b] = q_aug_t[:, b * blk:(b + 1) * blk]
            ka_ref[head, b] = k_aug[b * blk:(b + 1) * blk, :]
        group = KV_GROUP * blk
        for g in range(n_blocks // KV_GROUP):
            vt_ref[head, g] = v_aug_t[:, g * group:(g + 1) * group]


def _ffn_in_kernel(x_ref, n1_ref, wgu_ref, wd_ref, n2_ref, wa_ref, wlr_ref, wm_ref, wgk_ref, bgk_ref,
                   h_ref, qg_ref, kg_ref, vg_ref, gg_ref, la_ref, qt_ref, ka_ref, vt_ref, km_ref):
    @pl.when(pl.program_id(0) == 0)
    def _():
        km_ref[...] = jnp.zeros_like(km_ref)

    x = x_ref[...]
    h = x + 0.5 * _swiglu(_rms(x, n1_ref[...]).astype(BF16), wgu_ref, wd_ref)
    h_ref[...] = h
    n = _rms(h, n2_ref[...]).astype(BF16)
    ua = jnp.dot(n, wa_ref[...], preferred_element_type=F32)
    qg_ref[...] = ua[:, 0:GLA_QK]
    kg_ref[...] = ua[:, GLA_QK:2 * GLA_QK]
    vg_ref[...] = ua[:, 2 * GLA_QK:2 * GLA_QK + GLA_V]
    gg_ref[...] = ua[:, 2 * GLA_QK + GLA_V:]
    lr = jnp.dot(n, wlr_ref[...], preferred_element_type=F32)
    z = jnp.dot(lr.astype(BF16), wgk_ref[...], preferred_element_type=F32) + bgk_ref[...]
    la_ref[...] = (jnp.minimum(z, 0.0) - jnp.log1p(jnp.exp(-jnp.abs(z)))) * (1.0 / GLA_TAU)
    um = jnp.dot(n, wm_ref[...], preferred_element_type=F32)
    first_block = pl.program_id(0) * (x.shape[0] // MOBA_BLOCK)
    for pair in range(MOBA_HEADS // 2):
        q, k, v = (um[:, c * MOBA_W + pair * LANES:c * MOBA_W + (pair + 1) * LANES] for c in range(3))
        _moba_operands(pair, q, k, v, first_block, km_ref, qt_ref, ka_ref, vt_ref)


def _ffn_in(x, n1, wgu, wd, n2, wa, wlr, wm, wgk, bgk):
    t, d = x.shape
    tm = min(ROW_TILE, t)
    blk = MOBA_BLOCK
    nb = t // blk
    assert nb <= MOBA_HD
    step_blocks = tm // blk
    row = lambda w: pl.BlockSpec((tm, w), lambda i: (i, 0))
    outs = [(d, F32), (GLA_QK, F32), (GLA_QK, F32), (GLA_V, F32), (GLA_V, F32), (GLA_QK, F32)]
    heads = lambda shape: pl.BlockSpec((MOBA_HEADS,) + shape, lambda i: (0, i, 0, 0))
    return pl.pallas_call(
        _ffn_in_kernel,
        grid=(t // tm,),
        in_specs=[row(d)] + [_const_spec(a.shape) for a in (n1, wgu, wd, n2, wa, wlr, wm, wgk, bgk)],
        out_specs=[row(w) for w, _ in outs] + [heads((step_blocks, LANES, blk)), heads((step_blocks, blk, LANES)),
                                               heads((step_blocks // KV_GROUP, V_ROWS, KV_GROUP * blk))],
        out_shape=[jax.ShapeDtypeStruct((t, w), dt) for w, dt in outs]
        + [jax.ShapeDtypeStruct((MOBA_HEADS, nb, LANES, blk), BF16),
           jax.ShapeDtypeStruct((MOBA_HEADS, nb, blk, LANES), BF16),
           jax.ShapeDtypeStruct((MOBA_HEADS, nb // KV_GROUP, V_ROWS, KV_GROUP * blk), BF16)],
        scratch_shapes=[pltpu.VMEM((MOBA_HEADS // 2, MOBA_HD, LANES), F32)],
        compiler_params=pltpu.CompilerParams(dimension_semantics=("arbitrary",), vmem_limit_bytes=VMEM_LIMIT),
        name="ffn_in",
    )(x, n1, wgu, wd, n2, wa, wlr, wm, wgk, bgk)


def _gla_span_matrix(tile):
    i = np.arange(tile)[:, None]
    t = np.arange(tile)[None, :]
    mats = [t <= i, t > i]
    for s in GLA_LEVELS:
        split = (i // (2 * s)) * (2 * s) + s - 1
        mats.append(np.where((i & s) != 0, (t > split) & (t <= i), (t > i) & (t <= split)))
    return jnp.asarray(np.concatenate(mats, axis=0), BF16)


def _gla_kernel(n_cast, q_ref, k_ref, la_ref, v_ref, g_ref, gn_ref, span_ref, *refs):
    o_ref = refs[n_cast]
    st_ref = refs[2 * n_cast + 1]
    for w_ref, w_bf16_ref in zip(refs[:n_cast], refs[n_cast + 1:2 * n_cast + 1]):
        w_bf16_ref[...] = w_ref[...].astype(BF16)

    @pl.when(pl.program_id(0) == 0)
    def _():
        st_ref[...] = jnp.zeros_like(st_ref)

    tb = q_ref.shape[0]
    half = tb // 2
    la = la_ref[...]
    la_hi = la.astype(BF16)
    la_lo = (la - la_hi.astype(F32)).astype(BF16)

    def span_sum(r):
        m = span_ref[r * tb:(r + 1) * tb, :]
        return jnp.dot(m, la_hi, preferred_element_type=F32) + jnp.dot(m, la_lo, preferred_element_type=F32)

    b = span_sum(0)
    q = q_ref[...] * (GLA_DK ** -0.5)
    k = k_ref[...]
    qe = q * jnp.exp(b)
    ke = k * jnp.exp(span_sum(1))
    a_last = jnp.exp(b[tb - 1:tb, :])
    row = lax.broadcasted_iota(jnp.int32, (tb, GLA_QK), 0)
    q_lvl, k_lvl = [], []
    for idx, s in enumerate(GLA_LEVELS):
        e = jnp.exp(span_sum(2 + idx))
        upper = (row & s) != 0
        q_lvl.append(jnp.where(upper, q * e, 0.0))
        k_lvl.append(jnp.where(upper, 0.0, k * e))
    ri = lax.broadcasted_iota(jnp.int32, (half, 2 * half), 0)
    ci = lax.broadcasted_iota(jnp.int32, (half, 2 * half), 1) % half
    same_block = [ri // (2 * s) == ci // (2 * s) for s in GLA_LEVELS[1:]]
    diagonal = ri == ci
    first_head = lax.broadcasted_iota(jnp.int32, (half, 2 * GLA_DK), 1) < GLA_DK
    gn = gn_ref[...]
    nt = (((1,), (1,)), ((), ()))

    def pair(x, y, r0, c0, lanes):
        keys = y[c0:c0 + half, lanes]
        keys = jnp.concatenate([jnp.where(first_head, keys, 0.0), jnp.where(first_head, 0.0, keys)], axis=0)
        return lax.dot_general(x[r0:r0 + half, lanes].astype(BF16), keys.astype(BF16), nt,
                               preferred_element_type=F32)

    intra = []
    for hp in range(GLA_HEADS // 2):
        lanes = slice(2 * hp * GLA_DK, 2 * (hp + 1) * GLA_DK)
        diag_blocks = []
        for c in (0, half):
            blk = jnp.where(diagonal, pair(q, k, c, c, lanes), 0.0)
            for idx in range(1, len(GLA_LEVELS)):
                blk = blk + jnp.where(same_block[idx - 1], pair(q_lvl[idx], k_lvl[idx], c, c, lanes), 0.0)
            diag_blocks.append(blk)
        low = pair(q_lvl[0], k_lvl[0], half, 0, lanes)
        for a in range(2):
            cols = slice(a * half, (a + 1) * half)
            intra.append(jnp.concatenate(
                [jnp.concatenate([diag_blocks[0][:, cols], jnp.zeros((half, half), F32)], axis=1),
                 jnp.concatenate([low[:, cols], diag_blocks[1][:, cols]], axis=1)], axis=0).astype(BF16))
    for h in range(GLA_HEADS):
        ks = slice(h * GLA_DK, (h + 1) * GLA_DK)
        vs = slice(h * GLA_DV, (h + 1) * GLA_DV)
        attn = intra[h]
        v = v_ref[:, vs].astype(BF16)
        st = st_ref[h]
        o = jnp.dot(attn, v, preferred_element_type=F32)
        o = o + lax.dot_general(qe[:, ks].astype(BF16), st.astype(BF16), nt, preferred_element_type=F32)
        st_ref[h] = st * a_last[:, ks] + lax.dot_general(v, ke[:, ks].astype(BF16), (((0,), (0,)), ((), ())),
                                                         preferred_element_type=F32)
        g = g_ref[:, vs]
        o_ref[:, vs] = _rms(o, gn) * (g * jax.nn.sigmoid(g))


def _cast_spec(shape, steps):
    rows = shape[0]
    hold = next(d for d in range(1, steps + 1) if steps % d == 0 and rows * d % (steps * BF16_SUBLANES) == 0)
    return pl.BlockSpec((rows * hold // steps, shape[1]), lambda i: (i // hold, 0))


def _gla(qg, kg, la, vg, gg, gn, weights):
    t = qg.shape[0]
    tb = GLA_TILE
    steps = t // tb
    span = _gla_span_matrix(tb)
    row = lambda w: pl.BlockSpec((tb, w), lambda i: (i, 0))
    cast_specs = [_cast_spec(w.shape, steps) for w in weights]
    out = pl.pallas_call(
        functools.partial(_gla_kernel, len(weights)),
        grid=(steps,),
        in_specs=[row(GLA_QK), row(GLA_QK), row(GLA_QK), row(GLA_V), row(GLA_V), _const_spec(gn.shape),
                  _const_spec(span.shape)] + cast_specs,
        out_specs=[row(GLA_V)] + cast_specs,
        out_shape=[jax.ShapeDtypeStruct((t, GLA_V), F32)] + [jax.ShapeDtypeStruct(w.shape, BF16) for w in weights],
        scratch_shapes=[pltpu.VMEM((GLA_HEADS, GLA_DV, GLA_DK), F32)],
        compiler_params=pltpu.CompilerParams(dimension_semantics=("arbitrary",), vmem_limit_bytes=VMEM_LIMIT),
        name="gla",
    )(qg, kg, la, vg, gg, gn, span, *weights)
    return out[0], out[1:]


def _prologue_kernel(tab_ref, wgu_ref, wd_ref, win_ref, o_ref, wgu_out, wd_out, wa_out, wlr_out, wm_out):
    wgu_out[...] = wgu_ref[...].astype(BF16)
    wd_out[...] = wd_ref[...].astype(BF16)
    w_in = win_ref[...]
    wa_out[...] = w_in[:, :GLA_COLS].astype(BF16)
    wlr_out[...] = w_in[:, GLA_COLS:GLA_COLS + LANES].astype(BF16)
    wm_out[...] = w_in[:, GLA_COLS + GLA_LOWRANK:].astype(BF16)

    h = pl.program_id(0)
    blk = MOBA_BLOCK
    max_exact = REL_BUCKETS // 2
    x = lax.broadcasted_iota(jnp.int32, (8, 2 * blk), 1)
    table = jnp.broadcast_to(tab_ref[pl.ds(h, 1), :], (8, LANES))
    for delta in range(BIAS_TILES):
        dist = delta * blk + x - blk
        n = jnp.maximum(dist, 0)
        nf = jnp.maximum(n, 1).astype(F32)
        large = max_exact + (jnp.log(nf / max_exact) / math.log(REL_MAX_DIST / max_exact)
                             * (REL_BUCKETS - max_exact)).astype(jnp.int32)
        large = jnp.minimum(large, REL_BUCKETS - 1)
        bucket = jnp.where(n < max_exact, n, large)
        bias = jnp.concatenate([jnp.take_along_axis(table, bucket[:, c:c + LANES], axis=1)
                                for c in range(0, 2 * blk, LANES)], axis=1)
        by_dist = jnp.where(dist >= 0, bias * LOG2E, NEG)
        rows = jnp.broadcast_to(by_dist[0:1], (blk, 2 * blk))
        o_ref[0, delta] = pltpu.roll(rows, 0, 1, stride=1, stride_axis=0)[:, blk:].astype(o_ref.dtype)


def _prologue(rel_bias, wgu, wd, w_in):
    blk = MOBA_BLOCK
    steps = MOBA_HEADS
    table = jnp.pad(rel_bias.T, ((0, 0), (0, LANES - REL_BUCKETS)))
    d, width = w_in.shape
    moba_cols = width - GLA_COLS - GLA_LOWRANK
    slab = _cast_spec(w_in.shape, steps).block_shape[0]
    assert slab * steps == d
    w_slab = lambda cols: pl.BlockSpec((slab, cols), lambda i: (i, 0))
    return pl.pallas_call(
        _prologue_kernel,
        grid=(steps,),
        in_specs=[_const_spec(table.shape), _cast_spec(wgu.shape, steps), _cast_spec(wd.shape, steps), w_slab(width)],
        out_specs=[pl.BlockSpec((1, BIAS_TILES, blk, blk), lambda h: (h, 0, 0, 0)), _cast_spec(wgu.shape, steps),
                   _cast_spec(wd.shape, steps), w_slab(GLA_COLS), w_slab(LANES), w_slab(moba_cols)],
        out_shape=[jax.ShapeDtypeStruct((MOBA_HEADS, BIAS_TILES, blk, blk), BF16),
                   jax.ShapeDtypeStruct(wgu.shape, BF16), jax.ShapeDtypeStruct(wd.shape, BF16),
                   jax.ShapeDtypeStruct((d, GLA_COLS), BF16), jax.ShapeDtypeStruct((d, LANES), BF16),
                   jax.ShapeDtypeStruct((d, moba_cols), BF16)],
        compiler_params=pltpu.CompilerParams(dimension_semantics=("parallel",), vmem_limit_bytes=VMEM_LIMIT),
        name="prologue",
    )(table, wgu, wd, w_in)


def _moba_attn_kernel(qi_ref, gi_ref, qt_ref, ka_ref, vt_ref, bias_ref, o_ref, s0_ref, s1_ref, p0_ref, p1_ref,
                      acc_ref):
    n_pairs = qi_ref.shape[0] - 2
    blk = MOBA_BLOCK
    s_refs = (s0_ref, s1_ref)
    p_refs = (p0_ref, p1_ref)

    def scores(t, slot):
        s_ref = s_refs[slot]
        qi = qi_ref[t]
        gi = gi_ref[t]
        qt = qt_ref[0, qi]
        col_max = None
        for b in range(KV_GROUP):
            j = gi * KV_GROUP + b
            s = jnp.dot(ka_ref[0, j], qt, preferred_element_type=F32).astype(BF16)
            s = s + bias_ref[0, jnp.clip(qi - j, 0, BIAS_TILES - 1)]
            s_ref[b] = s
            cm = jnp.max(s, axis=0, keepdims=True)
            col_max = cm if col_max is None else jnp.maximum(col_max, cm)
        return col_max.astype(F32)

    def softmax(t, slot, m_run, col_max):
        m_prev = jnp.where(gi_ref[t] == 0, -jnp.inf, m_run)
        m = jnp.maximum(m_prev, col_max)
        for b in range(KV_GROUP):
            p_refs[slot][b * blk:(b + 1) * blk, :] = jnp.exp2(s_refs[slot][b] - m.astype(BF16))
        return m, jnp.exp2(m_prev - m)

    def values(t, slot, alpha):
        acc = acc_ref[...] * alpha + jnp.dot(vt_ref[0, gi_ref[t]], p_refs[slot][...], preferred_element_type=F32)
        acc_ref[...] = acc
        o_ref[0, qi_ref[t]] = (acc[0:MOBA_HD] / acc[MOBA_HD:MOBA_HD + 1]).astype(o_ref.dtype)

    acc_ref[...] = jnp.zeros_like(acc_ref)
    cm0 = scores(0, 0)
    cm1 = scores(1, 1)
    m0, alpha0 = softmax(0, 0, jnp.full((1, blk), -jnp.inf, F32), cm0)

    unroll = max(u for u in range(2, ATTN_UNROLL + 1, 2) if n_pairs % u == 0)

    def step(u, carry):
        m, alpha, cm = carry
        for i in range(unroll):
            t = unroll * u + i
            cm_next = scores(t + 2, i % 2)
            m, alpha_next = softmax(t + 1, 1 - i % 2, m, cm)
            values(t, i % 2, alpha)
            alpha, cm = alpha_next, cm_next
        return m, alpha, cm

    lax.fori_loop(0, n_pairs // unroll, step, (m0, alpha0, cm1))


def _moba_attn(qt, ka, vt, bias):
    nb = qt.shape[1]
    blk = MOBA_BLOCK
    pairs = [(qi, g) for qi in range(nb) for g in range(qi // KV_GROUP + 1)] + [(0, 0), (0, 0)]
    qi_tab = jnp.asarray([a for a, _ in pairs], jnp.int32)
    gi_tab = jnp.asarray([b for _, b in pairs], jnp.int32)
    head = lambda shape: pl.BlockSpec((1,) + shape, lambda h, *_: (h,) + (0,) * len(shape))
    return pl.pallas_call(
        _moba_attn_kernel,
        grid_spec=pltpu.PrefetchScalarGridSpec(
            num_scalar_prefetch=2,
            grid=(MOBA_HEADS,),
            in_specs=[head((nb, LANES, blk)), head((nb, blk, LANES)),
                      head((nb // KV_GROUP, V_ROWS, KV_GROUP * blk)), head((BIAS_TILES, blk, blk))],
            out_specs=head((nb, MOBA_HD, blk)),
            scratch_shapes=[pltpu.VMEM((KV_GROUP, blk, blk), BF16), pltpu.VMEM((KV_GROUP, blk, blk), BF16),
                            pltpu.VMEM((KV_GROUP * blk, blk), BF16), pltpu.VMEM((KV_GROUP * blk, blk), BF16),
                            pltpu.VMEM((V_ROWS, blk), F32)]),
        out_shape=jax.ShapeDtypeStruct((MOBA_HEADS, nb, MOBA_HD, blk), BF16),
        compiler_params=pltpu.CompilerParams(dimension_semantics=("parallel",), vmem_limit_bytes=VMEM_LIMIT),
        name="moba_attn",
    )(qi_tab, gi_tab, qt, ka, vt, bias)


def _out_ffn_kernel(final, h_ref, og_ref, om_ref, p_ref, wo_ref, n3_ref, wgu_ref, wd_ref,
                    n4_ref, wpg_ref, wpp_ref, nf_ref, o_ref):
    mix = jnp.dot(og_ref[...].astype(BF16), wo_ref[0:GLA_V, :], preferred_element_type=F32)
    om = [om_ref[:, b].reshape(MOBA_W, MOBA_BLOCK).astype(F32).T.astype(BF16) for b in range(om_ref.shape[1])]
    mix = mix + jnp.dot(jnp.concatenate(om, axis=0), wo_ref[GLA_V:, :], preferred_element_type=F32)
    h = h_ref[...] + mix
    h = h + 0.5 * _swiglu(_rms(h, n3_ref[...]).astype(BF16), wgu_ref, wd_ref)
    gate = jax.nn.sigmoid(jnp.dot(_rms(h, n4_ref[...]).astype(BF16), wpg_ref[...], preferred_element_type=F32))
    h = h + gate * jnp.dot(p_ref[...].astype(BF16), wpp_ref[...], preferred_element_type=F32)
    o_ref[...] = _rms(h, nf_ref[...]) if final else h


def _out_ffn(final, h, og, om, p, wo, n3, wgu, wd, n4, wpg, wpp, nf):
    t, d = h.shape
    tm = min(ROW_TILE, t)
    row = lambda w: pl.BlockSpec((tm, w), lambda i: (i, 0))
    consts = (wo, n3, wgu, wd, n4, wpg, wpp, nf)
    return pl.pallas_call(
        functools.partial(_out_ffn_kernel, final),
        grid=(t // tm,),
        in_specs=[row(d), row(GLA_V),
                  pl.BlockSpec((MOBA_HEADS, tm // MOBA_BLOCK, MOBA_HD, MOBA_BLOCK), lambda i: (0, i, 0, 0)),
                  row(p.shape[1])]
        + [_const_spec(a.shape) for a in consts],
        out_specs=row(d),
        out_shape=jax.ShapeDtypeStruct((t, d), F32),
        compiler_params=pltpu.CompilerParams(dimension_semantics=("parallel",), vmem_limit_bytes=VMEM_LIMIT),
        name="out_ffn",
    )(h, og, om, p, *consts)


def kernel(x, p, ffn1_norm, ffn1_w_gu, ffn1_w_down, mix_norm, w_in, w_gk_up, b_gk, gla_out_norm, w_out,
           ffn2_norm, ffn2_w_gu, ffn2_w_down, ple_norm, w_ple_gate, w_ple_proj, rel_bias, final_norm):
    batch, t, d = x.shape
    depth = p.shape[0]
    assert batch == 1 and t % ROW_TILE == 0 and ROW_TILE % (KV_GROUP * MOBA_BLOCK) == 0
    row = lambda a: a.reshape(1, -1)
    h = x[0]
    for i in range(depth):
        bias, wgu1, wd1, wa, wlr, wm = _prologue(rel_bias, ffn1_w_gu[i], ffn1_w_down[i], w_in[i])
        wgk = jnp.pad(w_gk_up[i], ((0, LANES - GLA_LOWRANK), (0, 0))).astype(BF16)
        h1, qg, kg, vg, gg, la, qt, ka, vt = _ffn_in(
            h, row(ffn1_norm[i]), wgu1, wd1, row(mix_norm[i]), wa, wlr, wm, wgk, row(b_gk[i]))
        og, (wo, wgu2, wd2, wpg) = _gla(qg, kg, la, vg, gg, row(gla_out_norm[i]),
                                        [w_out[i], ffn2_w_gu[i], ffn2_w_down[i], w_ple_gate[i]])
        om = _moba_attn(qt, ka, vt, bias)
        h = _out_ffn(i == depth - 1, h1, og, om, p[i, 0], wo, row(ffn2_norm[i]), wgu2, wd2, row(ple_norm[i]),
                     wpg, w_ple_proj[i].astype(BF16), row(final_norm))
    return h[None]
```

```python
import functools
import math

import jax
import jax.numpy as jnp
import numpy as np
from jax import lax
from jax.experimental import pallas as pl
from jax.experimental.pallas import tpu as pltpu

F32 = jnp.float32
BF16 = jnp.bfloat16

GLA_HEADS = 4
GLA_DK = 64
GLA_DV = 128
GLA_LOWRANK = 16
GLA_TAU = 16.0
MOBA_HEADS = 8
MOBA_HD = 64
MOBA_BLOCK = 256
MOBA_TOPK = 3
REL_BUCKETS = 32
REL_MAX_DIST = 4096
EPS = 1e-6
NEG = -1e30

GLA_QK = GLA_HEADS * GLA_DK
GLA_V = GLA_HEADS * GLA_DV
GLA_COLS = 2 * GLA_QK + 2 * GLA_V
MOBA_W = MOBA_HEADS * MOBA_HD

LANES = 128
BF16_SUBLANES = 16
VMEM_LIMIT = 56 * 1024 * 1024

ROW_TILE = 512
GLA_TILE = 256
GLA_LEVELS = tuple(GLA_TILE >> (i + 1) for i in range(GLA_TILE.bit_length() - 1))
_FAR_DIST = math.ceil((REL_BUCKETS // 2) * (REL_MAX_DIST / (REL_BUCKETS // 2))
                      ** ((REL_BUCKETS // 2 - 1) / (REL_BUCKETS // 2)))
_FAR_BLOCKS = -(-(_FAR_DIST - 1) // MOBA_BLOCK) + 1
BIAS_TILES = _FAR_BLOCKS + 1
KV_GROUP = 2
ATTN_UNROLL = 32
V_ROWS = 80
LOG2E = 1.4426950408889634


def _rms(x, g):
    return x * lax.rsqrt(jnp.mean(x * x, axis=-1, keepdims=True) + EPS) * g


def _ff_chunks(d_ff):
    step = 1024
    return [(s, min(step, d_ff - s)) for s in range(0, d_ff, step)]


def _swiglu(xn, wgu_ref, wd_ref):
    d_ff = wd_ref.shape[0]
    acc = None
    for start, size in _ff_chunks(d_ff):
        gate = jnp.dot(xn, wgu_ref[:, start:start + size], preferred_element_type=F32)
        up = jnp.dot(xn, wgu_ref[:, d_ff + start:d_ff + start + size], preferred_element_type=F32)
        act = (gate * jax.nn.sigmoid(gate) * up).astype(BF16)
        part = jnp.dot(act, wd_ref[start:start + size, :], preferred_element_type=F32)
        acc = part if acc is None else acc + part
    return acc


def _const_spec(shape):
    return pl.BlockSpec(shape, lambda *_: (0,) * len(shape))


def _moba_operands(pair, q, k, v, first_block, km_ref, qt_ref, ka_ref, vt_ref):
    blk = MOBA_BLOCK
    rows = q.shape[0]
    n_blocks = rows // blk
    hd = MOBA_HD
    means = jnp.mean(k.reshape(n_blocks, blk, LANES), axis=1)
    km = km_ref[pair]
    km_row = lax.broadcasted_iota(jnp.int32, km.shape, 0)
    for b in range(n_blocks):
        km = jnp.where(km_row == first_block + b, means[b:b + 1], km)
    km_ref[pair] = km
    km_lane = lax.broadcasted_iota(jnp.int32, km.shape, 1)
    q_t = q.T
    v_t = v.T
    block = lax.broadcasted_iota(jnp.int32, (hd, rows), 0)
    block_f = block.astype(F32)
    own = first_block + lax.broadcasted_iota(jnp.int32, (hd, rows), 1) // blk
    lane = lax.broadcasted_iota(jnp.int32, (rows, LANES), 1)
    own_k = first_block + lax.broadcasted_iota(jnp.int32, (rows, LANES), 0) // blk
    ones_rows = jnp.where(lax.broadcasted_iota(jnp.int32, (V_ROWS - hd, rows), 0) == 0, 1.0, 0.0)
    for a in range(2):
        head = 2 * pair + a
        gate = jnp.dot(jnp.where(km_lane // hd == a, km, 0.0), q_t, precision=lax.Precision.HIGHEST,
                       preferred_element_type=F32)
        gate = jnp.where(block < own, gate, -jnp.inf)
        chosen = jnp.where(block == own, 1.0, 0.0)
        for _ in range(MOBA_TOPK):
            top = jnp.max(gate, axis=0, keepdims=True)
            first = jnp.min(jnp.where(gate == top, block_f, float(hd)), axis=0, keepdims=True)
            hit = block_f == first
            chosen = jnp.where(hit, 1.0, chosen)
            gate = jnp.where(hit, -jnp.inf, gate)
        q_a = q_t[a * hd:(a + 1) * hd] * (hd ** -0.5 * LOG2E)
        q_aug_t = jnp.concatenate([q_a, chosen - 1.0] if a == 0 else [chosen - 1.0, q_a], axis=0).astype(BF16)
        v_aug_t = jnp.concatenate([v_t[a * hd:(a + 1) * hd], ones_rows], axis=0).astype(BF16)
        k_aug = jnp.where(lane // hd == a, k, jnp.where(lane - (1 - a) * hd == own_k, -NEG, 0.0)).astype(BF16)
        for b in range(n_blocks):
            qt_ref[head, b] = q_aug_t[:, b * blk:(b + 1) * blk]
            ka_ref[head, b] = k_aug[b * blk:(b + 1) * blk, :]
        group = KV_GROUP * blk
        for g in range(n_blocks // KV_GROUP):
            vt_ref[head, g] = v_aug_t[:, g * group:(g + 1) * group]


def _ffn_in_kernel(x_ref, n1_ref, wgu_ref, wd_ref, n2_ref, wa_ref, wlr_ref, wm_ref, wgk_ref, bgk_ref,
                   h_ref, qg_ref, kg_ref, vg_ref, gg_ref, la_ref, qt_ref, ka_ref, vt_ref, km_ref):
    @pl.when(pl.program_id(0) == 0)
    def _():
        km_ref[...] = jnp.zeros_like(km_ref)

    x = x_ref[...]
    h = x + 0.5 * _swiglu(_rms(x, n1_ref[...]).astype(BF16), wgu_ref, wd_ref)
    h_ref[...] = h
    n = _rms(h, n2_ref[...]).astype(BF16)
    ua = jnp.dot(n, wa_ref[...], preferred_element_type=F32)
    qg_ref[...] = ua[:, 0:GLA_QK]
    kg_ref[...] = ua[:, GLA_QK:2 * GLA_QK]
    vg_ref[...] = ua[:, 2 * GLA_QK:2 * GLA_QK + GLA_V]
    gg_ref[...] = ua[:, 2 * GLA_QK + GLA_V:]
    lr = jnp.dot(n, wlr_ref[...], preferred_element_type=F32)
    z = jnp.dot(lr.astype(BF16), wgk_ref[...], preferred_element_type=F32) + bgk_ref[...]
    la_ref[...] = (jnp.minimum(z, 0.0) - jnp.log1p(jnp.exp(-jnp.abs(z)))) * (1.0 / GLA_TAU)
    um = jnp.dot(n, wm_ref[...], preferred_element_type=F32)
    first_block = pl.program_id(0) * (x.shape[0] // MOBA_BLOCK)
    for pair in range(MOBA_HEADS // 2):
        q, k, v = (um[:, c * MOBA_W + pair * LANES:c * MOBA_W + (pair + 1) * LANES] for c in range(3))
        _moba_operands(pair, q, k, v, first_block, km_ref, qt_ref, ka_ref, vt_ref)


def _ffn_in(x, n1, wgu, wd, n2, wa, wlr, wm, wgk, bgk):
    t, d = x.shape
    tm = min(ROW_TILE, t)
    blk = MOBA_BLOCK
    nb = t // blk
    assert nb <= MOBA_HD
    step_blocks = tm // blk
    row = lambda w: pl.BlockSpec((tm, w), lambda i: (i, 0))
    outs = [(d, F32), (GLA_QK, F32), (GLA_QK, F32), (GLA_V, F32), (GLA_V, F32), (GLA_QK, F32)]
    heads = lambda shape: pl.BlockSpec((MOBA_HEADS,) + shape, lambda i: (0, i, 0, 0))
    return pl.pallas_call(
        _ffn_in_kernel,
        grid=(t // tm,),
        in_specs=[row(d)] + [_const_spec(a.shape) for a in (n1, wgu, wd, n2, wa, wlr, wm, wgk, bgk)],
        out_specs=[row(w) for w, _ in outs] + [heads((step_blocks, LANES, blk)), heads((step_blocks, blk, LANES)),
                                               heads((step_blocks // KV_GROUP, V_ROWS, KV_GROUP * blk))],
        out_shape=[jax.ShapeDtypeStruct((t, w), dt) for w, dt in outs]
        + [jax.ShapeDtypeStruct((MOBA_HEADS, nb, LANES, blk), BF16),
           jax.ShapeDtypeStruct((MOBA_HEADS, nb, blk, LANES), BF16),
           jax.ShapeDtypeStruct((MOBA_HEADS, nb // KV_GROUP, V_ROWS, KV_GROUP * blk), BF16)],
        scratch_shapes=[pltpu.VMEM((MOBA_HEADS // 2, MOBA_HD, LANES), F32)],
        compiler_params=pltpu.CompilerParams(dimension_semantics=("arbitrary",), vmem_limit_bytes=VMEM_LIMIT),
        name="ffn_in",
    )(x, n1, wgu, wd, n2, wa, wlr, wm, wgk, bgk)


def _gla_span_matrix(tile):
    i = np.arange(tile)[:, None]
    t = np.arange(tile)[None, :]
    mats = [t <= i, t > i]
    for s in GLA_LEVELS:
        split = (i // (2 * s)) * (2 * s) + s - 1
        mats.append(np.where((i & s) != 0, (t > split) & (t <= i), (t > i) & (t <= split)))
    return jnp.asarray(np.concatenate(mats, axis=0), BF16)


def _gla_kernel(n_cast, q_ref, k_ref, la_ref, v_ref, g_ref, gn_ref, span_ref, *refs):
    o_ref = refs[n_cast]
    st_ref = refs[2 * n_cast + 1]
    for w_ref, w_bf16_ref in zip(refs[:n_cast], refs[n_cast + 1:2 * n_cast + 1]):
        w_bf16_ref[...] = w_ref[...].astype(BF16)

    @pl.when(pl.program_id(0) == 0)
    def _():
        st_ref[...] = jnp.zeros_like(st_ref)

    tb = q_ref.shape[0]
    half = tb // 2
    la = la_ref[...]
    la_hi = la.astype(BF16)
    la_lo = (la - la_hi.astype(F32)).astype(BF16)

    def span_sum(r):
        m = span_ref[r * tb:(r + 1) * tb, :]
        return jnp.dot(m, la_hi, preferred_element_type=F32) + jnp.dot(m, la_lo, preferred_element_type=F32)

    b = span_sum(0)
    q = q_ref[...] * (GLA_DK ** -0.5)
    k = k_ref[...]
    qe = q * jnp.exp(b)
    ke = k * jnp.exp(span_sum(1))
    a_last = jnp.exp(b[tb - 1:tb, :])
    row = lax.broadcasted_iota(jnp.int32, (tb, GLA_QK), 0)
    q_lvl, k_lvl = [], []
    for idx, s in enumerate(GLA_LEVELS):
        e = jnp.exp(span_sum(2 + idx))
        upper = (row & s) != 0
        q_lvl.append(jnp.where(upper, q * e, 0.0))
        k_lvl.append(jnp.where(upper, 0.0, k * e))
    ri = lax.broadcasted_iota(jnp.int32, (half, 2 * half), 0)
    ci = lax.broadcasted_iota(jnp.int32, (half, 2 * half), 1) % half
    same_block = [ri // (2 * s) == ci // (2 * s) for s in GLA_LEVELS[1:]]
    diagonal = ri == ci
    first_head = lax.broadcasted_iota(jnp.int32, (half, 2 * GLA_DK), 1) < GLA_DK
    gn = gn_ref[...]
    nt = (((1,), (1,)), ((), ()))

    def pair(x, y, r0, c0, lanes):
        keys = y[c0:c0 + half, lanes]
        keys = jnp.concatenate([jnp.where(first_head, keys, 0.0), jnp.where(first_head, 0.0, keys)], axis=0)
        return lax.dot_general(x[r0:r0 + half, lanes].astype(BF16), keys.astype(BF16), nt,
                               preferred_element_type=F32)

    intra = []
    for hp in range(GLA_HEADS // 2):
        lanes = slice(2 * hp * GLA_DK, 2 * (hp + 1) * GLA_DK)
        diag_blocks = []
        for c in (0, half):
            blk = jnp.where(diagonal, pair(q, k, c, c, lanes), 0.0)
            for idx in range(1, len(GLA_LEVELS)):
                blk = blk + jnp.where(same_block[idx - 1], pair(q_lvl[idx], k_lvl[idx], c, c, lanes), 0.0)
            diag_blocks.append(blk)
        low = pair(q_lvl[0], k_lvl[0], half, 0, lanes)
        for a in range(2):
            cols = slice(a * half, (a + 1) * half)
            intra.append(jnp.concatenate(
                [jnp.concatenate([diag_blocks[0][:, cols], jnp.zeros((half, half), F32)], axis=1),
                 jnp.concatenate([low[:, cols], diag_blocks[1][:, cols]], axis=1)], axis=0).astype(BF16))
    for h in range(GLA_HEADS):
        ks = slice(h * GLA_DK, (h + 1) * GLA_DK)
        vs = slice(h * GLA_DV, (h + 1) * GLA_DV)
        attn = intra[h]
        v = v_ref[:, vs].astype(BF16)
        st = st_ref[h]
        o = jnp.dot(attn, v, preferred_element_type=F32)
        o = o + lax.dot_general(qe[:, ks].astype(BF16), st.astype(BF16), nt, preferred_element_type=F32)
        st_ref[h] = st * a_last[:, ks] + lax.dot_general(v, ke[:, ks].astype(BF16), (((0,), (0,)), ((), ())),
                                                         preferred_element_type=F32)
        g = g_ref[:, vs]
        o_ref[:, vs] = _rms(o, gn) * (g * jax.nn.sigmoid(g))


def _cast_spec(shape, steps):
    rows = shape[0]
    hold = next(d for d in range(1, steps + 1) if steps % d == 0 and rows * d % (steps * BF16_SUBLANES) == 0)
    return pl.BlockSpec((rows * hold // steps, shape[1]), lambda i: (i // hold, 0))


def _gla(qg, kg, la, vg, gg, gn, weights):
    t = qg.shape[0]
    tb = GLA_TILE
    steps = t // tb
    span = _gla_span_matrix(tb)
    row = lambda w: pl.BlockSpec((tb, w), lambda i: (i, 0))
    cast_specs = [_cast_spec(w.shape, steps) for w in weights]
    out = pl.pallas_call(
        functools.partial(_gla_kernel, len(weights)),
        grid=(steps,),
        in_specs=[row(GLA_QK), row(GLA_QK), row(GLA_QK), row(GLA_V), row(GLA_V), _const_spec(gn.shape),
                  _const_spec(span.shape)] + cast_specs,
        out_specs=[row(GLA_V)] + cast_specs,
        out_shape=[jax.ShapeDtypeStruct((t, GLA_V), F32)] + [jax.ShapeDtypeStruct(w.shape, BF16) for w in weights],
        scratch_shapes=[pltpu.VMEM((GLA_HEADS, GLA_DV, GLA_DK), F32)],
        compiler_params=pltpu.CompilerParams(dimension_semantics=("arbitrary",), vmem_limit_bytes=VMEM_LIMIT),
        name="gla",
    )(qg, kg, la, vg, gg, gn, span, *weights)
    return out[0], out[1:]


def _prologue_kernel(tab_ref, wgu_ref, wd_ref, win_ref, o_ref, wgu_out, wd_out, wa_out, wlr_out, wm_out):
    wgu_out[...] = wgu_ref[...].astype(BF16)
    wd_out[...] = wd_ref[...].astype(BF16)
    wa_out[...] = win_ref[:GLA_COLS, :].T.astype(BF16)
    wlr_out[...] = win_ref[GLA_COLS:GLA_COLS + LANES, :].T.astype(BF16)
    wm_out[...] = win_ref[GLA_COLS + GLA_LOWRANK:, :].T.astype(BF16)

    h = pl.program_id(0)
    blk = MOBA_BLOCK
    max_exact = REL_BUCKETS // 2
    x = lax.broadcasted_iota(jnp.int32, (8, 2 * blk), 1)
    table = jnp.broadcast_to(tab_ref[pl.ds(h, 1), :], (8, LANES))
    for delta in range(BIAS_TILES):
        dist = delta * blk + x - blk
        n = jnp.maximum(dist, 0)
        nf = jnp.maximum(n, 1).astype(F32)
        large = max_exact + (jnp.log(nf / max_exact) / math.log(REL_MAX_DIST / max_exact)
                             * (REL_BUCKETS - max_exact)).astype(jnp.int32)
        large = jnp.minimum(large, REL_BUCKETS - 1)
        bucket = jnp.where(n < max_exact, n, large)
        bias = jnp.concatenate([jnp.take_along_axis(table, bucket[:, c:c + LANES], axis=1)
                                for c in range(0, 2 * blk, LANES)], axis=1)
        by_dist = jnp.where(dist >= 0, bias * LOG2E, NEG)
        rows = jnp.broadcast_to(by_dist[0:1], (blk, 2 * blk))
        o_ref[0, delta] = pltpu.roll(rows, 0, 1, stride=1, stride_axis=0)[:, blk:].astype(o_ref.dtype)


def _prologue(rel_bias, wgu, wd, w_in):
    blk = MOBA_BLOCK
    steps = MOBA_HEADS
    table = jnp.pad(rel_bias.T, ((0, 0), (0, LANES - REL_BUCKETS)))
    d, width = w_in.shape
    moba_cols = width - GLA_COLS - GLA_LOWRANK
    slab = _cast_spec(w_in.shape, steps).block_shape[0]
    assert slab * steps == d and slab % LANES == 0
    w_slab = lambda cols: pl.BlockSpec((slab, cols), lambda i: (i, 0))
    w_in_t = w_in.T
    return pl.pallas_call(
        _prologue_kernel,
        grid=(steps,),
        in_specs=[_const_spec(table.shape), _cast_spec(wgu.shape, steps), _cast_spec(wd.shape, steps),
                  pl.BlockSpec((width, slab), lambda i: (0, i))],
        out_specs=[pl.BlockSpec((1, BIAS_TILES, blk, blk), lambda h: (h, 0, 0, 0)), _cast_spec(wgu.shape, steps),
                   _cast_spec(wd.shape, steps), w_slab(GLA_COLS), w_slab(LANES), w_slab(moba_cols)],
        out_shape=[jax.ShapeDtypeStruct((MOBA_HEADS, BIAS_TILES, blk, blk), BF16),
                   jax.ShapeDtypeStruct(wgu.shape, BF16), jax.ShapeDtypeStruct(wd.shape, BF16),
                   jax.ShapeDtypeStruct((d, GLA_COLS), BF16), jax.ShapeDtypeStruct((d, LANES), BF16),
                   jax.ShapeDtypeStruct((d, moba_cols), BF16)],
        compiler_params=pltpu.CompilerParams(dimension_semantics=("parallel",), vmem_limit_bytes=VMEM_LIMIT),
        name="prologue",
    )(table, wgu, wd, w_in_t)


def _moba_attn_kernel(qi_ref, gi_ref, qt_ref, ka_ref, vt_ref, bias_ref, o_ref, l_ref, s0_ref, s1_ref, p0_ref,
                      p1_ref, acc_ref):
    n_pairs = qi_ref.shape[0] - 2
    blk = MOBA_BLOCK
    s_refs = (s0_ref, s1_ref)
    p_refs = (p0_ref, p1_ref)

    def scores(t, slot):
        s_ref = s_refs[slot]
        qi = qi_ref[t]
        gi = gi_ref[t]
        qt = qt_ref[0, qi]
        col_max = None
        for b in range(KV_GROUP):
            j = gi * KV_GROUP + b
            s = jnp.dot(ka_ref[0, j], qt, preferred_element_type=F32).astype(BF16)
            s = s + bias_ref[0, jnp.clip(qi - j, 0, BIAS_TILES - 1)]
            s_ref[b] = s
            cm = jnp.max(s, axis=0, keepdims=True)
            col_max = cm if col_max is None else jnp.maximum(col_max, cm)
        return col_max.astype(F32)

    def softmax(t, slot, m_run, col_max):
        m_prev = jnp.where(gi_ref[t] == 0, -jnp.inf, m_run)
        m = jnp.maximum(m_prev, col_max)
        for b in range(KV_GROUP):
            p_refs[slot][b * blk:(b + 1) * blk, :] = jnp.exp2(s_refs[slot][b] - m.astype(BF16))
        return m, jnp.exp2(m_prev - m)

    def values(t, slot, alpha):
        acc = acc_ref[...] * alpha + jnp.dot(vt_ref[0, gi_ref[t]], p_refs[slot][...], preferred_element_type=F32)
        acc_ref[...] = acc
        o_ref[0, qi_ref[t]] = acc[0:MOBA_HD].astype(o_ref.dtype)
        l_ref[0, qi_ref[t]] = acc[MOBA_HD:MOBA_HD + 8]

    acc_ref[...] = jnp.zeros_like(acc_ref)
    cm0 = scores(0, 0)
    cm1 = scores(1, 1)
    m0, alpha0 = softmax(0, 0, jnp.full((1, blk), -jnp.inf, F32), cm0)

    unroll = max(u for u in range(2, ATTN_UNROLL + 1, 2) if n_pairs % u == 0)

    def step(u, carry):
        m, alpha, cm = carry
        for i in range(unroll):
            t = unroll * u + i
            cm_next = scores(t + 2, i % 2)
            m, alpha_next = softmax(t + 1, 1 - i % 2, m, cm)
            values(t, i % 2, alpha)
            alpha, cm = alpha_next, cm_next
        return m, alpha, cm

    lax.fori_loop(0, n_pairs // unroll, step, (m0, alpha0, cm1))


def _moba_attn(qt, ka, vt, bias):
    nb = qt.shape[1]
    blk = MOBA_BLOCK
    pairs = [(qi, g) for qi in range(nb) for g in range(qi // KV_GROUP + 1)] + [(0, 0), (0, 0)]
    qi_tab = jnp.asarray([a for a, _ in pairs], jnp.int32)
    gi_tab = jnp.asarray([b for _, b in pairs], jnp.int32)
    head = lambda shape: pl.BlockSpec((1,) + shape, lambda h, *_: (h,) + (0,) * len(shape))
    return pl.pallas_call(
        _moba_attn_kernel,
        grid_spec=pltpu.PrefetchScalarGridSpec(
            num_scalar_prefetch=2,
            grid=(MOBA_HEADS,),
            in_specs=[head((nb, LANES, blk)), head((nb, blk, LANES)),
                      head((nb // KV_GROUP, V_ROWS, KV_GROUP * blk)), head((BIAS_TILES, blk, blk))],
            out_specs=[head((nb, MOBA_HD, blk)), head((nb, 8, blk))],
            scratch_shapes=[pltpu.VMEM((KV_GROUP, blk, blk), BF16), pltpu.VMEM((KV_GROUP, blk, blk), BF16),
                            pltpu.VMEM((KV_GROUP * blk, blk), BF16), pltpu.VMEM((KV_GROUP * blk, blk), BF16),
                            pltpu.VMEM((V_ROWS, blk), F32)]),
        out_shape=[jax.ShapeDtypeStruct((MOBA_HEADS, nb, MOBA_HD, blk), BF16),
                   jax.ShapeDtypeStruct((MOBA_HEADS, nb, 8, blk), F32)],
        compiler_params=pltpu.CompilerParams(dimension_semantics=("parallel",), vmem_limit_bytes=VMEM_LIMIT),
        name="moba_attn",
    )(qi_tab, gi_tab, qt, ka, vt, bias)


def _out_ffn_kernel(final, h_ref, og_ref, om_ref, ol_ref, p_ref, wo_ref, n3_ref, wgu_ref, wd_ref,
                    n4_ref, wpg_ref, wpp_ref, nf_ref, o_ref):
    mix = jnp.dot(og_ref[...].astype(BF16), wo_ref[0:GLA_V, :], preferred_element_type=F32)
    om = [(om_ref[:, b].astype(F32) / ol_ref[:, b, 0:1, :]).reshape(MOBA_W, MOBA_BLOCK).T.astype(BF16)
          for b in range(om_ref.shape[1])]
    mix = mix + jnp.dot(jnp.concatenate(om, axis=0), wo_ref[GLA_V:, :], preferred_element_type=F32)
    h = h_ref[...] + mix
    h = h + 0.5 * _swiglu(_rms(h, n3_ref[...]).astype(BF16), wgu_ref, wd_ref)
    gate = jax.nn.sigmoid(jnp.dot(_rms(h, n4_ref[...]).astype(BF16), wpg_ref[...], preferred_element_type=F32))
    h = h + gate * jnp.dot(p_ref[...].astype(BF16), wpp_ref[...], preferred_element_type=F32)
    o_ref[...] = _rms(h, nf_ref[...]) if final else h


def _out_ffn(final, h, og, om, ol, p, wo, n3, wgu, wd, n4, wpg, wpp, nf):
    t, d = h.shape
    tm = min(ROW_TILE, t)
    row = lambda w: pl.BlockSpec((tm, w), lambda i: (i, 0))
    consts = (wo, n3, wgu, wd, n4, wpg, wpp, nf)
    return pl.pallas_call(
        functools.partial(_out_ffn_kernel, final),
        grid=(t // tm,),
        in_specs=[row(d), row(GLA_V),
                  pl.BlockSpec((MOBA_HEADS, tm // MOBA_BLOCK, MOBA_HD, MOBA_BLOCK), lambda i: (0, i, 0, 0)),
                  pl.BlockSpec((MOBA_HEADS, tm // MOBA_BLOCK, 8, MOBA_BLOCK), lambda i: (0, i, 0, 0)),
                  row(p.shape[1])]
        + [_const_spec(a.shape) for a in consts],
        out_specs=row(d),
        out_shape=jax.ShapeDtypeStruct((t, d), F32),
        compiler_params=pltpu.CompilerParams(dimension_semantics=("parallel",), vmem_limit_bytes=VMEM_LIMIT),
        name="out_ffn",
    )(h, og, om, ol, p, *consts)


def kernel(x, p, ffn1_norm, ffn1_w_gu, ffn1_w_down, mix_norm, w_in, w_gk_up, b_gk, gla_out_norm, w_out,
           ffn2_norm, ffn2_w_gu, ffn2_w_down, ple_norm, w_ple_gate, w_ple_proj, rel_bias, final_norm):
    batch, t, d = x.shape
    depth = p.shape[0]
    assert batch == 1 and t % ROW_TILE == 0 and ROW_TILE % (KV_GROUP * MOBA_BLOCK) == 0
    row = lambda a: a.reshape(1, -1)
    h = x[0]
    for i in range(depth):
        bias, wgu1, wd1, wa, wlr, wm = _prologue(rel_bias, ffn1_w_gu[i], ffn1_w_down[i], w_in[i])
        wgk = jnp.pad(w_gk_up[i], ((0, LANES - GLA_LOWRANK), (0, 0))).astype(BF16)
        h1, qg, kg, vg, gg, la, qt, ka, vt = _ffn_in(
            h, row(ffn1_norm[i]), wgu1, wd1, row(mix_norm[i]), wa, wlr, wm, wgk, row(b_gk[i]))
        og, (wo, wgu2, wd2, wpg) = _gla(qg, kg, la, vg, gg, row(gla_out_norm[i]),
                                        [w_out[i], ffn2_w_gu[i], ffn2_w_down[i], w_ple_gate[i]])
        om, ol = _moba_attn(qt, ka, vt, bias)
        h = _out_ffn(i == depth - 1, h1, og, om, ol, p[i, 0], wo, row(ffn2_norm[i]), wgu2, wd2, row(ple_norm[i]),
                     wpg, w_ple_proj[i].astype(BF16), row(final_norm))
    return h[None]
```

```python
import functools
import math

import jax
import jax.numpy as jnp
import numpy as np
from jax import lax
from jax.experimental import pallas as pl
from jax.experimental.pallas import tpu as pltpu

F32 = jnp.float32
BF16 = jnp.bfloat16

GLA_HEADS = 4
GLA_DK = 64
GLA_DV = 128
GLA_LOWRANK = 16
GLA_TAU = 16.0
MOBA_HEADS = 8
MOBA_HD = 64
MOBA_BLOCK = 256
MOBA_TOPK = 3
REL_BUCKETS = 32
REL_MAX_DIST = 4096
EPS = 1e-6
NEG = -1e30

GLA_QK = GLA_HEADS * GLA_DK
GLA_V = GLA_HEADS * GLA_DV
GLA_COLS = 2 * GLA_QK + 2 * GLA_V
MOBA_W = MOBA_HEADS * MOBA_HD

LANES = 128
BF16_SUBLANES = 16
VMEM_LIMIT = 56 * 1024 * 1024

ROW_TILE = 512
GLA_TILE = 256
GLA_LEVELS = tuple(GLA_TILE >> (i + 1) for i in range(GLA_TILE.bit_length() - 1))
_FAR_DIST = math.ceil((REL_BUCKETS // 2) * (REL_MAX_DIST / (REL_BUCKETS // 2))
                      ** ((REL_BUCKETS // 2 - 1) / (REL_BUCKETS // 2)))
_FAR_BLOCKS = -(-(_FAR_DIST - 1) // MOBA_BLOCK) + 1
BIAS_TILES = _FAR_BLOCKS + 1
KV_GROUP = 2
ATTN_UNROLL = 32
V_ROWS = 80
LOG2E = 1.4426950408889634


def _rms(x, g):
    return x * lax.rsqrt(jnp.mean(x * x, axis=-1, keepdims=True) + EPS) * g


def _ff_chunks(d_ff):
    step = 1024
    return [(s, min(step, d_ff - s)) for s in range(0, d_ff, step)]


def _swiglu(xn, wgu_ref, wd_ref):
    d_ff = wd_ref.shape[0]
    acc = None
    for start, size in _ff_chunks(d_ff):
        gate = jnp.dot(xn, wgu_ref[:, start:start + size], preferred_element_type=F32)
        up = jnp.dot(xn, wgu_ref[:, d_ff + start:d_ff + start + size], preferred_element_type=F32)
        act = (gate * jax.nn.sigmoid(gate) * up).astype(BF16)
        part = jnp.dot(act, wd_ref[start:start + size, :], preferred_element_type=F32)
        acc = part if acc is None else acc + part
    return acc


def _const_spec(shape):
    return pl.BlockSpec(shape, lambda *_: (0,) * len(shape))


def _moba_operands(pair, q, k, v, first_block, km_ref, qt_ref, ka_ref, vt_ref):
    blk = MOBA_BLOCK
    rows = q.shape[0]
    n_blocks = rows // blk
    hd = MOBA_HD
    means = jnp.mean(k.reshape(n_blocks, blk, LANES), axis=1)
    km = km_ref[pair]
    km_row = lax.broadcasted_iota(jnp.int32, km.shape, 0)
    for b in range(n_blocks):
        km = jnp.where(km_row == first_block + b, means[b:b + 1], km)
    km_ref[pair] = km
    km_lane = lax.broadcasted_iota(jnp.int32, km.shape, 1)
    q_t = q.T
    v_t = v.T
    block = lax.broadcasted_iota(jnp.int32, (hd, rows), 0)
    block_f = block.astype(F32)
    own = first_block + lax.broadcasted_iota(jnp.int32, (hd, rows), 1) // blk
    lane = lax.broadcasted_iota(jnp.int32, (rows, LANES), 1)
    own_k = first_block + lax.broadcasted_iota(jnp.int32, (rows, LANES), 0) // blk
    ones_rows = jnp.where(lax.broadcasted_iota(jnp.int32, (V_ROWS - hd, rows), 0) == 0, 1.0, 0.0)
    for a in range(2):
        head = 2 * pair + a
        gate = jnp.dot(jnp.where(km_lane // hd == a, km, 0.0), q_t, precision=lax.Precision.HIGHEST,
                       preferred_element_type=F32)
        gate = jnp.where(block < own, gate, -jnp.inf)
        chosen = jnp.where(block == own, 1.0, 0.0)
        for _ in range(MOBA_TOPK):
            top = jnp.max(gate, axis=0, keepdims=True)
            first = jnp.min(jnp.where(gate == top, block_f, float(hd)), axis=0, keepdims=True)
            hit = block_f == first
            chosen = jnp.where(hit, 1.0, chosen)
            gate = jnp.where(hit, -jnp.inf, gate)
        q_a = q_t[a * hd:(a + 1) * hd] * (hd ** -0.5 * LOG2E)
        q_aug_t = jnp.concatenate([q_a, chosen - 1.0] if a == 0 else [chosen - 1.0, q_a], axis=0).astype(BF16)
        v_aug_t = jnp.concatenate([v_t[a * hd:(a + 1) * hd], ones_rows], axis=0).astype(BF16)
        k_aug = jnp.where(lane // hd == a, k, jnp.where(lane - (1 - a) * hd == own_k, -NEG, 0.0)).astype(BF16)
        for b in range(n_blocks):
            qt_ref[head, b] = q_aug_t[:, b * blk:(b + 1) * blk]
            ka_ref[head, b] = k_aug[b * blk:(b + 1) * blk, :]
        group = KV_GROUP * blk
        for g in range(n_blocks // KV_GROUP):
            vt_ref[head, g] = v_aug_t[:, g * group:(g + 1) * group]


def _ffn_in_kernel(x_ref, n1_ref, wgu_ref, wd_ref, n2_ref, wa_ref, wlr_ref, wm_ref, wgk_ref, bgk_ref,
                   h_ref, qg_ref, kg_ref, vg_ref, gg_ref, la_ref, qt_ref, ka_ref, vt_ref, km_ref):
    @pl.when(pl.program_id(0) == 0)
    def _():
        km_ref[...] = jnp.zeros_like(km_ref)

    x = x_ref[...]
    h = x + 0.5 * _swiglu(_rms(x, n1_ref[...]).astype(BF16), wgu_ref, wd_ref)
    h_ref[...] = h
    n = _rms(h, n2_ref[...]).astype(BF16)
    ua = jnp.dot(n, wa_ref[...], preferred_element_type=F32)
    qg_ref[...] = ua[:, 0:GLA_QK]
    kg_ref[...] = ua[:, GLA_QK:2 * GLA_QK]
    vg_ref[...] = ua[:, 2 * GLA_QK:2 * GLA_QK + GLA_V]
    gg_ref[...] = ua[:, 2 * GLA_QK + GLA_V:]
    lr = jnp.dot(n, wlr_ref[...], preferred_element_type=F32)
    z = jnp.dot(lr.astype(BF16), wgk_ref[...], preferred_element_type=F32) + bgk_ref[...]
    la_ref[...] = (jnp.minimum(z, 0.0) - jnp.log1p(jnp.exp(-jnp.abs(z)))) * (1.0 / GLA_TAU)
    um = jnp.dot(n, wm_ref[...], preferred_element_type=F32)
    first_block = pl.program_id(0) * (x.shape[0] // MOBA_BLOCK)
    for pair in range(MOBA_HEADS // 2):
        q, k, v = (um[:, c * MOBA_W + pair * LANES:c * MOBA_W + (pair + 1) * LANES] for c in range(3))
        _moba_operands(pair, q, k, v, first_block, km_ref, qt_ref, ka_ref, vt_ref)


def _ffn_in(x, n1, wgu, wd, n2, wa, wlr, wm, wgk, bgk):
    t, d = x.shape
    tm = min(ROW_TILE, t)
    blk = MOBA_BLOCK
    nb = t // blk
    assert nb <= MOBA_HD
    step_blocks = tm // blk
    row = lambda w: pl.BlockSpec((tm, w), lambda i: (i, 0))
    outs = [(d, F32), (GLA_QK, F32), (GLA_QK, F32), (GLA_V, F32), (GLA_V, F32), (GLA_QK, F32)]
    heads = lambda shape: pl.BlockSpec((MOBA_HEADS,) + shape, lambda i: (0, i, 0, 0))
    return pl.pallas_call(
        _ffn_in_kernel,
        grid=(t // tm,),
        in_specs=[row(d)] + [_const_spec(a.shape) for a in (n1, wgu, wd, n2, wa, wlr, wm, wgk, bgk)],
        out_specs=[row(w) for w, _ in outs] + [heads((step_blocks, LANES, blk)), heads((step_blocks, blk, LANES)),
                                               heads((step_blocks // KV_GROUP, V_ROWS, KV_GROUP * blk))],
        out_shape=[jax.ShapeDtypeStruct((t, w), dt) for w, dt in outs]
        + [jax.ShapeDtypeStruct((MOBA_HEADS, nb, LANES, blk), BF16),
           jax.ShapeDtypeStruct((MOBA_HEADS, nb, blk, LANES), BF16),
           jax.ShapeDtypeStruct((MOBA_HEADS, nb // KV_GROUP, V_ROWS, KV_GROUP * blk), BF16)],
        scratch_shapes=[pltpu.VMEM((MOBA_HEADS // 2, MOBA_HD, LANES), F32)],
        compiler_params=pltpu.CompilerParams(dimension_semantics=("arbitrary",), vmem_limit_bytes=VMEM_LIMIT),
        name="ffn_in",
    )(x, n1, wgu, wd, n2, wa, wlr, wm, wgk, bgk)


def _gla_span_matrix(tile):
    i = np.arange(tile)[:, None]
    t = np.arange(tile)[None, :]
    mats = [t <= i, t > i]
    for s in GLA_LEVELS:
        split = (i // (2 * s)) * (2 * s) + s - 1
        mats.append(np.where((i & s) != 0, (t > split) & (t <= i), (t > i) & (t <= split)))
    return jnp.asarray(np.concatenate(mats, axis=0), BF16)


def _gla_kernel(n_cast, q_ref, k_ref, la_ref, v_ref, g_ref, gn_ref, span_ref, *refs):
    o_ref = refs[n_cast]
    st_ref = refs[2 * n_cast + 1]
    for w_ref, w_bf16_ref in zip(refs[:n_cast], refs[n_cast + 1:2 * n_cast + 1]):
        w_bf16_ref[...] = w_ref[...].astype(BF16)

    @pl.when(pl.program_id(0) == 0)
    def _():
        st_ref[...] = jnp.zeros_like(st_ref)

    tb = q_ref.shape[0]
    half = tb // 2
    la = la_ref[...]
    la_hi = la.astype(BF16)
    la_lo = (la - la_hi.astype(F32)).astype(BF16)

    def span_sum(r):
        m = span_ref[r * tb:(r + 1) * tb, :]
        return jnp.dot(m, la_hi, preferred_element_type=F32) + jnp.dot(m, la_lo, preferred_element_type=F32)

    b = span_sum(0)
    q = q_ref[...] * (GLA_DK ** -0.5)
    k = k_ref[...]
    qe = q * jnp.exp(b)
    ke = k * jnp.exp(span_sum(1))
    a_last = jnp.exp(b[tb - 1:tb, :])
    row = lax.broadcasted_iota(jnp.int32, (tb, GLA_QK), 0)
    q_lvl, k_lvl = [], []
    for idx, s in enumerate(GLA_LEVELS):
        e = jnp.exp(span_sum(2 + idx))
        upper = (row & s) != 0
        q_lvl.append(jnp.where(upper, q * e, 0.0))
        k_lvl.append(jnp.where(upper, 0.0, k * e))
    ri = lax.broadcasted_iota(jnp.int32, (half, 2 * half), 0)
    ci = lax.broadcasted_iota(jnp.int32, (half, 2 * half), 1) % half
    same_block = [ri // (2 * s) == ci // (2 * s) for s in GLA_LEVELS[1:]]
    diagonal = ri == ci
    first_head = lax.broadcasted_iota(jnp.int32, (half, 2 * GLA_DK), 1) < GLA_DK
    gn = gn_ref[...]
    nt = (((1,), (1,)), ((), ()))

    def pair(x, y, r0, c0, lanes):
        keys = y[c0:c0 + half, lanes]
        keys = jnp.concatenate([jnp.where(first_head, keys, 0.0), jnp.where(first_head, 0.0, keys)], axis=0)
        return lax.dot_general(x[r0:r0 + half, lanes].astype(BF16), keys.astype(BF16), nt,
                               preferred_element_type=F32)

    intra = []
    for hp in range(GLA_HEADS // 2):
        lanes = slice(2 * hp * GLA_DK, 2 * (hp + 1) * GLA_DK)
        diag_blocks = []
        for c in (0, half):
            blk = jnp.where(diagonal, pair(q, k, c, c, lanes), 0.0)
            for idx in range(1, len(GLA_LEVELS)):
                blk = blk + jnp.where(same_block[idx - 1], pair(q_lvl[idx], k_lvl[idx], c, c, lanes), 0.0)
            diag_blocks.append(blk)
        low = pair(q_lvl[0], k_lvl[0], half, 0, lanes)
        for a in range(2):
            cols = slice(a * half, (a + 1) * half)
            intra.append(jnp.concatenate(
                [jnp.concatenate([diag_blocks[0][:, cols], jnp.zeros((half, half), F32)], axis=1),
                 jnp.concatenate([low[:, cols], diag_blocks[1][:, cols]], axis=1)], axis=0).astype(BF16))
    for h in range(GLA_HEADS):
        ks = slice(h * GLA_DK, (h + 1) * GLA_DK)
        vs = slice(h * GLA_DV, (h + 1) * GLA_DV)
        attn = intra[h]
        v = v_ref[:, vs].astype(BF16)
        st = st_ref[h]
        o = jnp.dot(attn, v, preferred_element_type=F32)
        o = o + lax.dot_general(qe[:, ks].astype(BF16), st.astype(BF16), nt, preferred_element_type=F32)
        st_ref[h] = st * a_last[:, ks] + lax.dot_general(v, ke[:, ks].astype(BF16), (((0,), (0,)), ((), ())),
                                                         preferred_element_type=F32)
        g = g_ref[:, vs]
        o_ref[:, vs] = _rms(o, gn) * (g * jax.nn.sigmoid(g))


def _cast_spec(shape, steps):
    rows = shape[0]
    hold = next(d for d in range(1, steps + 1) if steps % d == 0 and rows * d % (steps * BF16_SUBLANES) == 0)
    return pl.BlockSpec((rows * hold // steps, shape[1]), lambda i: (i // hold, 0))


def _gla(qg, kg, la, vg, gg, gn, weights):
    t = qg.shape[0]
    tb = GLA_TILE
    steps = t // tb
    span = _gla_span_matrix(tb)
    row = lambda w: pl.BlockSpec((tb, w), lambda i: (i, 0))
    cast_specs = [_cast_spec(w.shape, steps) for w in weights]
    out = pl.pallas_call(
        functools.partial(_gla_kernel, len(weights)),
        grid=(steps,),
        in_specs=[row(GLA_QK), row(GLA_QK), row(GLA_QK), row(GLA_V), row(GLA_V), _const_spec(gn.shape),
                  _const_spec(span.shape)] + cast_specs,
        out_specs=[row(GLA_V)] + cast_specs,
        out_shape=[jax.ShapeDtypeStruct((t, GLA_V), F32)] + [jax.ShapeDtypeStruct(w.shape, BF16) for w in weights],
        scratch_shapes=[pltpu.VMEM((GLA_HEADS, GLA_DV, GLA_DK), F32)],
        compiler_params=pltpu.CompilerParams(dimension_semantics=("arbitrary",), vmem_limit_bytes=VMEM_LIMIT),
        name="gla",
    )(qg, kg, la, vg, gg, gn, span, *weights)
    return out[0], out[1:]


def _prologue_kernel(tab_ref, wgu_ref, wd_ref, win_ref, o_ref, wgu_out, wd_out, wa_out, wlr_out, wm_out):
    wgu_out[...] = wgu_ref[...].astype(BF16)
    wd_out[...] = wd_ref[...].astype(BF16)
    wa_out[...] = win_ref[:GLA_COLS, :].T.astype(BF16)
    wlr_out[...] = win_ref[GLA_COLS:GLA_COLS + LANES, :].T.astype(BF16)
    wm_out[...] = win_ref[GLA_COLS + GLA_LOWRANK:, :].T.astype(BF16)

    h = pl.program_id(0)
    blk = MOBA_BLOCK
    max_exact = REL_BUCKETS // 2
    x = lax.broadcasted_iota(jnp.int32, (8, 2 * blk), 1)
    table = jnp.broadcast_to(tab_ref[pl.ds(h, 1), :], (8, LANES))
    for delta in range(BIAS_TILES):
        dist = delta * blk + x - blk
        n = jnp.maximum(dist, 0)
        nf = jnp.maximum(n, 1).astype(F32)
        large = max_exact + (jnp.log(nf / max_exact) / math.log(REL_MAX_DIST / max_exact)
                             * (REL_BUCKETS - max_exact)).astype(jnp.int32)
        large = jnp.minimum(large, REL_BUCKETS - 1)
        bucket = jnp.where(n < max_exact, n, large)
        bias = jnp.concatenate([jnp.take_along_axis(table, bucket[:, c:c + LANES], axis=1)
                                for c in range(0, 2 * blk, LANES)], axis=1)
        by_dist = jnp.where(dist >= 0, bias * LOG2E, NEG)
        rows = jnp.broadcast_to(by_dist[0:1], (blk, 2 * blk))
        o_ref[0, delta] = pltpu.roll(rows, 0, 1, stride=1, stride_axis=0)[:, blk:].astype(o_ref.dtype)


def _prologue(rel_bias, wgu, wd, w_in):
    blk = MOBA_BLOCK
    steps = MOBA_HEADS
    table = jnp.pad(rel_bias.T, ((0, 0), (0, LANES - REL_BUCKETS)))
    d, width = w_in.shape
    moba_cols = width - GLA_COLS - GLA_LOWRANK
    slab = _cast_spec(w_in.shape, steps).block_shape[0]
    assert slab * steps == d and slab % LANES == 0
    w_slab = lambda cols: pl.BlockSpec((slab, cols), lambda i: (i, 0))
    w_in_t = w_in.T
    return pl.pallas_call(
        _prologue_kernel,
        grid=(steps,),
        in_specs=[_const_spec(table.shape), _cast_spec(wgu.shape, steps), _cast_spec(wd.shape, steps),
                  pl.BlockSpec((width, slab), lambda i: (0, i))],
        out_specs=[pl.BlockSpec((1, BIAS_TILES, blk, blk), lambda h: (h, 0, 0, 0)), _cast_spec(wgu.shape, steps),
                   _cast_spec(wd.shape, steps), w_slab(GLA_COLS), w_slab(LANES), w_slab(moba_cols)],
        out_shape=[jax.ShapeDtypeStruct((MOBA_HEADS, BIAS_TILES, blk, blk), BF16),
                   jax.ShapeDtypeStruct(wgu.shape, BF16), jax.ShapeDtypeStruct(wd.shape, BF16),
                   jax.ShapeDtypeStruct((d, GLA_COLS), BF16), jax.ShapeDtypeStruct((d, LANES), BF16),
                   jax.ShapeDtypeStruct((d, moba_cols), BF16)],
        compiler_params=pltpu.CompilerParams(dimension_semantics=("parallel",), vmem_limit_bytes=VMEM_LIMIT),
        name="prologue",
    )(table, wgu, wd, w_in_t)


def _moba_attn_kernel(qi_ref, gi_ref, qt_ref, ka_ref, vt_ref, bias_ref, o_ref, l_ref, s0_ref, s1_ref, p0_ref,
                      p1_ref, acc_ref):
    n_pairs = qi_ref.shape[0] - 2
    blk = MOBA_BLOCK
    s_refs = (s0_ref, s1_ref)
    p_refs = (p0_ref, p1_ref)

    def scores(t, slot):
        s_ref = s_refs[slot]
        qi = qi_ref[t]
        gi = gi_ref[t]
        qt = qt_ref[0, qi]
        col_max = None
        for b in range(KV_GROUP):
            j = gi * KV_GROUP + b
            s = jnp.dot(ka_ref[0, j], qt, preferred_element_type=F32).astype(BF16)
            s = s + bias_ref[0, jnp.clip(qi - j, 0, BIAS_TILES - 1)]
            s_ref[b] = s
            cm = jnp.max(s.reshape(blk // BF16_SUBLANES, BF16_SUBLANES, blk), axis=0)
            col_max = cm if col_max is None else jnp.maximum(col_max, cm)
        return jnp.max(col_max, axis=0, keepdims=True).astype(F32)

    def softmax(t, slot, m_run, col_max):
        m_prev = jnp.where(gi_ref[t] == 0, -jnp.inf, m_run)
        m = jnp.maximum(m_prev, col_max)
        for b in range(KV_GROUP):
            p_refs[slot][b * blk:(b + 1) * blk, :] = jnp.exp2(s_refs[slot][b] - m.astype(BF16))
        return m, jnp.exp2(m_prev - m)

    def values(t, slot, alpha):
        acc = acc_ref[...] * alpha + jnp.dot(vt_ref[0, gi_ref[t]], p_refs[slot][...], preferred_element_type=F32)
        acc_ref[...] = acc
        o_ref[0, qi_ref[t]] = acc[0:MOBA_HD].astype(o_ref.dtype)
        l_ref[0, qi_ref[t]] = acc[MOBA_HD:MOBA_HD + 8]

    acc_ref[...] = jnp.zeros_like(acc_ref)
    cm0 = scores(0, 0)
    cm1 = scores(1, 1)
    m0, alpha0 = softmax(0, 0, jnp.full((1, blk), -jnp.inf, F32), cm0)

    unroll = max(u for u in range(2, ATTN_UNROLL + 1, 2) if n_pairs % u == 0)

    def step(u, carry):
        m, alpha, cm = carry
        for i in range(unroll):
            t = unroll * u + i
            cm_next = scores(t + 2, i % 2)
            m, alpha_next = softmax(t + 1, 1 - i % 2, m, cm)
            values(t, i % 2, alpha)
            alpha, cm = alpha_next, cm_next
        return m, alpha, cm

    lax.fori_loop(0, n_pairs // unroll, step, (m0, alpha0, cm1))


def _moba_attn(qt, ka, vt, bias):
    nb = qt.shape[1]
    blk = MOBA_BLOCK
    pairs = [(qi, g) for qi in range(nb) for g in range(qi // KV_GROUP + 1)] + [(0, 0), (0, 0)]
    qi_tab = jnp.asarray([a for a, _ in pairs], jnp.int32)
    gi_tab = jnp.asarray([b for _, b in pairs], jnp.int32)
    head = lambda shape: pl.BlockSpec((1,) + shape, lambda h, *_: (h,) + (0,) * len(shape))
    return pl.pallas_call(
        _moba_attn_kernel,
        grid_spec=pltpu.PrefetchScalarGridSpec(
            num_scalar_prefetch=2,
            grid=(MOBA_HEADS,),
            in_specs=[head((nb, LANES, blk)), head((nb, blk, LANES)),
                      head((nb // KV_GROUP, V_ROWS, KV_GROUP * blk)), head((BIAS_TILES, blk, blk))],
            out_specs=[head((nb, MOBA_HD, blk)), head((nb, 8, blk))],
            scratch_shapes=[pltpu.VMEM((KV_GROUP, blk, blk), BF16), pltpu.VMEM((KV_GROUP, blk, blk), BF16),
                            pltpu.VMEM((KV_GROUP * blk, blk), BF16), pltpu.VMEM((KV_GROUP * blk, blk), BF16),
                            pltpu.VMEM((V_ROWS, blk), F32)]),
        out_shape=[jax.ShapeDtypeStruct((MOBA_HEADS, nb, MOBA_HD, blk), BF16),
                   jax.ShapeDtypeStruct((MOBA_HEADS, nb, 8, blk), F32)],
        compiler_params=pltpu.CompilerParams(dimension_semantics=("parallel",), vmem_limit_bytes=VMEM_LIMIT),
        name="moba_attn",
    )(qi_tab, gi_tab, qt, ka, vt, bias)


def _out_ffn_kernel(final, h_ref, og_ref, om_ref, ol_ref, p_ref, wo_ref, n3_ref, wgu_ref, wd_ref,
                    n4_ref, wpg_ref, wpp_ref, nf_ref, o_ref):
    mix = jnp.dot(og_ref[...].astype(BF16), wo_ref[0:GLA_V, :], preferred_element_type=F32)
    om = [(om_ref[:, b].astype(F32) / ol_ref[:, b, 0:1, :]).reshape(MOBA_W, MOBA_BLOCK).T.astype(BF16)
          for b in range(om_ref.shape[1])]
    mix = mix + jnp.dot(jnp.concatenate(om, axis=0), wo_ref[GLA_V:, :], preferred_element_type=F32)
    h = h_ref[...] + mix
    h = h + 0.5 * _swiglu(_rms(h, n3_ref[...]).astype(BF16), wgu_ref, wd_ref)
    gate = jax.nn.sigmoid(jnp.dot(_rms(h, n4_ref[...]).astype(BF16), wpg_ref[...], preferred_element_type=F32))
    h = h + gate * jnp.dot(p_ref[...].astype(BF16), wpp_ref[...], preferred_element_type=F32)
    o_ref[...] = _rms(h, nf_ref[...]) if final else h


def _out_ffn(final, h, og, om, ol, p, wo, n3, wgu, wd, n4, wpg, wpp, nf):
    t, d = h.shape
    tm = min(ROW_TILE, t)
    row = lambda w: pl.BlockSpec((tm, w), lambda i: (i, 0))
    consts = (wo, n3, wgu, wd, n4, wpg, wpp, nf)
    return pl.pallas_call(
        functools.partial(_out_ffn_kernel, final),
        grid=(t // tm,),
        in_specs=[row(d), row(GLA_V),
                  pl.BlockSpec((MOBA_HEADS, tm // MOBA_BLOCK, MOBA_HD, MOBA_BLOCK), lambda i: (0, i, 0, 0)),
                  pl.BlockSpec((MOBA_HEADS, tm // MOBA_BLOCK, 8, MOBA_BLOCK), lambda i: (0, i, 0, 0)),
                  row(p.shape[1])]
        + [_const_spec(a.shape) for a in consts],
        out_specs=row(d),
        out_shape=jax.ShapeDtypeStruct((t, d), F32),
        compiler_params=pltpu.CompilerParams(dimension_semantics=("parallel",), vmem_limit_bytes=VMEM_LIMIT),
        name="out_ffn",
    )(h, og, om, ol, p, *consts)


def kernel(x, p, ffn1_norm, ffn1_w_gu, ffn1_w_down, mix_norm, w_in, w_gk_up, b_gk, gla_out_norm, w_out,
           ffn2_norm, ffn2_w_gu, ffn2_w_down, ple_norm, w_ple_gate, w_ple_proj, rel_bias, final_norm):
    batch, t, d = x.shape
    depth = p.shape[0]
    assert batch == 1 and t % ROW_TILE == 0 and ROW_TILE % (KV_GROUP * MOBA_BLOCK) == 0
    row = lambda a: a.reshape(1, -1)
    h = x[0]
    for i in range(depth):
        bias, wgu1, wd1, wa, wlr, wm = _prologue(rel_bias, ffn1_w_gu[i], ffn1_w_down[i], w_in[i])
        wgk = jnp.pad(w_gk_up[i], ((0, LANES - GLA_LOWRANK), (0, 0))).astype(BF16)
        h1, qg, kg, vg, gg, la, qt, ka, vt = _ffn_in(
            h, row(ffn1_norm[i]), wgu1, wd1, row(mix_norm[i]), wa, wlr, wm, wgk, row(b_gk[i]))
        og, (wo, wgu2, wd2, wpg) = _gla(qg, kg, la, vg, gg, row(gla_out_norm[i]),
                                        [w_out[i], ffn2_w_gu[i], ffn2_w_down[i], w_ple_gate[i]])
        om, ol = _moba_attn(qt, ka, vt, bias)
        h = _out_ffn(i == depth - 1, h1, og, om, ol, p[i, 0], wo, row(ffn2_norm[i]), wgu2, wd2, row(ple_norm[i]),
                     wpg, w_ple_proj[i].astype(BF16), row(final_norm))
    return h[None]
```

```python
import functools
import math

import jax
import jax.numpy as jnp
import numpy as np
from jax import lax
from jax.experimental import pallas as pl
from jax.experimental.pallas import tpu as pltpu

F32 = jnp.float32
BF16 = jnp.bfloat16

GLA_HEADS = 4
GLA_DK = 64
GLA_DV = 128
GLA_LOWRANK = 16
GLA_TAU = 16.0
MOBA_HEADS = 8
MOBA_HD = 64
MOBA_BLOCK = 256
MOBA_TOPK = 3
REL_BUCKETS = 32
REL_MAX_DIST = 4096
EPS = 1e-6
NEG = -1e30

GLA_QK = GLA_HEADS * GLA_DK
GLA_V = GLA_HEADS * GLA_DV
GLA_COLS = 2 * GLA_QK + 2 * GLA_V
MOBA_W = MOBA_HEADS * MOBA_HD

LANES = 128
BF16_SUBLANES = 16
VMEM_LIMIT = 56 * 1024 * 1024

ROW_TILE = 512
GLA_TILE = 256
GLA_STEP_TILES = 4
GLA_LEVELS = tuple(GLA_TILE >> (i + 1) for i in range(GLA_TILE.bit_length() - 1))
_FAR_DIST = math.ceil((REL_BUCKETS // 2) * (REL_MAX_DIST / (REL_BUCKETS // 2))
                      ** ((REL_BUCKETS // 2 - 1) / (REL_BUCKETS // 2)))
_FAR_BLOCKS = -(-(_FAR_DIST - 1) // MOBA_BLOCK) + 1
BIAS_TILES = _FAR_BLOCKS + 1
KV_GROUP = 2
ATTN_UNROLL = 32
V_ROWS = 80
LOG2E = 1.4426950408889634


def _rms(x, g):
    return x * lax.rsqrt(jnp.mean(x * x, axis=-1, keepdims=True) + EPS) * g


def _ff_chunks(d_ff):
    step = 1024
    return [(s, min(step, d_ff - s)) for s in range(0, d_ff, step)]


def _swiglu(xn, wgu_ref, wd_ref):
    d_ff = wd_ref.shape[0]
    acc = None
    for start, size in _ff_chunks(d_ff):
        gate = jnp.dot(xn, wgu_ref[:, start:start + size], preferred_element_type=F32)
        up = jnp.dot(xn, wgu_ref[:, d_ff + start:d_ff + start + size], preferred_element_type=F32)
        act = (gate * jax.nn.sigmoid(gate) * up).astype(BF16)
        part = jnp.dot(act, wd_ref[start:start + size, :], preferred_element_type=F32)
        acc = part if acc is None else acc + part
    return acc


def _const_spec(shape):
    return pl.BlockSpec(shape, lambda *_: (0,) * len(shape))


def _moba_operands(pair, q, k, v, first_block, km_ref, qt_ref, ka_ref, vt_ref):
    blk = MOBA_BLOCK
    rows = q.shape[0]
    n_blocks = rows // blk
    hd = MOBA_HD
    means = jnp.mean(k.reshape(n_blocks, blk, LANES), axis=1)
    km = km_ref[pair]
    km_row = lax.broadcasted_iota(jnp.int32, km.shape, 0)
    for b in range(n_blocks):
        km = jnp.where(km_row == first_block + b, means[b:b + 1], km)
    km_ref[pair] = km
    km_lane = lax.broadcasted_iota(jnp.int32, km.shape, 1)
    q_t = q.T
    v_t = v.T
    block = lax.broadcasted_iota(jnp.int32, (hd, rows), 0)
    block_f = block.astype(F32)
    own = first_block + lax.broadcasted_iota(jnp.int32, (hd, rows), 1) // blk
    lane = lax.broadcasted_iota(jnp.int32, (rows, LANES), 1)
    own_k = first_block + lax.broadcasted_iota(jnp.int32, (rows, LANES), 0) // blk
    ones_rows = jnp.where(lax.broadcasted_iota(jnp.int32, (V_ROWS - hd, rows), 0) == 0, 1.0, 0.0)
    for a in range(2):
        head = 2 * pair + a
        gate = jnp.dot(jnp.where(km_lane // hd == a, km, 0.0), q_t, precision=lax.Precision.HIGHEST,
                       preferred_element_type=F32)
        gate = jnp.where(block < own, gate, -jnp.inf)
        chosen = jnp.where(block == own, 1.0, 0.0)
        for _ in range(MOBA_TOPK):
            top = jnp.max(gate, axis=0, keepdims=True)
            first = jnp.min(jnp.where(gate == top, block_f, float(hd)), axis=0, keepdims=True)
            hit = block_f == first
            chosen = jnp.where(hit, 1.0, chosen)
            gate = jnp.where(hit, -jnp.inf, gate)
        q_a = q_t[a * hd:(a + 1) * hd] * (hd ** -0.5 * LOG2E)
        q_aug_t = jnp.concatenate([q_a, chosen - 1.0] if a == 0 else [chosen - 1.0, q_a], axis=0).astype(BF16)
        v_aug_t = jnp.concatenate([v_t[a * hd:(a + 1) * hd], ones_rows], axis=0).astype(BF16)
        k_aug = jnp.where(lane // hd == a, k, jnp.where(lane - (1 - a) * hd == own_k, -NEG, 0.0)).astype(BF16)
        for b in range(n_blocks):
            qt_ref[head, b] = q_aug_t[:, b * blk:(b + 1) * blk]
            ka_ref[head, b] = k_aug[b * blk:(b + 1) * blk, :]
        group = KV_GROUP * blk
        for g in range(n_blocks // KV_GROUP):
            vt_ref[head, g] = v_aug_t[:, g * group:(g + 1) * group]


def _ffn_in_kernel(x_ref, n1_ref, wgu_ref, wd_ref, n2_ref, wa_ref, wlr_ref, wm_ref, wgk_ref, bgk_ref,
                   h_ref, qg_ref, kg_ref, vg_ref, gg_ref, la_ref, qt_ref, ka_ref, vt_ref, km_ref):
    @pl.when(pl.program_id(0) == 0)
    def _():
        km_ref[...] = jnp.zeros_like(km_ref)

    x = x_ref[...]
    h = x + 0.5 * _swiglu(_rms(x, n1_ref[...]).astype(BF16), wgu_ref, wd_ref)
    h_ref[...] = h
    n = _rms(h, n2_ref[...]).astype(BF16)
    ua = jnp.dot(n, wa_ref[...], preferred_element_type=F32)
    qg_ref[...] = ua[:, 0:GLA_QK]
    kg_ref[...] = ua[:, GLA_QK:2 * GLA_QK]
    vg_ref[...] = ua[:, 2 * GLA_QK:2 * GLA_QK + GLA_V]
    gg_ref[...] = ua[:, 2 * GLA_QK + GLA_V:]
    lr = jnp.dot(n, wlr_ref[...], preferred_element_type=F32)
    z = jnp.dot(lr.astype(BF16), wgk_ref[...], preferred_element_type=F32) + bgk_ref[...]
    la_ref[...] = (jnp.minimum(z, 0.0) - jnp.log1p(jnp.exp(-jnp.abs(z)))) * (1.0 / GLA_TAU)
    um = jnp.dot(n, wm_ref[...], preferred_element_type=F32)
    first_block = pl.program_id(0) * (x.shape[0] // MOBA_BLOCK)
    for pair in range(MOBA_HEADS // 2):
        q, k, v = (um[:, c * MOBA_W + pair * LANES:c * MOBA_W + (pair + 1) * LANES] for c in range(3))
        _moba_operands(pair, q, k, v, first_block, km_ref, qt_ref, ka_ref, vt_ref)


def _ffn_in(x, n1, wgu, wd, n2, wa, wlr, wm, wgk, bgk):
    t, d = x.shape
    tm = min(ROW_TILE, t)
    blk = MOBA_BLOCK
    nb = t // blk
    assert nb <= MOBA_HD
    step_blocks = tm // blk
    row = lambda w: pl.BlockSpec((tm, w), lambda i: (i, 0))
    outs = [(d, F32), (GLA_QK, F32), (GLA_QK, F32), (GLA_V, F32), (GLA_V, F32), (GLA_QK, F32)]
    heads = lambda shape: pl.BlockSpec((MOBA_HEADS,) + shape, lambda i: (0, i, 0, 0))
    return pl.pallas_call(
        _ffn_in_kernel,
        grid=(t // tm,),
        in_specs=[row(d)] + [_const_spec(a.shape) for a in (n1, wgu, wd, n2, wa, wlr, wm, wgk, bgk)],
        out_specs=[row(w) for w, _ in outs] + [heads((step_blocks, LANES, blk)), heads((step_blocks, blk, LANES)),
                                               heads((step_blocks // KV_GROUP, V_ROWS, KV_GROUP * blk))],
        out_shape=[jax.ShapeDtypeStruct((t, w), dt) for w, dt in outs]
        + [jax.ShapeDtypeStruct((MOBA_HEADS, nb, LANES, blk), BF16),
           jax.ShapeDtypeStruct((MOBA_HEADS, nb, blk, LANES), BF16),
           jax.ShapeDtypeStruct((MOBA_HEADS, nb // KV_GROUP, V_ROWS, KV_GROUP * blk), BF16)],
        scratch_shapes=[pltpu.VMEM((MOBA_HEADS // 2, MOBA_HD, LANES), F32)],
        compiler_params=pltpu.CompilerParams(dimension_semantics=("arbitrary",), vmem_limit_bytes=VMEM_LIMIT),
        name="ffn_in",
    )(x, n1, wgu, wd, n2, wa, wlr, wm, wgk, bgk)


def _gla_span_matrix(tile):
    i = np.arange(tile)[:, None]
    t = np.arange(tile)[None, :]
    mats = [t <= i, t > i]
    for s in GLA_LEVELS:
        split = (i // (2 * s)) * (2 * s) + s - 1
        mats.append(np.where((i & s) != 0, (t > split) & (t <= i), (t > i) & (t <= split)))
    return jnp.asarray(np.concatenate(mats, axis=0), BF16)


def _gla_kernel(n_cast, q_ref, k_ref, la_ref, v_ref, g_ref, gn_ref, span_ref, *refs):
    o_ref = refs[n_cast]
    st_ref = refs[2 * n_cast + 1]
    for w_ref, w_bf16_ref in zip(refs[:n_cast], refs[n_cast + 1:2 * n_cast + 1]):
        w_bf16_ref[...] = w_ref[...].astype(BF16)

    @pl.when(pl.program_id(0) == 0)
    def _():
        st_ref[...] = jnp.zeros_like(st_ref)

    states = [st_ref[h] for h in range(GLA_HEADS)]
    for r0 in range(0, q_ref.shape[0], GLA_TILE):
        states = _gla_tile(r0, states, q_ref, k_ref, la_ref, v_ref, g_ref, gn_ref, span_ref, o_ref)
    for h in range(GLA_HEADS):
        st_ref[h] = states[h]


def _gla_tile(r0, states, q_ref, k_ref, la_ref, v_ref, g_ref, gn_ref, span_ref, o_ref):
    tb = GLA_TILE
    rows = pl.ds(r0, tb)
    half = tb // 2
    la = la_ref[rows, :]
    la_hi = la.astype(BF16)
    la_lo = (la - la_hi.astype(F32)).astype(BF16)

    def span_sum(r):
        m = span_ref[r * tb:(r + 1) * tb, :]
        return jnp.dot(m, la_hi, preferred_element_type=F32) + jnp.dot(m, la_lo, preferred_element_type=F32)

    b = span_sum(0)
    q = q_ref[rows, :] * (GLA_DK ** -0.5)
    k = k_ref[rows, :]
    qe = q * jnp.exp(b)
    ke = k * jnp.exp(span_sum(1))
    a_last = jnp.exp(b[tb - 1:tb, :])
    row = lax.broadcasted_iota(jnp.int32, (tb, GLA_QK), 0)
    q_lvl, k_lvl = [], []
    for idx, s in enumerate(GLA_LEVELS):
        e = jnp.exp(span_sum(2 + idx))
        upper = (row & s) != 0
        q_lvl.append(jnp.where(upper, q * e, 0.0))
        k_lvl.append(jnp.where(upper, 0.0, k * e))
    ri = lax.broadcasted_iota(jnp.int32, (half, 2 * half), 0)
    ci = lax.broadcasted_iota(jnp.int32, (half, 2 * half), 1) % half
    same_block = [ri // (2 * s) == ci // (2 * s) for s in GLA_LEVELS[1:]]
    diagonal = ri == ci
    first_head = lax.broadcasted_iota(jnp.int32, (half, 2 * GLA_DK), 1) < GLA_DK
    gn = gn_ref[...]
    nt = (((1,), (1,)), ((), ()))

    def pair(x, y, r0, c0, lanes):
        keys = y[c0:c0 + half, lanes]
        keys = jnp.concatenate([jnp.where(first_head, keys, 0.0), jnp.where(first_head, 0.0, keys)], axis=0)
        return lax.dot_general(x[r0:r0 + half, lanes].astype(BF16), keys.astype(BF16), nt,
                               preferred_element_type=F32)

    intra = []
    for hp in range(GLA_HEADS // 2):
        lanes = slice(2 * hp * GLA_DK, 2 * (hp + 1) * GLA_DK)
        diag_blocks = []
        for c in (0, half):
            blk = jnp.where(diagonal, pair(q, k, c, c, lanes), 0.0)
            for idx in range(1, len(GLA_LEVELS)):
                blk = blk + jnp.where(same_block[idx - 1], pair(q_lvl[idx], k_lvl[idx], c, c, lanes), 0.0)
            diag_blocks.append(blk)
        low = pair(q_lvl[0], k_lvl[0], half, 0, lanes)
        for a in range(2):
            cols = slice(a * half, (a + 1) * half)
            intra.append(jnp.concatenate(
                [jnp.concatenate([diag_blocks[0][:, cols], jnp.zeros((half, half), F32)], axis=1),
                 jnp.concatenate([low[:, cols], diag_blocks[1][:, cols]], axis=1)], axis=0).astype(BF16))
    new_states = []
    for h in range(GLA_HEADS):
        ks = slice(h * GLA_DK, (h + 1) * GLA_DK)
        vs = slice(h * GLA_DV, (h + 1) * GLA_DV)
        attn = intra[h]
        v = v_ref[rows, vs].astype(BF16)
        st = states[h]
        o = jnp.dot(attn, v, preferred_element_type=F32)
        o = o + lax.dot_general(qe[:, ks].astype(BF16), st.astype(BF16), nt, preferred_element_type=F32)
        new_states.append(st * a_last[:, ks] + lax.dot_general(v, ke[:, ks].astype(BF16), (((0,), (0,)), ((), ())),
                                                               preferred_element_type=F32))
        g = g_ref[rows, vs]
        o_ref[rows, vs] = _rms(o, gn) * (g * jax.nn.sigmoid(g))
    return new_states


def _cast_spec(shape, steps):
    rows = shape[0]
    hold = next(d for d in range(1, steps + 1) if steps % d == 0 and rows * d % (steps * BF16_SUBLANES) == 0)
    return pl.BlockSpec((rows * hold // steps, shape[1]), lambda i: (i // hold, 0))


def _gla(qg, kg, la, vg, gg, gn, weights):
    t = qg.shape[0]
    tb = GLA_TILE * GLA_STEP_TILES
    assert t % tb == 0
    steps = t // tb
    span = _gla_span_matrix(GLA_TILE)
    row = lambda w: pl.BlockSpec((tb, w), lambda i: (i, 0))
    cast_specs = [_cast_spec(w.shape, steps) for w in weights]
    out = pl.pallas_call(
        functools.partial(_gla_kernel, len(weights)),
        grid=(steps,),
        in_specs=[row(GLA_QK), row(GLA_QK), row(GLA_QK), row(GLA_V), row(GLA_V), _const_spec(gn.shape),
                  _const_spec(span.shape)] + cast_specs,
        out_specs=[row(GLA_V)] + cast_specs,
        out_shape=[jax.ShapeDtypeStruct((t, GLA_V), F32)] + [jax.ShapeDtypeStruct(w.shape, BF16) for w in weights],
        scratch_shapes=[pltpu.VMEM((GLA_HEADS, GLA_DV, GLA_DK), F32)],
        compiler_params=pltpu.CompilerParams(dimension_semantics=("arbitrary",), vmem_limit_bytes=VMEM_LIMIT),
        name="gla",
    )(qg, kg, la, vg, gg, gn, span, *weights)
    return out[0], out[1:]


def _prologue_kernel(tab_ref, wgu_ref, wd_ref, win_ref, o_ref, wgu_out, wd_out, wa_out, wlr_out, wm_out):
    wgu_out[...] = wgu_ref[...].astype(BF16)
    wd_out[...] = wd_ref[...].astype(BF16)
    wa_out[...] = win_ref[:GLA_COLS, :].T.astype(BF16)
    wlr_out[...] = win_ref[GLA_COLS:GLA_COLS + LANES, :].T.astype(BF16)
    wm_out[...] = win_ref[GLA_COLS + GLA_LOWRANK:, :].T.astype(BF16)

    h = pl.program_id(0)
    blk = MOBA_BLOCK
    max_exact = REL_BUCKETS // 2
    x = lax.broadcasted_iota(jnp.int32, (8, 2 * blk), 1)
    table = jnp.broadcast_to(tab_ref[pl.ds(h, 1), :], (8, LANES))
    for delta in range(BIAS_TILES):
        dist = delta * blk + x - blk
        n = jnp.maximum(dist, 0)
        nf = jnp.maximum(n, 1).astype(F32)
        large = max_exact + (jnp.log(nf / max_exact) / math.log(REL_MAX_DIST / max_exact)
                             * (REL_BUCKETS - max_exact)).astype(jnp.int32)
        large = jnp.minimum(large, REL_BUCKETS - 1)
        bucket = jnp.where(n < max_exact, n, large)
        bias = jnp.concatenate([jnp.take_along_axis(table, bucket[:, c:c + LANES], axis=1)
                                for c in range(0, 2 * blk, LANES)], axis=1)
        by_dist = jnp.where(dist >= 0, bias * LOG2E, NEG)
        rows = jnp.broadcast_to(by_dist[0:1], (blk, 2 * blk))
        o_ref[0, delta] = pltpu.roll(rows, 0, 1, stride=1, stride_axis=0)[:, blk:].astype(o_ref.dtype)


def _prologue(rel_bias, wgu, wd, w_in):
    blk = MOBA_BLOCK
    steps = MOBA_HEADS
    table = jnp.pad(rel_bias.T, ((0, 0), (0, LANES - REL_BUCKETS)))
    d, width = w_in.shape
    moba_cols = width - GLA_COLS - GLA_LOWRANK
    slab = _cast_spec(w_in.shape, steps).block_shape[0]
    assert slab * steps == d and slab % LANES == 0
    w_slab = lambda cols: pl.BlockSpec((slab, cols), lambda i: (i, 0))
    w_in_t = w_in.T
    return pl.pallas_call(
        _prologue_kernel,
        grid=(steps,),
        in_specs=[_const_spec(table.shape), _cast_spec(wgu.shape, steps), _cast_spec(wd.shape, steps),
                  pl.BlockSpec((width, slab), lambda i: (0, i))],
        out_specs=[pl.BlockSpec((1, BIAS_TILES, blk, blk), lambda h: (h, 0, 0, 0)), _cast_spec(wgu.shape, steps),
                   _cast_spec(wd.shape, steps), w_slab(GLA_COLS), w_slab(LANES), w_slab(moba_cols)],
        out_shape=[jax.ShapeDtypeStruct((MOBA_HEADS, BIAS_TILES, blk, blk), BF16),
                   jax.ShapeDtypeStruct(wgu.shape, BF16), jax.ShapeDtypeStruct(wd.shape, BF16),
                   jax.ShapeDtypeStruct((d, GLA_COLS), BF16), jax.ShapeDtypeStruct((d, LANES), BF16),
                   jax.ShapeDtypeStruct((d, moba_cols), BF16)],
        compiler_params=pltpu.CompilerParams(dimension_semantics=("parallel",), vmem_limit_bytes=VMEM_LIMIT),
        name="prologue",
    )(table, wgu, wd, w_in_t)


def _moba_attn_kernel(qi_ref, gi_ref, qt_ref, ka_ref, vt_ref, bias_ref, o_ref, l_ref, s0_ref, s1_ref, p0_ref,
                      p1_ref, acc_ref):
    n_pairs = qi_ref.shape[0] - 2
    blk = MOBA_BLOCK
    s_refs = (s0_ref, s1_ref)
    p_refs = (p0_ref, p1_ref)

    def scores(t, slot):
        s_ref = s_refs[slot]
        qi = qi_ref[t]
        gi = gi_ref[t]
        qt = qt_ref[0, qi]
        col_max = None
        for b in range(KV_GROUP):
            j = gi * KV_GROUP + b
            s = jnp.dot(ka_ref[0, j], qt, preferred_element_type=F32).astype(BF16)
            s = s + bias_ref[0, jnp.clip(qi - j, 0, BIAS_TILES - 1)]
            s_ref[b] = s
            cm = jnp.max(s.reshape(blk // BF16_SUBLANES, BF16_SUBLANES, blk), axis=0)
            col_max = cm if col_max is None else jnp.maximum(col_max, cm)
        return jnp.max(col_max, axis=0, keepdims=True).astype(F32)

    def softmax(t, slot, m_run, col_max):
        m_prev = jnp.where(gi_ref[t] == 0, -jnp.inf, m_run)
        m = jnp.maximum(m_prev, col_max)
        for b in range(KV_GROUP):
            p_refs[slot][b * blk:(b + 1) * blk, :] = jnp.exp2(s_refs[slot][b] - m.astype(BF16))
        return m, jnp.exp2(m_prev - m)

    def values(t, slot, alpha):
        acc = acc_ref[...] * alpha + jnp.dot(vt_ref[0, gi_ref[t]], p_refs[slot][...], preferred_element_type=F32)
        acc_ref[...] = acc
        o_ref[0, qi_ref[t]] = acc[0:MOBA_HD].astype(o_ref.dtype)
        l_ref[0, qi_ref[t]] = acc[MOBA_HD:MOBA_HD + 8]

    acc_ref[...] = jnp.zeros_like(acc_ref)
    cm0 = scores(0, 0)
    cm1 = scores(1, 1)
    m0, alpha0 = softmax(0, 0, jnp.full((1, blk), -jnp.inf, F32), cm0)

    unroll = max(u for u in range(2, ATTN_UNROLL + 1, 2) if n_pairs % u == 0)

    def step(u, carry):
        m, alpha, cm = carry
        for i in range(unroll):
            t = unroll * u + i
            cm_next = scores(t + 2, i % 2)
            m, alpha_next = softmax(t + 1, 1 - i % 2, m, cm)
            values(t, i % 2, alpha)
            alpha, cm = alpha_next, cm_next
        return m, alpha, cm

    lax.fori_loop(0, n_pairs // unroll, step, (m0, alpha0, cm1))


def _moba_attn(qt, ka, vt, bias):
    nb = qt.shape[1]
    blk = MOBA_BLOCK
    pairs = [(qi, g) for qi in range(nb) for g in range(qi // KV_GROUP + 1)] + [(0, 0), (0, 0)]
    qi_tab = jnp.asarray([a for a, _ in pairs], jnp.int32)
    gi_tab = jnp.asarray([b for _, b in pairs], jnp.int32)
    head = lambda shape: pl.BlockSpec((1,) + shape, lambda h, *_: (h,) + (0,) * len(shape))
    return pl.pallas_call(
        _moba_attn_kernel,
        grid_spec=pltpu.PrefetchScalarGridSpec(
            num_scalar_prefetch=2,
            grid=(MOBA_HEADS,),
            in_specs=[head((nb, LANES, blk)), head((nb, blk, LANES)),
                      head((nb // KV_GROUP, V_ROWS, KV_GROUP * blk)), head((BIAS_TILES, blk, blk))],
            out_specs=[head((nb, MOBA_HD, blk)), head((nb, 8, blk))],
            scratch_shapes=[pltpu.VMEM((KV_GROUP, blk, blk), BF16), pltpu.VMEM((KV_GROUP, blk, blk), BF16),
                            pltpu.VMEM((KV_GROUP * blk, blk), BF16), pltpu.VMEM((KV_GROUP * blk, blk), BF16),
                            pltpu.VMEM((V_ROWS, blk), F32)]),
        out_shape=[jax.ShapeDtypeStruct((MOBA_HEADS, nb, MOBA_HD, blk), BF16),
                   jax.ShapeDtypeStruct((MOBA_HEADS, nb, 8, blk), F32)],
        compiler_params=pltpu.CompilerParams(dimension_semantics=("parallel",), vmem_limit_bytes=VMEM_LIMIT),
        name="moba_attn",
    )(qi_tab, gi_tab, qt, ka, vt, bias)


def _out_ffn_kernel(final, h_ref, og_ref, om_ref, ol_ref, p_ref, wo_ref, n3_ref, wgu_ref, wd_ref,
                    n4_ref, wpg_ref, wpp_ref, nf_ref, o_ref):
    mix = jnp.dot(og_ref[...].astype(BF16), wo_ref[0:GLA_V, :], preferred_element_type=F32)
    om = [(om_ref[:, b].astype(F32) / ol_ref[:, b, 0:1, :]).reshape(MOBA_W, MOBA_BLOCK).T.astype(BF16)
          for b in range(om_ref.shape[1])]
    mix = mix + jnp.dot(jnp.concatenate(om, axis=0), wo_ref[GLA_V:, :], preferred_element_type=F32)
    h = h_ref[...] + mix
    h = h + 0.5 * _swiglu(_rms(h, n3_ref[...]).astype(BF16), wgu_ref, wd_ref)
    gate = jax.nn.sigmoid(jnp.dot(_rms(h, n4_ref[...]).astype(BF16), wpg_ref[...], preferred_element_type=F32))
    h = h + gate * jnp.dot(p_ref[...].astype(BF16), wpp_ref[...], preferred_element_type=F32)
    o_ref[...] = _rms(h, nf_ref[...]) if final else h


def _out_ffn(final, h, og, om, ol, p, wo, n3, wgu, wd, n4, wpg, wpp, nf):
    t, d = h.shape
    tm = min(ROW_TILE, t)
    row = lambda w: pl.BlockSpec((tm, w), lambda i: (i, 0))
    consts = (wo, n3, wgu, wd, n4, wpg, wpp, nf)
    return pl.pallas_call(
        functools.partial(_out_ffn_kernel, final),
        grid=(t // tm,),
        in_specs=[row(d), row(GLA_V),
                  pl.BlockSpec((MOBA_HEADS, tm // MOBA_BLOCK, MOBA_HD, MOBA_BLOCK), lambda i: (0, i, 0, 0)),
                  pl.BlockSpec((MOBA_HEADS, tm // MOBA_BLOCK, 8, MOBA_BLOCK), lambda i: (0, i, 0, 0)),
                  row(p.shape[1])]
        + [_const_spec(a.shape) for a in consts],
        out_specs=row(d),
        out_shape=jax.ShapeDtypeStruct((t, d), F32),
        compiler_params=pltpu.CompilerParams(dimension_semantics=("parallel",), vmem_limit_bytes=VMEM_LIMIT),
        name="out_ffn",
    )(h, og, om, ol, p, *consts)


def kernel(x, p, ffn1_norm, ffn1_w_gu, ffn1_w_down, mix_norm, w_in, w_gk_up, b_gk, gla_out_norm, w_out,
           ffn2_norm, ffn2_w_gu, ffn2_w_down, ple_norm, w_ple_gate, w_ple_proj, rel_bias, final_norm):
    batch, t, d = x.shape
    depth = p.shape[0]
    assert batch == 1 and t % ROW_TILE == 0 and ROW_TILE % (KV_GROUP * MOBA_BLOCK) == 0
    row = lambda a: a.reshape(1, -1)
    h = x[0]
    for i in range(depth):
        bias, wgu1, wd1, wa, wlr, wm = _prologue(rel_bias, ffn1_w_gu[i], ffn1_w_down[i], w_in[i])
        wgk = jnp.pad(w_gk_up[i], ((0, LANES - GLA_LOWRANK), (0, 0))).astype(BF16)
        h1, qg, kg, vg, gg, la, qt, ka, vt = _ffn_in(
            h, row(ffn1_norm[i]), wgu1, wd1, row(mix_norm[i]), wa, wlr, wm, wgk, row(b_gk[i]))
        og, (wo, wgu2, wd2, wpg) = _gla(qg, kg, la, vg, gg, row(gla_out_norm[i]),
                                        [w_out[i], ffn2_w_gu[i], ffn2_w_down[i], w_ple_gate[i]])
        om, ol = _moba_attn(qt, ka, vt, bias)
        h = _out_ffn(i == depth - 1, h1, og, om, ol, p[i, 0], wo, row(ffn2_norm[i]), wgu2, wd2, row(ple_norm[i]),
                     wpg, w_ple_proj[i].astype(BF16), row(final_norm))
    return h[None]
```

```python
import functools
import math

import jax
import jax.numpy as jnp
import numpy as np
from jax import lax
from jax.experimental import pallas as pl
from jax.experimental.pallas import tpu as pltpu

F32 = jnp.float32
BF16 = jnp.bfloat16

GLA_HEADS = 4
GLA_DK = 64
GLA_DV = 128
GLA_LOWRANK = 16
GLA_TAU = 16.0
MOBA_HEADS = 8
MOBA_HD = 64
MOBA_BLOCK = 256
MOBA_TOPK = 3
REL_BUCKETS = 32
REL_MAX_DIST = 4096
EPS = 1e-6
NEG = -1e30

GLA_QK = GLA_HEADS * GLA_DK
GLA_V = GLA_HEADS * GLA_DV
GLA_COLS = 2 * GLA_QK + 2 * GLA_V
MOBA_W = MOBA_HEADS * MOBA_HD

LANES = 128
BF16_SUBLANES = 16
VMEM_LIMIT = 56 * 1024 * 1024

ROW_TILE = 512
OUT_STEP_TILES = 2
GLA_TILE = 256
GLA_STEP_TILES = 4
GLA_LEVELS = tuple(GLA_TILE >> (i + 1) for i in range(GLA_TILE.bit_length() - 1))
_FAR_DIST = math.ceil((REL_BUCKETS // 2) * (REL_MAX_DIST / (REL_BUCKETS // 2))
                      ** ((REL_BUCKETS // 2 - 1) / (REL_BUCKETS // 2)))
_FAR_BLOCKS = -(-(_FAR_DIST - 1) // MOBA_BLOCK) + 1
BIAS_TILES = _FAR_BLOCKS + 1
KV_GROUP = 2
ATTN_UNROLL = 32
V_ROWS = 80
LOG2E = 1.4426950408889634


def _rms(x, g):
    return x * lax.rsqrt(jnp.mean(x * x, axis=-1, keepdims=True) + EPS) * g


def _ff_chunks(d_ff):
    step = 1024
    return [(s, min(step, d_ff - s)) for s in range(0, d_ff, step)]


def _swiglu(xn, wgu_ref, wd_ref):
    d_ff = wd_ref.shape[0]
    acc = None
    for start, size in _ff_chunks(d_ff):
        gate = jnp.dot(xn, wgu_ref[:, start:start + size], preferred_element_type=F32)
        up = jnp.dot(xn, wgu_ref[:, d_ff + start:d_ff + start + size], preferred_element_type=F32)
        act = (gate * jax.nn.sigmoid(gate) * up).astype(BF16)
        part = jnp.dot(act, wd_ref[start:start + size, :], preferred_element_type=F32)
        acc = part if acc is None else acc + part
    return acc


def _const_spec(shape):
    return pl.BlockSpec(shape, lambda *_: (0,) * len(shape))


def _moba_operands(pair, q, k, v, first_block, km_ref, qt_ref, ka_ref, vt_ref):
    blk = MOBA_BLOCK
    rows = q.shape[0]
    n_blocks = rows // blk
    hd = MOBA_HD
    means = jnp.mean(k.reshape(n_blocks, blk, LANES), axis=1)
    km = km_ref[pair]
    km_row = lax.broadcasted_iota(jnp.int32, km.shape, 0)
    for b in range(n_blocks):
        km = jnp.where(km_row == first_block + b, means[b:b + 1], km)
    km_ref[pair] = km
    km_lane = lax.broadcasted_iota(jnp.int32, km.shape, 1)
    q_t = q.T
    v_t = v.T
    block = lax.broadcasted_iota(jnp.int32, (hd, rows), 0)
    block_f = block.astype(F32)
    own = first_block + lax.broadcasted_iota(jnp.int32, (hd, rows), 1) // blk
    lane = lax.broadcasted_iota(jnp.int32, (rows, LANES), 1)
    own_k = first_block + lax.broadcasted_iota(jnp.int32, (rows, LANES), 0) // blk
    ones_rows = jnp.where(lax.broadcasted_iota(jnp.int32, (V_ROWS - hd, rows), 0) == 0, 1.0, 0.0)
    for a in range(2):
        head = 2 * pair + a
        gate = jnp.dot(jnp.where(km_lane // hd == a, km, 0.0), q_t, precision=lax.Precision.HIGHEST,
                       preferred_element_type=F32)
        gate = jnp.where(block < own, gate, -jnp.inf)
        chosen = jnp.where(block == own, 1.0, 0.0)
        for _ in range(MOBA_TOPK):
            top = jnp.max(gate, axis=0, keepdims=True)
            first = jnp.min(jnp.where(gate == top, block_f, float(hd)), axis=0, keepdims=True)
            hit = block_f == first
            chosen = jnp.where(hit, 1.0, chosen)
            gate = jnp.where(hit, -jnp.inf, gate)
        q_a = q_t[a * hd:(a + 1) * hd] * (hd ** -0.5 * LOG2E)
        q_aug_t = jnp.concatenate([q_a, chosen - 1.0] if a == 0 else [chosen - 1.0, q_a], axis=0).astype(BF16)
        v_aug_t = jnp.concatenate([v_t[a * hd:(a + 1) * hd], ones_rows], axis=0).astype(BF16)
        k_aug = jnp.where(lane // hd == a, k, jnp.where(lane - (1 - a) * hd == own_k, -NEG, 0.0)).astype(BF16)
        for b in range(n_blocks):
            qt_ref[head, b] = q_aug_t[:, b * blk:(b + 1) * blk]
            ka_ref[head, b] = k_aug[b * blk:(b + 1) * blk, :]
        group = KV_GROUP * blk
        for g in range(n_blocks // KV_GROUP):
            vt_ref[head, g] = v_aug_t[:, g * group:(g + 1) * group]


def _ffn_in_kernel(x_ref, n1_ref, wgu_ref, wd_ref, n2_ref, wa_ref, wlr_ref, wm_ref, wgk_ref, bgk_ref,
                   h_ref, qg_ref, kg_ref, vg_ref, gg_ref, la_ref, qt_ref, ka_ref, vt_ref, km_ref):
    @pl.when(pl.program_id(0) == 0)
    def _():
        km_ref[...] = jnp.zeros_like(km_ref)

    x = x_ref[...]
    h = x + 0.5 * _swiglu(_rms(x, n1_ref[...]).astype(BF16), wgu_ref, wd_ref)
    h_ref[...] = h
    n = _rms(h, n2_ref[...]).astype(BF16)
    ua = jnp.dot(n, wa_ref[...], preferred_element_type=F32)
    qg_ref[...] = ua[:, 0:GLA_QK]
    kg_ref[...] = ua[:, GLA_QK:2 * GLA_QK]
    vg_ref[...] = ua[:, 2 * GLA_QK:2 * GLA_QK + GLA_V]
    gg_ref[...] = ua[:, 2 * GLA_QK + GLA_V:]
    lr = jnp.dot(n, wlr_ref[...], preferred_element_type=F32)
    z = jnp.dot(lr.astype(BF16), wgk_ref[...], preferred_element_type=F32) + bgk_ref[...]
    la_ref[...] = (jnp.minimum(z, 0.0) - jnp.log1p(jnp.exp(-jnp.abs(z)))) * (1.0 / GLA_TAU)
    um = jnp.dot(n, wm_ref[...], preferred_element_type=F32)
    first_block = pl.program_id(0) * (x.shape[0] // MOBA_BLOCK)
    for pair in range(MOBA_HEADS // 2):
        q, k, v = (um[:, c * MOBA_W + pair * LANES:c * MOBA_W + (pair + 1) * LANES] for c in range(3))
        _moba_operands(pair, q, k, v, first_block, km_ref, qt_ref, ka_ref, vt_ref)


def _ffn_in(x, n1, wgu, wd, n2, wa, wlr, wm, wgk, bgk):
    t, d = x.shape
    tm = min(ROW_TILE, t)
    blk = MOBA_BLOCK
    nb = t // blk
    assert nb <= MOBA_HD
    step_blocks = tm // blk
    row = lambda w: pl.BlockSpec((tm, w), lambda i: (i, 0))
    outs = [(d, F32), (GLA_QK, F32), (GLA_QK, F32), (GLA_V, F32), (GLA_V, F32), (GLA_QK, F32)]
    heads = lambda shape: pl.BlockSpec((MOBA_HEADS,) + shape, lambda i: (0, i, 0, 0))
    return pl.pallas_call(
        _ffn_in_kernel,
        grid=(t // tm,),
        in_specs=[row(d)] + [_const_spec(a.shape) for a in (n1, wgu, wd, n2, wa, wlr, wm, wgk, bgk)],
        out_specs=[row(w) for w, _ in outs] + [heads((step_blocks, LANES, blk)), heads((step_blocks, blk, LANES)),
                                               heads((step_blocks // KV_GROUP, V_ROWS, KV_GROUP * blk))],
        out_shape=[jax.ShapeDtypeStruct((t, w), dt) for w, dt in outs]
        + [jax.ShapeDtypeStruct((MOBA_HEADS, nb, LANES, blk), BF16),
           jax.ShapeDtypeStruct((MOBA_HEADS, nb, blk, LANES), BF16),
           jax.ShapeDtypeStruct((MOBA_HEADS, nb // KV_GROUP, V_ROWS, KV_GROUP * blk), BF16)],
        scratch_shapes=[pltpu.VMEM((MOBA_HEADS // 2, MOBA_HD, LANES), F32)],
        compiler_params=pltpu.CompilerParams(dimension_semantics=("arbitrary",), vmem_limit_bytes=VMEM_LIMIT),
        name="ffn_in",
    )(x, n1, wgu, wd, n2, wa, wlr, wm, wgk, bgk)


def _gla_span_matrix(tile):
    i = np.arange(tile)[:, None]
    t = np.arange(tile)[None, :]
    mats = [t <= i, t > i]
    for s in GLA_LEVELS:
        split = (i // (2 * s)) * (2 * s) + s - 1
        mats.append(np.where((i & s) != 0, (t > split) & (t <= i), (t > i) & (t <= split)))
    return jnp.asarray(np.concatenate(mats, axis=0), BF16)


def _gla_kernel(n_cast, q_ref, k_ref, la_ref, v_ref, g_ref, gn_ref, span_ref, *refs):
    o_ref = refs[n_cast]
    st_ref = refs[2 * n_cast + 1]
    for w_ref, w_bf16_ref in zip(refs[:n_cast], refs[n_cast + 1:2 * n_cast + 1]):
        w_bf16_ref[...] = w_ref[...].astype(BF16)

    @pl.when(pl.program_id(0) == 0)
    def _():
        st_ref[...] = jnp.zeros_like(st_ref)

    states = [st_ref[h] for h in range(GLA_HEADS)]
    for r0 in range(0, q_ref.shape[0], GLA_TILE):
        states = _gla_tile(r0, states, q_ref, k_ref, la_ref, v_ref, g_ref, gn_ref, span_ref, o_ref)
    for h in range(GLA_HEADS):
        st_ref[h] = states[h]


def _gla_tile(r0, states, q_ref, k_ref, la_ref, v_ref, g_ref, gn_ref, span_ref, o_ref):
    tb = GLA_TILE
    rows = pl.ds(r0, tb)
    half = tb // 2
    la = la_ref[rows, :]
    la_hi = la.astype(BF16)
    la_lo = (la - la_hi.astype(F32)).astype(BF16)

    def span_sum(r):
        m = span_ref[r * tb:(r + 1) * tb, :]
        return jnp.dot(m, la_hi, preferred_element_type=F32) + jnp.dot(m, la_lo, preferred_element_type=F32)

    b = span_sum(0)
    q = q_ref[rows, :] * (GLA_DK ** -0.5)
    k = k_ref[rows, :]
    qe = q * jnp.exp(b)
    ke = k * jnp.exp(span_sum(1))
    a_last = jnp.exp(b[tb - 1:tb, :])
    row = lax.broadcasted_iota(jnp.int32, (tb, GLA_QK), 0)
    q_lvl, k_lvl = [], []
    for idx, s in enumerate(GLA_LEVELS):
        e = jnp.exp(span_sum(2 + idx))
        upper = (row & s) != 0
        q_lvl.append(jnp.where(upper, q * e, 0.0))
        k_lvl.append(jnp.where(upper, 0.0, k * e))
    ri = lax.broadcasted_iota(jnp.int32, (half, 2 * half), 0)
    ci = lax.broadcasted_iota(jnp.int32, (half, 2 * half), 1) % half
    same_block = [ri // (2 * s) == ci // (2 * s) for s in GLA_LEVELS[1:]]
    diagonal = ri == ci
    first_head = lax.broadcasted_iota(jnp.int32, (half, 2 * GLA_DK), 1) < GLA_DK
    gn = gn_ref[...]
    nt = (((1,), (1,)), ((), ()))

    def pair(x, y, r0, c0, lanes):
        keys = y[c0:c0 + half, lanes]
        keys = jnp.concatenate([jnp.where(first_head, keys, 0.0), jnp.where(first_head, 0.0, keys)], axis=0)
        return lax.dot_general(x[r0:r0 + half, lanes].astype(BF16), keys.astype(BF16), nt,
                               preferred_element_type=F32)

    intra = []
    for hp in range(GLA_HEADS // 2):
        lanes = slice(2 * hp * GLA_DK, 2 * (hp + 1) * GLA_DK)
        diag_blocks = []
        for c in (0, half):
            blk = jnp.where(diagonal, pair(q, k, c, c, lanes), 0.0)
            for idx in range(1, len(GLA_LEVELS)):
                blk = blk + jnp.where(same_block[idx - 1], pair(q_lvl[idx], k_lvl[idx], c, c, lanes), 0.0)
            diag_blocks.append(blk)
        low = pair(q_lvl[0], k_lvl[0], half, 0, lanes)
        for a in range(2):
            cols = slice(a * half, (a + 1) * half)
            intra.append(jnp.concatenate(
                [jnp.concatenate([diag_blocks[0][:, cols], jnp.zeros((half, half), F32)], axis=1),
                 jnp.concatenate([low[:, cols], diag_blocks[1][:, cols]], axis=1)], axis=0).astype(BF16))
    new_states = []
    for h in range(GLA_HEADS):
        ks = slice(h * GLA_DK, (h + 1) * GLA_DK)
        vs = slice(h * GLA_DV, (h + 1) * GLA_DV)
        attn = intra[h]
        v = v_ref[rows, vs].astype(BF16)
        st = states[h]
        o = jnp.dot(attn, v, preferred_element_type=F32)
        o = o + lax.dot_general(qe[:, ks].astype(BF16), st.astype(BF16), nt, preferred_element_type=F32)
        new_states.append(st * a_last[:, ks] + lax.dot_general(v, ke[:, ks].astype(BF16), (((0,), (0,)), ((), ())),
                                                               preferred_element_type=F32))
        g = g_ref[rows, vs]
        o_ref[rows, vs] = _rms(o, gn) * (g * jax.nn.sigmoid(g))
    return new_states


def _cast_spec(shape, steps):
    rows = shape[0]
    hold = next(d for d in range(1, steps + 1) if steps % d == 0 and rows * d % (steps * BF16_SUBLANES) == 0)
    return pl.BlockSpec((rows * hold // steps, shape[1]), lambda i: (i // hold, 0))


def _gla(qg, kg, la, vg, gg, gn, weights):
    t = qg.shape[0]
    tb = GLA_TILE * GLA_STEP_TILES
    assert t % tb == 0
    steps = t // tb
    span = _gla_span_matrix(GLA_TILE)
    row = lambda w: pl.BlockSpec((tb, w), lambda i: (i, 0))
    cast_specs = [_cast_spec(w.shape, steps) for w in weights]
    out = pl.pallas_call(
        functools.partial(_gla_kernel, len(weights)),
        grid=(steps,),
        in_specs=[row(GLA_QK), row(GLA_QK), row(GLA_QK), row(GLA_V), row(GLA_V), _const_spec(gn.shape),
                  _const_spec(span.shape)] + cast_specs,
        out_specs=[row(GLA_V)] + cast_specs,
        out_shape=[jax.ShapeDtypeStruct((t, GLA_V), F32)] + [jax.ShapeDtypeStruct(w.shape, BF16) for w in weights],
        scratch_shapes=[pltpu.VMEM((GLA_HEADS, GLA_DV, GLA_DK), F32)],
        compiler_params=pltpu.CompilerParams(dimension_semantics=("arbitrary",), vmem_limit_bytes=VMEM_LIMIT),
        name="gla",
    )(qg, kg, la, vg, gg, gn, span, *weights)
    return out[0], out[1:]


def _prologue_kernel(tab_ref, wgu_ref, wd_ref, win_ref, o_ref, wgu_out, wd_out, wa_out, wlr_out, wm_out):
    wgu_out[...] = wgu_ref[...].astype(BF16)
    wd_out[...] = wd_ref[...].astype(BF16)
    wa_out[...] = win_ref[:GLA_COLS, :].T.astype(BF16)
    wlr_out[...] = win_ref[GLA_COLS:GLA_COLS + LANES, :].T.astype(BF16)
    wm_out[...] = win_ref[GLA_COLS + GLA_LOWRANK:, :].T.astype(BF16)

    h = pl.program_id(0)
    blk = MOBA_BLOCK
    max_exact = REL_BUCKETS // 2
    x = lax.broadcasted_iota(jnp.int32, (8, 2 * blk), 1)
    table = jnp.broadcast_to(tab_ref[pl.ds(h, 1), :], (8, LANES))
    for delta in range(BIAS_TILES):
        dist = delta * blk + x - blk
        n = jnp.maximum(dist, 0)
        nf = jnp.maximum(n, 1).astype(F32)
        large = max_exact + (jnp.log(nf / max_exact) / math.log(REL_MAX_DIST / max_exact)
                             * (REL_BUCKETS - max_exact)).astype(jnp.int32)
        large = jnp.minimum(large, REL_BUCKETS - 1)
        bucket = jnp.where(n < max_exact, n, large)
        bias = jnp.concatenate([jnp.take_along_axis(table, bucket[:, c:c + LANES], axis=1)
                                for c in range(0, 2 * blk, LANES)], axis=1)
        by_dist = jnp.where(dist >= 0, bias * LOG2E, NEG)
        rows = jnp.broadcast_to(by_dist[0:1], (blk, 2 * blk))
        o_ref[0, delta] = pltpu.roll(rows, 0, 1, stride=1, stride_axis=0)[:, blk:].astype(o_ref.dtype)


def _prologue(rel_bias, wgu, wd, w_in):
    blk = MOBA_BLOCK
    steps = MOBA_HEADS
    table = jnp.pad(rel_bias.T, ((0, 0), (0, LANES - REL_BUCKETS)))
    d, width = w_in.shape
    moba_cols = width - GLA_COLS - GLA_LOWRANK
    slab = _cast_spec(w_in.shape, steps).block_shape[0]
    assert slab * steps == d and slab % LANES == 0
    w_slab = lambda cols: pl.BlockSpec((slab, cols), lambda i: (i, 0))
    w_in_t = w_in.T
    return pl.pallas_call(
        _prologue_kernel,
        grid=(steps,),
        in_specs=[_const_spec(table.shape), _cast_spec(wgu.shape, steps), _cast_spec(wd.shape, steps),
                  pl.BlockSpec((width, slab), lambda i: (0, i))],
        out_specs=[pl.BlockSpec((1, BIAS_TILES, blk, blk), lambda h: (h, 0, 0, 0)), _cast_spec(wgu.shape, steps),
                   _cast_spec(wd.shape, steps), w_slab(GLA_COLS), w_slab(LANES), w_slab(moba_cols)],
        out_shape=[jax.ShapeDtypeStruct((MOBA_HEADS, BIAS_TILES, blk, blk), BF16),
                   jax.ShapeDtypeStruct(wgu.shape, BF16), jax.ShapeDtypeStruct(wd.shape, BF16),
                   jax.ShapeDtypeStruct((d, GLA_COLS), BF16), jax.ShapeDtypeStruct((d, LANES), BF16),
                   jax.ShapeDtypeStruct((d, moba_cols), BF16)],
        compiler_params=pltpu.CompilerParams(dimension_semantics=("parallel",), vmem_limit_bytes=VMEM_LIMIT),
        name="prologue",
    )(table, wgu, wd, w_in_t)


def _moba_attn_kernel(qi_ref, gi_ref, qt_ref, ka_ref, vt_ref, bias_ref, o_ref, l_ref, s0_ref, s1_ref, p0_ref,
                      p1_ref, acc_ref):
    n_pairs = qi_ref.shape[0] - 2
    blk = MOBA_BLOCK
    s_refs = (s0_ref, s1_ref)
    p_refs = (p0_ref, p1_ref)

    def scores(t, slot):
        s_ref = s_refs[slot]
        qi = qi_ref[t]
        gi = gi_ref[t]
        qt = qt_ref[0, qi]
        col_max = None
        for b in range(KV_GROUP):
            j = gi * KV_GROUP + b
            s = jnp.dot(ka_ref[0, j], qt, preferred_element_type=F32).astype(BF16)
            s = s + bias_ref[0, jnp.clip(qi - j, 0, BIAS_TILES - 1)]
            s_ref[b] = s
            cm = jnp.max(s.reshape(blk // BF16_SUBLANES, BF16_SUBLANES, blk), axis=0)
            col_max = cm if col_max is None else jnp.maximum(col_max, cm)
        return jnp.max(col_max, axis=0, keepdims=True).astype(F32)

    def softmax(t, slot, m_run, col_max):
        m_prev = jnp.where(gi_ref[t] == 0, -jnp.inf, m_run)
        m = jnp.maximum(m_prev, col_max)
        for b in range(KV_GROUP):
            p_refs[slot][b * blk:(b + 1) * blk, :] = jnp.exp2(s_refs[slot][b] - m.astype(BF16))
        return m, jnp.exp2(m_prev - m)

    def values(t, slot, alpha):
        acc = acc_ref[...] * alpha + jnp.dot(vt_ref[0, gi_ref[t]], p_refs[slot][...], preferred_element_type=F32)
        acc_ref[...] = acc
        o_ref[0, qi_ref[t]] = acc[0:MOBA_HD].astype(o_ref.dtype)
        l_ref[0, qi_ref[t]] = acc[MOBA_HD:MOBA_HD + 8]

    acc_ref[...] = jnp.zeros_like(acc_ref)
    cm0 = scores(0, 0)
    cm1 = scores(1, 1)
    m0, alpha0 = softmax(0, 0, jnp.full((1, blk), -jnp.inf, F32), cm0)

    unroll = max(u for u in range(2, ATTN_UNROLL + 1, 2) if n_pairs % u == 0)

    def step(u, carry):
        m, alpha, cm = carry
        for i in range(unroll):
            t = unroll * u + i
            cm_next = scores(t + 2, i % 2)
            m, alpha_next = softmax(t + 1, 1 - i % 2, m, cm)
            values(t, i % 2, alpha)
            alpha, cm = alpha_next, cm_next
        return m, alpha, cm

    lax.fori_loop(0, n_pairs // unroll, step, (m0, alpha0, cm1))


def _moba_attn(qt, ka, vt, bias):
    nb = qt.shape[1]
    blk = MOBA_BLOCK
    pairs = [(qi, g) for qi in range(nb) for g in range(qi // KV_GROUP + 1)] + [(0, 0), (0, 0)]
    qi_tab = jnp.asarray([a for a, _ in pairs], jnp.int32)
    gi_tab = jnp.asarray([b for _, b in pairs], jnp.int32)
    head = lambda shape: pl.BlockSpec((1,) + shape, lambda h, *_: (h,) + (0,) * len(shape))
    return pl.pallas_call(
        _moba_attn_kernel,
        grid_spec=pltpu.PrefetchScalarGridSpec(
            num_scalar_prefetch=2,
            grid=(MOBA_HEADS,),
            in_specs=[head((nb, LANES, blk)), head((nb, blk, LANES)),
                      head((nb // KV_GROUP, V_ROWS, KV_GROUP * blk)), head((BIAS_TILES, blk, blk))],
            out_specs=[head((nb, MOBA_HD, blk)), head((nb, 8, blk))],
            scratch_shapes=[pltpu.VMEM((KV_GROUP, blk, blk), BF16), pltpu.VMEM((KV_GROUP, blk, blk), BF16),
                            pltpu.VMEM((KV_GROUP * blk, blk), BF16), pltpu.VMEM((KV_GROUP * blk, blk), BF16),
                            pltpu.VMEM((V_ROWS, blk), F32)]),
        out_shape=[jax.ShapeDtypeStruct((MOBA_HEADS, nb, MOBA_HD, blk), BF16),
                   jax.ShapeDtypeStruct((MOBA_HEADS, nb, 8, blk), F32)],
        compiler_params=pltpu.CompilerParams(dimension_semantics=("parallel",), vmem_limit_bytes=VMEM_LIMIT),
        name="moba_attn",
    )(qi_tab, gi_tab, qt, ka, vt, bias)


def _out_ffn_kernel(final, h_ref, og_ref, om_ref, ol_ref, p_ref, wo_ref, n3_ref, wgu_ref, wd_ref,
                    n4_ref, wpg_ref, wpp_ref, nf_ref, o_ref):
    for r0 in range(0, h_ref.shape[0], ROW_TILE):
        rows = pl.ds(r0, ROW_TILE)
        mix = jnp.dot(og_ref[rows, :].astype(BF16), wo_ref[0:GLA_V, :], preferred_element_type=F32)
        b0 = r0 // MOBA_BLOCK
        om = [(om_ref[:, b].astype(F32) / ol_ref[:, b, 0:1, :]).reshape(MOBA_W, MOBA_BLOCK).T.astype(BF16)
              for b in range(b0, b0 + ROW_TILE // MOBA_BLOCK)]
        mix = mix + jnp.dot(jnp.concatenate(om, axis=0), wo_ref[GLA_V:, :], preferred_element_type=F32)
        h = h_ref[rows, :] + mix
        h = h + 0.5 * _swiglu(_rms(h, n3_ref[...]).astype(BF16), wgu_ref, wd_ref)
        gate = jax.nn.sigmoid(jnp.dot(_rms(h, n4_ref[...]).astype(BF16), wpg_ref[...], preferred_element_type=F32))
        h = h + gate * jnp.dot(p_ref[rows, :].astype(BF16), wpp_ref[...], preferred_element_type=F32)
        o_ref[rows, :] = _rms(h, nf_ref[...]) if final else h


def _out_ffn(final, h, og, om, ol, p, wo, n3, wgu, wd, n4, wpg, wpp, nf):
    t, d = h.shape
    tm = min(ROW_TILE * OUT_STEP_TILES, t)
    assert t % tm == 0
    row = lambda w: pl.BlockSpec((tm, w), lambda i: (i, 0))
    consts = (wo, n3, wgu, wd, n4, wpg, wpp, nf)
    return pl.pallas_call(
        functools.partial(_out_ffn_kernel, final),
        grid=(t // tm,),
        in_specs=[row(d), row(GLA_V),
                  pl.BlockSpec((MOBA_HEADS, tm // MOBA_BLOCK, MOBA_HD, MOBA_BLOCK), lambda i: (0, i, 0, 0)),
                  pl.BlockSpec((MOBA_HEADS, tm // MOBA_BLOCK, 8, MOBA_BLOCK), lambda i: (0, i, 0, 0)),
                  row(p.shape[1])]
        + [_const_spec(a.shape) for a in consts],
        out_specs=row(d),
        out_shape=jax.ShapeDtypeStruct((t, d), F32),
        compiler_params=pltpu.CompilerParams(dimension_semantics=("parallel",), vmem_limit_bytes=VMEM_LIMIT),
        name="out_ffn",
    )(h, og, om, ol, p, *consts)


def kernel(x, p, ffn1_norm, ffn1_w_gu, ffn1_w_down, mix_norm, w_in, w_gk_up, b_gk, gla_out_norm, w_out,
           ffn2_norm, ffn2_w_gu, ffn2_w_down, ple_norm, w_ple_gate, w_ple_proj, rel_bias, final_norm):
    batch, t, d = x.shape
    depth = p.shape[0]
    assert batch == 1 and t % ROW_TILE == 0 and ROW_TILE % (KV_GROUP * MOBA_BLOCK) == 0
    row = lambda a: a.reshape(1, -1)
    h = x[0]
    for i in range(depth):
        bias, wgu1, wd1, wa, wlr, wm = _prologue(rel_bias, ffn1_w_gu[i], ffn1_w_down[i], w_in[i])
        wgk = jnp.pad(w_gk_up[i], ((0, LANES - GLA_LOWRANK), (0, 0))).astype(BF16)
        h1, qg, kg, vg, gg, la, qt, ka, vt = _ffn_in(
            h, row(ffn1_norm[i]), wgu1, wd1, row(mix_norm[i]), wa, wlr, wm, wgk, row(b_gk[i]))
        og, (wo, wgu2, wd2, wpg) = _gla(qg, kg, la, vg, gg, row(gla_out_norm[i]),
                                        [w_out[i], ffn2_w_gu[i], ffn2_w_down[i], w_ple_gate[i]])
        om, ol = _moba_attn(qt, ka, vt, bias)
        h = _out_ffn(i == depth - 1, h1, og, om, ol, p[i, 0], wo, row(ffn2_norm[i]), wgu2, wd2, row(ple_norm[i]),
                     wpg, w_ple_proj[i].astype(BF16), row(final_norm))
    return h[None]
```

```python
import functools
import math

import jax
import jax.numpy as jnp
import numpy as np
from jax import lax
from jax.experimental import pallas as pl
from jax.experimental.pallas import tpu as pltpu

F32 = jnp.float32
BF16 = jnp.bfloat16

GLA_HEADS = 4
GLA_DK = 64
GLA_DV = 128
GLA_LOWRANK = 16
GLA_TAU = 16.0
MOBA_HEADS = 8
MOBA_HD = 64
MOBA_BLOCK = 256
MOBA_TOPK = 3
REL_BUCKETS = 32
REL_MAX_DIST = 4096
EPS = 1e-6
NEG = -1e30

GLA_QK = GLA_HEADS * GLA_DK
GLA_V = GLA_HEADS * GLA_DV
GLA_COLS = 2 * GLA_QK + 2 * GLA_V
MOBA_W = MOBA_HEADS * MOBA_HD

LANES = 128
BF16_SUBLANES = 16
VMEM_LIMIT = 56 * 1024 * 1024

ROW_TILE = 512
GLA_TILE = 256
GLA_STEP_TILES = 4
GLA_LEVELS = tuple(GLA_TILE >> (i + 1) for i in range(GLA_TILE.bit_length() - 1))
_FAR_DIST = math.ceil((REL_BUCKETS // 2) * (REL_MAX_DIST / (REL_BUCKETS // 2))
                      ** ((REL_BUCKETS // 2 - 1) / (REL_BUCKETS // 2)))
_FAR_BLOCKS = -(-(_FAR_DIST - 1) // MOBA_BLOCK) + 1
BIAS_TILES = _FAR_BLOCKS + 1
KV_GROUP = 2
ATTN_UNROLL = 96
V_ROWS = 80
LOG2E = 1.4426950408889634


def _rms(x, g):
    return x * lax.rsqrt(jnp.mean(x * x, axis=-1, keepdims=True) + EPS) * g


def _ff_chunks(d_ff):
    step = 1024
    return [(s, min(step, d_ff - s)) for s in range(0, d_ff, step)]


def _swiglu(xn, wgu_ref, wd_ref):
    d_ff = wd_ref.shape[0]
    acc = None
    for start, size in _ff_chunks(d_ff):
        gate = jnp.dot(xn, wgu_ref[:, start:start + size], preferred_element_type=F32)
        up = jnp.dot(xn, wgu_ref[:, d_ff + start:d_ff + start + size], preferred_element_type=F32)
        act = (gate * jax.nn.sigmoid(gate) * up).astype(BF16)
        part = jnp.dot(act, wd_ref[start:start + size, :], preferred_element_type=F32)
        acc = part if acc is None else acc + part
    return acc


def _const_spec(shape):
    return pl.BlockSpec(shape, lambda *_: (0,) * len(shape))


def _moba_operands(pair, q, k, v, first_block, km_ref, qt_ref, ka_ref, vt_ref):
    blk = MOBA_BLOCK
    rows = q.shape[0]
    n_blocks = rows // blk
    hd = MOBA_HD
    means = jnp.mean(k.reshape(n_blocks, blk, LANES), axis=1)
    km = km_ref[pair]
    km_row = lax.broadcasted_iota(jnp.int32, km.shape, 0)
    for b in range(n_blocks):
        km = jnp.where(km_row == first_block + b, means[b:b + 1], km)
    km_ref[pair] = km
    km_lane = lax.broadcasted_iota(jnp.int32, km.shape, 1)
    q_t = q.T
    v_t = v.T
    block = lax.broadcasted_iota(jnp.int32, (hd, rows), 0)
    block_f = block.astype(F32)
    own = first_block + lax.broadcasted_iota(jnp.int32, (hd, rows), 1) // blk
    lane = lax.broadcasted_iota(jnp.int32, (rows, LANES), 1)
    own_k = first_block + lax.broadcasted_iota(jnp.int32, (rows, LANES), 0) // blk
    ones_rows = jnp.where(lax.broadcasted_iota(jnp.int32, (V_ROWS - hd, rows), 0) == 0, 1.0, 0.0)
    for a in range(2):
        head = 2 * pair + a
        gate = jnp.dot(jnp.where(km_lane // hd == a, km, 0.0), q_t, precision=lax.Precision.HIGHEST,
                       preferred_element_type=F32)
        gate = jnp.where(block < own, gate, -jnp.inf)
        chosen = jnp.where(block == own, 1.0, 0.0)
        for _ in range(MOBA_TOPK):
            top = jnp.max(gate, axis=0, keepdims=True)
            first = jnp.min(jnp.where(gate == top, block_f, float(hd)), axis=0, keepdims=True)
            hit = block_f == first
            chosen = jnp.where(hit, 1.0, chosen)
            gate = jnp.where(hit, -jnp.inf, gate)
        q_a = q_t[a * hd:(a + 1) * hd] * (hd ** -0.5 * LOG2E)
        q_aug_t = jnp.concatenate([q_a, chosen - 1.0] if a == 0 else [chosen - 1.0, q_a], axis=0).astype(BF16)
        v_aug_t = jnp.concatenate([v_t[a * hd:(a + 1) * hd], ones_rows], axis=0).astype(BF16)
        k_aug = jnp.where(lane // hd == a, k, jnp.where(lane - (1 - a) * hd == own_k, -NEG, 0.0)).astype(BF16)
        for b in range(n_blocks):
            qt_ref[head, b] = q_aug_t[:, b * blk:(b + 1) * blk]
            ka_ref[head, b] = k_aug[b * blk:(b + 1) * blk, :]
        group = KV_GROUP * blk
        for g in range(n_blocks // KV_GROUP):
            vt_ref[head, g] = v_aug_t[:, g * group:(g + 1) * group]


def _ffn_in_kernel(x_ref, n1_ref, wgu_ref, wd_ref, n2_ref, wa_ref, wlr_ref, wm_ref, wgk_ref, bgk_ref,
                   h_ref, qg_ref, kg_ref, vg_ref, gg_ref, la_ref, qt_ref, ka_ref, vt_ref, km_ref):
    @pl.when(pl.program_id(0) == 0)
    def _():
        km_ref[...] = jnp.zeros_like(km_ref)

    x = x_ref[...]
    h = x + 0.5 * _swiglu(_rms(x, n1_ref[...]).astype(BF16), wgu_ref, wd_ref)
    h_ref[...] = h
    n = _rms(h, n2_ref[...]).astype(BF16)
    ua = jnp.dot(n, wa_ref[...], preferred_element_type=F32)
    qg_ref[...] = ua[:, 0:GLA_QK]
    kg_ref[...] = ua[:, GLA_QK:2 * GLA_QK]
    vg_ref[...] = ua[:, 2 * GLA_QK:2 * GLA_QK + GLA_V]
    gg_ref[...] = ua[:, 2 * GLA_QK + GLA_V:]
    lr = jnp.dot(n, wlr_ref[...], preferred_element_type=F32)
    z = jnp.dot(lr.astype(BF16), wgk_ref[...], preferred_element_type=F32) + bgk_ref[...]
    la_ref[...] = (jnp.minimum(z, 0.0) - jnp.log1p(jnp.exp(-jnp.abs(z)))) * (1.0 / GLA_TAU)
    um = jnp.dot(n, wm_ref[...], preferred_element_type=F32)
    first_block = pl.program_id(0) * (x.shape[0] // MOBA_BLOCK)
    for pair in range(MOBA_HEADS // 2):
        q, k, v = (um[:, c * MOBA_W + pair * LANES:c * MOBA_W + (pair + 1) * LANES] for c in range(3))
        _moba_operands(pair, q, k, v, first_block, km_ref, qt_ref, ka_ref, vt_ref)


def _ffn_in(x, n1, wgu, wd, n2, wa, wlr, wm, wgk, bgk):
    t, d = x.shape
    tm = min(ROW_TILE, t)
    blk = MOBA_BLOCK
    nb = t // blk
    assert nb <= MOBA_HD
    step_blocks = tm // blk
    row = lambda w: pl.BlockSpec((tm, w), lambda i: (i, 0))
    outs = [(d, F32), (GLA_QK, F32), (GLA_QK, F32), (GLA_V, F32), (GLA_V, F32), (GLA_QK, F32)]
    heads = lambda shape: pl.BlockSpec((MOBA_HEADS,) + shape, lambda i: (0, i, 0, 0))
    return pl.pallas_call(
        _ffn_in_kernel,
        grid=(t // tm,),
        in_specs=[row(d)] + [_const_spec(a.shape) for a in (n1, wgu, wd, n2, wa, wlr, wm, wgk, bgk)],
        out_specs=[row(w) for w, _ in outs] + [heads((step_blocks, LANES, blk)), heads((step_blocks, blk, LANES)),
                                               heads((step_blocks // KV_GROUP, V_ROWS, KV_GROUP * blk))],
        out_shape=[jax.ShapeDtypeStruct((t, w), dt) for w, dt in outs]
        + [jax.ShapeDtypeStruct((MOBA_HEADS, nb, LANES, blk), BF16),
           jax.ShapeDtypeStruct((MOBA_HEADS, nb, blk, LANES), BF16),
           jax.ShapeDtypeStruct((MOBA_HEADS, nb // KV_GROUP, V_ROWS, KV_GROUP * blk), BF16)],
        scratch_shapes=[pltpu.VMEM((MOBA_HEADS // 2, MOBA_HD, LANES), F32)],
        compiler_params=pltpu.CompilerParams(dimension_semantics=("arbitrary",), vmem_limit_bytes=VMEM_LIMIT),
        name="ffn_in",
    )(x, n1, wgu, wd, n2, wa, wlr, wm, wgk, bgk)


def _gla_span_matrix(tile):
    i = np.arange(tile)[:, None]
    t = np.arange(tile)[None, :]
    mats = [t <= i, t > i]
    for s in GLA_LEVELS:
        split = (i // (2 * s)) * (2 * s) + s - 1
        mats.append(np.where((i & s) != 0, (t > split) & (t <= i), (t > i) & (t <= split)))
    return jnp.asarray(np.concatenate(mats, axis=0), BF16)


def _gla_kernel(n_cast, q_ref, k_ref, la_ref, v_ref, g_ref, gn_ref, span_ref, *refs):
    o_ref = refs[n_cast]
    st_ref = refs[2 * n_cast + 1]
    for w_ref, w_bf16_ref in zip(refs[:n_cast], refs[n_cast + 1:2 * n_cast + 1]):
        w_bf16_ref[...] = w_ref[...].astype(BF16)

    @pl.when(pl.program_id(0) == 0)
    def _():
        st_ref[...] = jnp.zeros_like(st_ref)

    states = [st_ref[h] for h in range(GLA_HEADS)]
    for r0 in range(0, q_ref.shape[0], GLA_TILE):
        states = _gla_tile(r0, states, q_ref, k_ref, la_ref, v_ref, g_ref, gn_ref, span_ref, o_ref)
    for h in range(GLA_HEADS):
        st_ref[h] = states[h]


def _gla_tile(r0, states, q_ref, k_ref, la_ref, v_ref, g_ref, gn_ref, span_ref, o_ref):
    tb = GLA_TILE
    rows = pl.ds(r0, tb)
    half = tb // 2
    la = la_ref[rows, :]
    la_hi = la.astype(BF16)
    la_lo = (la - la_hi.astype(F32)).astype(BF16)

    def span_sum(r):
        m = span_ref[r * tb:(r + 1) * tb, :]
        return jnp.dot(m, la_hi, preferred_element_type=F32) + jnp.dot(m, la_lo, preferred_element_type=F32)

    b = span_sum(0)
    q = q_ref[rows, :] * (GLA_DK ** -0.5)
    k = k_ref[rows, :]
    qe = q * jnp.exp(b)
    ke = k * jnp.exp(span_sum(1))
    a_last = jnp.exp(b[tb - 1:tb, :])
    row = lax.broadcasted_iota(jnp.int32, (tb, GLA_QK), 0)
    q_lvl, k_lvl = [], []
    for idx, s in enumerate(GLA_LEVELS):
        e = jnp.exp(span_sum(2 + idx))
        upper = (row & s) != 0
        q_lvl.append(jnp.where(upper, q * e, 0.0))
        k_lvl.append(jnp.where(upper, 0.0, k * e))
    ri = lax.broadcasted_iota(jnp.int32, (half, 2 * half), 0)
    ci = lax.broadcasted_iota(jnp.int32, (half, 2 * half), 1) % half
    same_block = [ri // (2 * s) == ci // (2 * s) for s in GLA_LEVELS[1:]]
    diagonal = ri == ci
    first_head = lax.broadcasted_iota(jnp.int32, (half, 2 * GLA_DK), 1) < GLA_DK
    gn = gn_ref[...]
    nt = (((1,), (1,)), ((), ()))

    def pair(x, y, r0, c0, lanes):
        keys = y[c0:c0 + half, lanes]
        keys = jnp.concatenate([jnp.where(first_head, keys, 0.0), jnp.where(first_head, 0.0, keys)], axis=0)
        return lax.dot_general(x[r0:r0 + half, lanes].astype(BF16), keys.astype(BF16), nt,
                               preferred_element_type=F32)

    intra = []
    for hp in range(GLA_HEADS // 2):
        lanes = slice(2 * hp * GLA_DK, 2 * (hp + 1) * GLA_DK)
        diag_blocks = []
        for c in (0, half):
            blk = jnp.where(diagonal, pair(q, k, c, c, lanes), 0.0)
            for idx in range(1, len(GLA_LEVELS)):
                blk = blk + jnp.where(same_block[idx - 1], pair(q_lvl[idx], k_lvl[idx], c, c, lanes), 0.0)
            diag_blocks.append(blk)
        low = pair(q_lvl[0], k_lvl[0], half, 0, lanes)
        for a in range(2):
            cols = slice(a * half, (a + 1) * half)
            intra.append(jnp.concatenate(
                [jnp.concatenate([diag_blocks[0][:, cols], jnp.zeros((half, half), F32)], axis=1),
                 jnp.concatenate([low[:, cols], diag_blocks[1][:, cols]], axis=1)], axis=0).astype(BF16))
    new_states = []
    for h in range(GLA_HEADS):
        ks = slice(h * GLA_DK, (h + 1) * GLA_DK)
        vs = slice(h * GLA_DV, (h + 1) * GLA_DV)
        attn = intra[h]
        v = v_ref[rows, vs].astype(BF16)
        st = states[h]
        o = jnp.dot(attn, v, preferred_element_type=F32)
        o = o + lax.dot_general(qe[:, ks].astype(BF16), st.astype(BF16), nt, preferred_element_type=F32)
        new_states.append(st * a_last[:, ks] + lax.dot_general(v, ke[:, ks].astype(BF16), (((0,), (0,)), ((), ())),
                                                               preferred_element_type=F32))
        g = g_ref[rows, vs]
        o_ref[rows, vs] = _rms(o, gn) * (g * jax.nn.sigmoid(g))
    return new_states


def _cast_spec(shape, steps):
    rows = shape[0]
    hold = next(d for d in range(1, steps + 1) if steps % d == 0 and rows * d % (steps * BF16_SUBLANES) == 0)
    return pl.BlockSpec((rows * hold // steps, shape[1]), lambda i: (i // hold, 0))


def _gla(qg, kg, la, vg, gg, gn, weights):
    t = qg.shape[0]
    tb = GLA_TILE * GLA_STEP_TILES
    assert t % tb == 0
    steps = t // tb
    span = _gla_span_matrix(GLA_TILE)
    row = lambda w: pl.BlockSpec((tb, w), lambda i: (i, 0))
    cast_specs = [_cast_spec(w.shape, steps) for w in weights]
    out = pl.pallas_call(
        functools.partial(_gla_kernel, len(weights)),
        grid=(steps,),
        in_specs=[row(GLA_QK), row(GLA_QK), row(GLA_QK), row(GLA_V), row(GLA_V), _const_spec(gn.shape),
                  _const_spec(span.shape)] + cast_specs,
        out_specs=[row(GLA_V)] + cast_specs,
        out_shape=[jax.ShapeDtypeStruct((t, GLA_V), F32)] + [jax.ShapeDtypeStruct(w.shape, BF16) for w in weights],
        scratch_shapes=[pltpu.VMEM((GLA_HEADS, GLA_DV, GLA_DK), F32)],
        compiler_params=pltpu.CompilerParams(dimension_semantics=("arbitrary",), vmem_limit_bytes=VMEM_LIMIT),
        name="gla",
    )(qg, kg, la, vg, gg, gn, span, *weights)
    return out[0], out[1:]


def _prologue_kernel(tab_ref, wgu_ref, wd_ref, win_ref, o_ref, wgu_out, wd_out, wa_out, wlr_out, wm_out):
    wgu_out[...] = wgu_ref[...].astype(BF16)
    wd_out[...] = wd_ref[...].astype(BF16)
    wa_out[...] = win_ref[:GLA_COLS, :].T.astype(BF16)
    wlr_out[...] = win_ref[GLA_COLS:GLA_COLS + LANES, :].T.astype(BF16)
    wm_out[...] = win_ref[GLA_COLS + GLA_LOWRANK:, :].T.astype(BF16)

    h = pl.program_id(0)
    blk = MOBA_BLOCK
    max_exact = REL_BUCKETS // 2
    x = lax.broadcasted_iota(jnp.int32, (8, 2 * blk), 1)
    table = jnp.broadcast_to(tab_ref[pl.ds(h, 1), :], (8, LANES))
    for delta in range(BIAS_TILES):
        dist = delta * blk + x - blk
        n = jnp.maximum(dist, 0)
        nf = jnp.maximum(n, 1).astype(F32)
        large = max_exact + (jnp.log(nf / max_exact) / math.log(REL_MAX_DIST / max_exact)
                             * (REL_BUCKETS - max_exact)).astype(jnp.int32)
        large = jnp.minimum(large, REL_BUCKETS - 1)
        bucket = jnp.where(n < max_exact, n, large)
        bias = jnp.concatenate([jnp.take_along_axis(table, bucket[:, c:c + LANES], axis=1)
                                for c in range(0, 2 * blk, LANES)], axis=1)
        by_dist = jnp.where(dist >= 0, bias * LOG2E, NEG)
        rows = jnp.broadcast_to(by_dist[0:1], (blk, 2 * blk))
        o_ref[0, delta] = pltpu.roll(rows, 0, 1, stride=1, stride_axis=0)[:, blk:].astype(o_ref.dtype)


def _prologue(rel_bias, wgu, wd, w_in):
    blk = MOBA_BLOCK
    steps = MOBA_HEADS
    table = jnp.pad(rel_bias.T, ((0, 0), (0, LANES - REL_BUCKETS)))
    d, width = w_in.shape
    moba_cols = width - GLA_COLS - GLA_LOWRANK
    slab = _cast_spec(w_in.shape, steps).block_shape[0]
    assert slab * steps == d and slab % LANES == 0
    w_slab = lambda cols: pl.BlockSpec((slab, cols), lambda i: (i, 0))
    w_in_t = w_in.T
    return pl.pallas_call(
        _prologue_kernel,
        grid=(steps,),
        in_specs=[_const_spec(table.shape), _cast_spec(wgu.shape, steps), _cast_spec(wd.shape, steps),
                  pl.BlockSpec((width, slab), lambda i: (0, i))],
        out_specs=[pl.BlockSpec((1, BIAS_TILES, blk, blk), lambda h: (h, 0, 0, 0)), _cast_spec(wgu.shape, steps),
                   _cast_spec(wd.shape, steps), w_slab(GLA_COLS), w_slab(LANES), w_slab(moba_cols)],
        out_shape=[jax.ShapeDtypeStruct((MOBA_HEADS, BIAS_TILES, blk, blk), BF16),
                   jax.ShapeDtypeStruct(wgu.shape, BF16), jax.ShapeDtypeStruct(wd.shape, BF16),
                   jax.ShapeDtypeStruct((d, GLA_COLS), BF16), jax.ShapeDtypeStruct((d, LANES), BF16),
                   jax.ShapeDtypeStruct((d, moba_cols), BF16)],
        compiler_params=pltpu.CompilerParams(dimension_semantics=("parallel",), vmem_limit_bytes=VMEM_LIMIT),
        name="prologue",
    )(table, wgu, wd, w_in_t)


def _moba_attn_kernel(qi_ref, gi_ref, qt_ref, ka_ref, vt_ref, bias_ref, o_ref, l_ref, s0_ref, s1_ref, p0_ref,
                      p1_ref, acc_ref):
    n_pairs = qi_ref.shape[0] - 2
    blk = MOBA_BLOCK
    s_refs = (s0_ref, s1_ref)
    p_refs = (p0_ref, p1_ref)

    def scores(t, slot):
        s_ref = s_refs[slot]
        qi = qi_ref[t]
        gi = gi_ref[t]
        qt = qt_ref[0, qi]
        col_max = None
        for b in range(KV_GROUP):
            j = gi * KV_GROUP + b
            s = jnp.dot(ka_ref[0, j], qt, preferred_element_type=F32).astype(BF16)
            s = s + bias_ref[0, jnp.clip(qi - j, 0, BIAS_TILES - 1)]
            s_ref[b] = s
            cm = jnp.max(s.reshape(blk // BF16_SUBLANES, BF16_SUBLANES, blk), axis=0)
            col_max = cm if col_max is None else jnp.maximum(col_max, cm)
        return jnp.max(col_max, axis=0, keepdims=True).astype(F32)

    def softmax(t, slot, m_run, col_max):
        m_prev = jnp.where(gi_ref[t] == 0, -jnp.inf, m_run)
        m = jnp.maximum(m_prev, col_max)
        for b in range(KV_GROUP):
            p_refs[slot][b * blk:(b + 1) * blk, :] = jnp.exp2(s_refs[slot][b] - m.astype(BF16))
        return m, jnp.exp2(m_prev - m)

    def values(t, slot, alpha):
        acc = acc_ref[...] * alpha + jnp.dot(vt_ref[0, gi_ref[t]], p_refs[slot][...], preferred_element_type=F32)
        acc_ref[...] = acc
        o_ref[0, qi_ref[t]] = acc[0:MOBA_HD].astype(o_ref.dtype)
        l_ref[0, qi_ref[t]] = acc[MOBA_HD:MOBA_HD + 8]

    acc_ref[...] = jnp.zeros_like(acc_ref)
    cm0 = scores(0, 0)
    cm1 = scores(1, 1)
    m0, alpha0 = softmax(0, 0, jnp.full((1, blk), -jnp.inf, F32), cm0)

    unroll = max(u for u in range(2, ATTN_UNROLL + 1, 2) if n_pairs % u == 0)

    def step(u, carry):
        m, alpha, cm = carry
        for i in range(unroll):
            t = unroll * u + i
            cm_next = scores(t + 2, i % 2)
            m, alpha_next = softmax(t + 1, 1 - i % 2, m, cm)
            values(t, i % 2, alpha)
            alpha, cm = alpha_next, cm_next
        return m, alpha, cm

    lax.fori_loop(0, n_pairs // unroll, step, (m0, alpha0, cm1))


def _moba_attn(qt, ka, vt, bias):
    nb = qt.shape[1]
    blk = MOBA_BLOCK
    pairs = [(qi, g) for qi in range(nb) for g in range(qi // KV_GROUP + 1)] + [(0, 0), (0, 0)]
    qi_tab = jnp.asarray([a for a, _ in pairs], jnp.int32)
    gi_tab = jnp.asarray([b for _, b in pairs], jnp.int32)
    head = lambda shape: pl.BlockSpec((1,) + shape, lambda h, *_: (h,) + (0,) * len(shape))
    return pl.pallas_call(
        _moba_attn_kernel,
        grid_spec=pltpu.PrefetchScalarGridSpec(
            num_scalar_prefetch=2,
            grid=(MOBA_HEADS,),
            in_specs=[head((nb, LANES, blk)), head((nb, blk, LANES)),
                      head((nb // KV_GROUP, V_ROWS, KV_GROUP * blk)), head((BIAS_TILES, blk, blk))],
            out_specs=[head((nb, MOBA_HD, blk)), head((nb, 8, blk))],
            scratch_shapes=[pltpu.VMEM((KV_GROUP, blk, blk), BF16), pltpu.VMEM((KV_GROUP, blk, blk), BF16),
                            pltpu.VMEM((KV_GROUP * blk, blk), BF16), pltpu.VMEM((KV_GROUP * blk, blk), BF16),
                            pltpu.VMEM((V_ROWS, blk), F32)]),
        out_shape=[jax.ShapeDtypeStruct((MOBA_HEADS, nb, MOBA_HD, blk), BF16),
                   jax.ShapeDtypeStruct((MOBA_HEADS, nb, 8, blk), F32)],
        compiler_params=pltpu.CompilerParams(dimension_semantics=("parallel",), vmem_limit_bytes=VMEM_LIMIT),
        name="moba_attn",
    )(qi_tab, gi_tab, qt, ka, vt, bias)


def _out_ffn_kernel(final, h_ref, og_ref, om_ref, ol_ref, p_ref, wo_ref, n3_ref, wgu_ref, wd_ref,
                    n4_ref, wpg_ref, wpp_ref, nf_ref, o_ref):
    mix = jnp.dot(og_ref[...].astype(BF16), wo_ref[0:GLA_V, :], preferred_element_type=F32)
    om = [(om_ref[:, b].astype(F32) / ol_ref[:, b, 0:1, :]).reshape(MOBA_W, MOBA_BLOCK).T.astype(BF16)
          for b in range(om_ref.shape[1])]
    mix = mix + jnp.dot(jnp.concatenate(om, axis=0), wo_ref[GLA_V:, :], preferred_element_type=F32)
    h = h_ref[...] + mix
    h = h + 0.5 * _swiglu(_rms(h, n3_ref[...]).astype(BF16), wgu_ref, wd_ref)
    gate = jax.nn.sigmoid(jnp.dot(_rms(h, n4_ref[...]).astype(BF16), wpg_ref[...], preferred_element_type=F32))
    h = h + gate * jnp.dot(p_ref[...].astype(BF16), wpp_ref[...], preferred_element_type=F32)
    o_ref[...] = _rms(h, nf_ref[...]) if final else h


def _out_ffn(final, h, og, om, ol, p, wo, n3, wgu, wd, n4, wpg, wpp, nf):
    t, d = h.shape
    tm = min(ROW_TILE, t)
    row = lambda w: pl.BlockSpec((tm, w), lambda i: (i, 0))
    consts = (wo, n3, wgu, wd, n4, wpg, wpp, nf)
    return pl.pallas_call(
        functools.partial(_out_ffn_kernel, final),
        grid=(t // tm,),
        in_specs=[row(d), row(GLA_V),
                  pl.BlockSpec((MOBA_HEADS, tm // MOBA_BLOCK, MOBA_HD, MOBA_BLOCK), lambda i: (0, i, 0, 0)),
                  pl.BlockSpec((MOBA_HEADS, tm // MOBA_BLOCK, 8, MOBA_BLOCK), lambda i: (0, i, 0, 0)),
                  row(p.shape[1])]
        + [_const_spec(a.shape) for a in consts],
        out_specs=row(d),
        out_shape=jax.ShapeDtypeStruct((t, d), F32),
        compiler_params=pltpu.CompilerParams(dimension_semantics=("parallel",), vmem_limit_bytes=VMEM_LIMIT),
        name="out_ffn",
    )(h, og, om, ol, p, *consts)


def kernel(x, p, ffn1_norm, ffn1_w_gu, ffn1_w_down, mix_norm, w_in, w_gk_up, b_gk, gla_out_norm, w_out,
           ffn2_norm, ffn2_w_gu, ffn2_w_down, ple_norm, w_ple_gate, w_ple_proj, rel_bias, final_norm):
    batch, t, d = x.shape
    depth = p.shape[0]
    assert batch == 1 and t % ROW_TILE == 0 and ROW_TILE % (KV_GROUP * MOBA_BLOCK) == 0
    row = lambda a: a.reshape(1, -1)
    h = x[0]
    for i in range(depth):
        bias, wgu1, wd1, wa, wlr, wm = _prologue(rel_bias, ffn1_w_gu[i], ffn1_w_down[i], w_in[i])
        wgk = jnp.pad(w_gk_up[i], ((0, LANES - GLA_LOWRANK), (0, 0))).astype(BF16)
        h1, qg, kg, vg, gg, la, qt, ka, vt = _ffn_in(
            h, row(ffn1_norm[i]), wgu1, wd1, row(mix_norm[i]), wa, wlr, wm, wgk, row(b_gk[i]))
        og, (wo, wgu2, wd2, wpg) = _gla(qg, kg, la, vg, gg, row(gla_out_norm[i]),
                                        [w_out[i], ffn2_w_gu[i], ffn2_w_down[i], w_ple_gate[i]])
        om, ol = _moba_attn(qt, ka, vt, bias)
        h = _out_ffn(i == depth - 1, h1, og, om, ol, p[i, 0], wo, row(ffn2_norm[i]), wgu2, wd2, row(ple_norm[i]),
                     wpg, w_ple_proj[i].astype(BF16), row(final_norm))
    return h[None]
```

```python
import functools
import math

import jax
import jax.numpy as jnp
import numpy as np
from jax import lax
from jax.experimental import pallas as pl
from jax.experimental.pallas import tpu as pltpu

F32 = jnp.float32
BF16 = jnp.bfloat16

GLA_HEADS = 4
GLA_DK = 64
GLA_DV = 128
GLA_LOWRANK = 16
GLA_TAU = 16.0
MOBA_HEADS = 8
MOBA_HD = 64
MOBA_BLOCK = 256
MOBA_TOPK = 3
REL_BUCKETS = 32
REL_MAX_DIST = 4096
EPS = 1e-6
NEG = -1e30

GLA_QK = GLA_HEADS * GLA_DK
GLA_V = GLA_HEADS * GLA_DV
GLA_COLS = 2 * GLA_QK + 2 * GLA_V
MOBA_W = MOBA_HEADS * MOBA_HD

LANES = 128
BF16_SUBLANES = 16
VMEM_LIMIT = 56 * 1024 * 1024

ROW_TILE = 512
GLA_TILE = 256
GLA_STEP_TILES = 4
GLA_LEVELS = tuple(GLA_TILE >> (i + 1) for i in range(GLA_TILE.bit_length() - 1))
_FAR_DIST = math.ceil((REL_BUCKETS // 2) * (REL_MAX_DIST / (REL_BUCKETS // 2))
                      ** ((REL_BUCKETS // 2 - 1) / (REL_BUCKETS // 2)))
_FAR_BLOCKS = -(-(_FAR_DIST - 1) // MOBA_BLOCK) + 1
BIAS_TILES = _FAR_BLOCKS + 1
KV_GROUP = 2
ATTN_UNROLL = 48
V_ROWS = 80
LOG2E = 1.4426950408889634


def _rms(x, g):
    return x * lax.rsqrt(jnp.mean(x * x, axis=-1, keepdims=True) + EPS) * g


def _ff_chunks(d_ff):
    step = 1024
    return [(s, min(step, d_ff - s)) for s in range(0, d_ff, step)]


def _swiglu(xn, wgu_ref, wd_ref):
    d_ff = wd_ref.shape[0]
    acc = None
    for start, size in _ff_chunks(d_ff):
        gate = jnp.dot(xn, wgu_ref[:, start:start + size], preferred_element_type=F32)
        up = jnp.dot(xn, wgu_ref[:, d_ff + start:d_ff + start + size], preferred_element_type=F32)
        act = (gate * jax.nn.sigmoid(gate) * up).astype(BF16)
        part = jnp.dot(act, wd_ref[start:start + size, :], preferred_element_type=F32)
        acc = part if acc is None else acc + part
    return acc


def _const_spec(shape):
    return pl.BlockSpec(shape, lambda *_: (0,) * len(shape))


def _moba_operands(pair, q, k, v, first_block, km_ref, qt_ref, ka_ref, vt_ref):
    blk = MOBA_BLOCK
    rows = q.shape[0]
    n_blocks = rows // blk
    hd = MOBA_HD
    means = jnp.mean(k.reshape(n_blocks, blk, LANES), axis=1)
    km = km_ref[pair]
    km_row = lax.broadcasted_iota(jnp.int32, km.shape, 0)
    for b in range(n_blocks):
        km = jnp.where(km_row == first_block + b, means[b:b + 1], km)
    km_ref[pair] = km
    km_lane = lax.broadcasted_iota(jnp.int32, km.shape, 1)
    q_t = q.T
    v_t = v.T
    block = lax.broadcasted_iota(jnp.int32, (hd, rows), 0)
    block_f = block.astype(F32)
    own = first_block + lax.broadcasted_iota(jnp.int32, (hd, rows), 1) // blk
    lane = lax.broadcasted_iota(jnp.int32, (rows, LANES), 1)
    own_k = first_block + lax.broadcasted_iota(jnp.int32, (rows, LANES), 0) // blk
    ones_rows = jnp.where(lax.broadcasted_iota(jnp.int32, (V_ROWS - hd, rows), 0) == 0, 1.0, 0.0)
    for a in range(2):
        head = 2 * pair + a
        gate = jnp.dot(jnp.where(km_lane // hd == a, km, 0.0), q_t, precision=lax.Precision.HIGHEST,
                       preferred_element_type=F32)
        gate = jnp.where(block < own, gate, -jnp.inf)
        chosen = jnp.where(block == own, 1.0, 0.0)
        for _ in range(MOBA_TOPK):
            top = jnp.max(gate, axis=0, keepdims=True)
            first = jnp.min(jnp.where(gate == top, block_f, float(hd)), axis=0, keepdims=True)
            hit = block_f == first
            chosen = jnp.where(hit, 1.0, chosen)
            gate = jnp.where(hit, -jnp.inf, gate)
        q_a = q_t[a * hd:(a + 1) * hd] * (hd ** -0.5 * LOG2E)
        q_aug_t = jnp.concatenate([q_a, chosen - 1.0] if a == 0 else [chosen - 1.0, q_a], axis=0).astype(BF16)
        v_aug_t = jnp.concatenate([v_t[a * hd:(a + 1) * hd], ones_rows], axis=0).astype(BF16)
        k_aug = jnp.where(lane // hd == a, k, jnp.where(lane - (1 - a) * hd == own_k, -NEG, 0.0)).astype(BF16)
        for b in range(n_blocks):
            qt_ref[head, b] = q_aug_t[:, b * blk:(b + 1) * blk]
            ka_ref[head, b] = k_aug[b * blk:(b + 1) * blk, :]
        group = KV_GROUP * blk
        for g in range(n_blocks // KV_GROUP):
            vt_ref[head, g] = v_aug_t[:, g * group:(g + 1) * group]


def _ffn_in_kernel(x_ref, n1_ref, wgu_ref, wd_ref, n2_ref, wa_ref, wlr_ref, wm_ref, wgk_ref, bgk_ref,
                   h_ref, qg_ref, kg_ref, vg_ref, gg_ref, la_ref, qt_ref, ka_ref, vt_ref, km_ref):
    @pl.when(pl.program_id(0) == 0)
    def _():
        km_ref[...] = jnp.zeros_like(km_ref)

    x = x_ref[...]
    h = x + 0.5 * _swiglu(_rms(x, n1_ref[...]).astype(BF16), wgu_ref, wd_ref)
    h_ref[...] = h
    n = _rms(h, n2_ref[...]).astype(BF16)
    ua = jnp.dot(n, wa_ref[...], preferred_element_type=F32)
    qg_ref[...] = ua[:, 0:GLA_QK]
    kg_ref[...] = ua[:, GLA_QK:2 * GLA_QK]
    vg_ref[...] = ua[:, 2 * GLA_QK:2 * GLA_QK + GLA_V]
    gg_ref[...] = ua[:, 2 * GLA_QK + GLA_V:]
    lr = jnp.dot(n, wlr_ref[...], preferred_element_type=F32)
    z = jnp.dot(lr.astype(BF16), wgk_ref[...], preferred_element_type=F32) + bgk_ref[...]
    la_ref[...] = (jnp.minimum(z, 0.0) - jnp.log1p(jnp.exp(-jnp.abs(z)))) * (1.0 / GLA_TAU)
    um = jnp.dot(n, wm_ref[...], preferred_element_type=F32)
    first_block = pl.program_id(0) * (x.shape[0] // MOBA_BLOCK)
    for pair in range(MOBA_HEADS // 2):
        q, k, v = (um[:, c * MOBA_W + pair * LANES:c * MOBA_W + (pair + 1) * LANES] for c in range(3))
        _moba_operands(pair, q, k, v, first_block, km_ref, qt_ref, ka_ref, vt_ref)


def _ffn_in(x, n1, wgu, wd, n2, wa, wlr, wm, wgk, bgk):
    t, d = x.shape
    tm = min(ROW_TILE, t)
    blk = MOBA_BLOCK
    nb = t // blk
    assert nb <= MOBA_HD
    step_blocks = tm // blk
    row = lambda w: pl.BlockSpec((tm, w), lambda i: (i, 0))
    outs = [(d, F32), (GLA_QK, F32), (GLA_QK, F32), (GLA_V, F32), (GLA_V, F32), (GLA_QK, F32)]
    heads = lambda shape: pl.BlockSpec((MOBA_HEADS,) + shape, lambda i: (0, i, 0, 0))
    return pl.pallas_call(
        _ffn_in_kernel,
        grid=(t // tm,),
        in_specs=[row(d)] + [_const_spec(a.shape) for a in (n1, wgu, wd, n2, wa, wlr, wm, wgk, bgk)],
        out_specs=[row(w) for w, _ in outs] + [heads((step_blocks, LANES, blk)), heads((step_blocks, blk, LANES)),
                                               heads((step_blocks // KV_GROUP, V_ROWS, KV_GROUP * blk))],
        out_shape=[jax.ShapeDtypeStruct((t, w), dt) for w, dt in outs]
        + [jax.ShapeDtypeStruct((MOBA_HEADS, nb, LANES, blk), BF16),
           jax.ShapeDtypeStruct((MOBA_HEADS, nb, blk, LANES), BF16),
           jax.ShapeDtypeStruct((MOBA_HEADS, nb // KV_GROUP, V_ROWS, KV_GROUP * blk), BF16)],
        scratch_shapes=[pltpu.VMEM((MOBA_HEADS // 2, MOBA_HD, LANES), F32)],
        compiler_params=pltpu.CompilerParams(dimension_semantics=("arbitrary",), vmem_limit_bytes=VMEM_LIMIT),
        name="ffn_in",
    )(x, n1, wgu, wd, n2, wa, wlr, wm, wgk, bgk)


def _gla_span_matrix(tile):
    i = np.arange(tile)[:, None]
    t = np.arange(tile)[None, :]
    mats = [t <= i, t > i]
    for s in GLA_LEVELS:
        split = (i // (2 * s)) * (2 * s) + s - 1
        mats.append(np.where((i & s) != 0, (t > split) & (t <= i), (t > i) & (t <= split)))
    return jnp.asarray(np.concatenate(mats, axis=0), BF16)


def _gla_kernel(n_cast, q_ref, k_ref, la_ref, v_ref, g_ref, gn_ref, span_ref, *refs):
    o_ref = refs[n_cast]
    st_ref = refs[2 * n_cast + 1]
    for w_ref, w_bf16_ref in zip(refs[:n_cast], refs[n_cast + 1:2 * n_cast + 1]):
        w_bf16_ref[...] = w_ref[...].astype(BF16)

    @pl.when(pl.program_id(0) == 0)
    def _():
        st_ref[...] = jnp.zeros_like(st_ref)

    states = [st_ref[h] for h in range(GLA_HEADS)]
    for r0 in range(0, q_ref.shape[0], GLA_TILE):
        states = _gla_tile(r0, states, q_ref, k_ref, la_ref, v_ref, g_ref, gn_ref, span_ref, o_ref)
    for h in range(GLA_HEADS):
        st_ref[h] = states[h]


def _gla_tile(r0, states, q_ref, k_ref, la_ref, v_ref, g_ref, gn_ref, span_ref, o_ref):
    tb = GLA_TILE
    rows = pl.ds(r0, tb)
    half = tb // 2
    la = la_ref[rows, :]
    la_hi = la.astype(BF16)
    la_lo = (la - la_hi.astype(F32)).astype(BF16)

    def span_sum(r):
        m = span_ref[r * tb:(r + 1) * tb, :]
        return jnp.dot(m, la_hi, preferred_element_type=F32) + jnp.dot(m, la_lo, preferred_element_type=F32)

    b = span_sum(0)
    q = q_ref[rows, :] * (GLA_DK ** -0.5)
    k = k_ref[rows, :]
    qe = q * jnp.exp(b)
    ke = k * jnp.exp(span_sum(1))
    a_last = jnp.exp(b[tb - 1:tb, :])
    row = lax.broadcasted_iota(jnp.int32, (tb, GLA_QK), 0)
    q_lvl, k_lvl = [], []
    for idx, s in enumerate(GLA_LEVELS):
        e = jnp.exp(span_sum(2 + idx))
        upper = (row & s) != 0
        q_lvl.append(jnp.where(upper, q * e, 0.0))
        k_lvl.append(jnp.where(upper, 0.0, k * e))
    ri = lax.broadcasted_iota(jnp.int32, (half, 2 * half), 0)
    ci = lax.broadcasted_iota(jnp.int32, (half, 2 * half), 1) % half
    same_block = [ri // (2 * s) == ci // (2 * s) for s in GLA_LEVELS[1:]]
    diagonal = ri == ci
    first_head = lax.broadcasted_iota(jnp.int32, (half, 2 * GLA_DK), 1) < GLA_DK
    gn = gn_ref[...]
    nt = (((1,), (1,)), ((), ()))

    def pair(x, y, r0, c0, lanes):
        keys = y[c0:c0 + half, lanes]
        keys = jnp.concatenate([jnp.where(first_head, keys, 0.0), jnp.where(first_head, 0.0, keys)], axis=0)
        return lax.dot_general(x[r0:r0 + half, lanes].astype(BF16), keys.astype(BF16), nt,
                               preferred_element_type=F32)

    intra = []
    for hp in range(GLA_HEADS // 2):
        lanes = slice(2 * hp * GLA_DK, 2 * (hp + 1) * GLA_DK)
        diag_blocks = []
        for c in (0, half):
            blk = jnp.where(diagonal, pair(q, k, c, c, lanes), 0.0)
            for idx in range(1, len(GLA_LEVELS)):
                blk = blk + jnp.where(same_block[idx - 1], pair(q_lvl[idx], k_lvl[idx], c, c, lanes), 0.0)
            diag_blocks.append(blk)
        low = pair(q_lvl[0], k_lvl[0], half, 0, lanes)
        for a in range(2):
            cols = slice(a * half, (a + 1) * half)
            intra.append(jnp.concatenate(
                [jnp.concatenate([diag_blocks[0][:, cols], jnp.zeros((half, half), F32)], axis=1),
                 jnp.concatenate([low[:, cols], diag_blocks[1][:, cols]], axis=1)], axis=0).astype(BF16))
    new_states = []
    for h in range(GLA_HEADS):
        ks = slice(h * GLA_DK, (h + 1) * GLA_DK)
        vs = slice(h * GLA_DV, (h + 1) * GLA_DV)
        attn = intra[h]
        v = v_ref[rows, vs].astype(BF16)
        st = states[h]
        o = jnp.dot(attn, v, preferred_element_type=F32)
        o = o + lax.dot_general(qe[:, ks].astype(BF16), st.astype(BF16), nt, preferred_element_type=F32)
        new_states.append(st * a_last[:, ks] + lax.dot_general(v, ke[:, ks].astype(BF16), (((0,), (0,)), ((), ())),
                                                               preferred_element_type=F32))
        g = g_ref[rows, vs]
        o_ref[rows, vs] = _rms(o, gn) * (g * jax.nn.sigmoid(g))
    return new_states


def _cast_spec(shape, steps):
    rows = shape[0]
    hold = next(d for d in range(1, steps + 1) if steps % d == 0 and rows * d % (steps * BF16_SUBLANES) == 0)
    return pl.BlockSpec((rows * hold // steps, shape[1]), lambda i: (i // hold, 0))


def _gla(qg, kg, la, vg, gg, gn, weights):
    t = qg.shape[0]
    tb = GLA_TILE * GLA_STEP_TILES
    assert t % tb == 0
    steps = t // tb
    span = _gla_span_matrix(GLA_TILE)
    row = lambda w: pl.BlockSpec((tb, w), lambda i: (i, 0))
    cast_specs = [_cast_spec(w.shape, steps) for w in weights]
    out = pl.pallas_call(
        functools.partial(_gla_kernel, len(weights)),
        grid=(steps,),
        in_specs=[row(GLA_QK), row(GLA_QK), row(GLA_QK), row(GLA_V), row(GLA_V), _const_spec(gn.shape),
                  _const_spec(span.shape)] + cast_specs,
        out_specs=[row(GLA_V)] + cast_specs,
        out_shape=[jax.ShapeDtypeStruct((t, GLA_V), F32)] + [jax.ShapeDtypeStruct(w.shape, BF16) for w in weights],
        scratch_shapes=[pltpu.VMEM((GLA_HEADS, GLA_DV, GLA_DK), F32)],
        compiler_params=pltpu.CompilerParams(dimension_semantics=("arbitrary",), vmem_limit_bytes=VMEM_LIMIT),
        name="gla",
    )(qg, kg, la, vg, gg, gn, span, *weights)
    return out[0], out[1:]


def _prologue_kernel(tab_ref, wgu_ref, wd_ref, win_ref, o_ref, wgu_out, wd_out, wa_out, wlr_out, wm_out):
    wgu_out[...] = wgu_ref[...].astype(BF16)
    wd_out[...] = wd_ref[...].astype(BF16)
    wa_out[...] = win_ref[:GLA_COLS, :].T.astype(BF16)
    wlr_out[...] = win_ref[GLA_COLS:GLA_COLS + LANES, :].T.astype(BF16)
    wm_out[...] = win_ref[GLA_COLS + GLA_LOWRANK:, :].T.astype(BF16)

    h = pl.program_id(0)
    blk = MOBA_BLOCK
    max_exact = REL_BUCKETS // 2
    x = lax.broadcasted_iota(jnp.int32, (8, 2 * blk), 1)
    table = jnp.broadcast_to(tab_ref[pl.ds(h, 1), :], (8, LANES))
    for delta in range(BIAS_TILES):
        dist = delta * blk + x - blk
        n = jnp.maximum(dist, 0)
        nf = jnp.maximum(n, 1).astype(F32)
        large = max_exact + (jnp.log(nf / max_exact) / math.log(REL_MAX_DIST / max_exact)
                             * (REL_BUCKETS - max_exact)).astype(jnp.int32)
        large = jnp.minimum(large, REL_BUCKETS - 1)
        bucket = jnp.where(n < max_exact, n, large)
        bias = jnp.concatenate([jnp.take_along_axis(table, bucket[:, c:c + LANES], axis=1)
                                for c in range(0, 2 * blk, LANES)], axis=1)
        by_dist = jnp.where(dist >= 0, bias * LOG2E, NEG)
        rows = jnp.broadcast_to(by_dist[0:1], (blk, 2 * blk))
        o_ref[0, delta] = pltpu.roll(rows, 0, 1, stride=1, stride_axis=0)[:, blk:].astype(o_ref.dtype)


def _prologue(rel_bias, wgu, wd, w_in):
    blk = MOBA_BLOCK
    steps = MOBA_HEADS
    table = jnp.pad(rel_bias.T, ((0, 0), (0, LANES - REL_BUCKETS)))
    d, width = w_in.shape
    moba_cols = width - GLA_COLS - GLA_LOWRANK
    slab = _cast_spec(w_in.shape, steps).block_shape[0]
    assert slab * steps == d and slab % LANES == 0
    w_slab = lambda cols: pl.BlockSpec((slab, cols), lambda i: (i, 0))
    w_in_t = w_in.T
    return pl.pallas_call(
        _prologue_kernel,
        grid=(steps,),
        in_specs=[_const_spec(table.shape), _cast_spec(wgu.shape, steps), _cast_spec(wd.shape, steps),
                  pl.BlockSpec((width, slab), lambda i: (0, i))],
        out_specs=[pl.BlockSpec((1, BIAS_TILES, blk, blk), lambda h: (h, 0, 0, 0)), _cast_spec(wgu.shape, steps),
                   _cast_spec(wd.shape, steps), w_slab(GLA_COLS), w_slab(LANES), w_slab(moba_cols)],
        out_shape=[jax.ShapeDtypeStruct((MOBA_HEADS, BIAS_TILES, blk, blk), BF16),
                   jax.ShapeDtypeStruct(wgu.shape, BF16), jax.ShapeDtypeStruct(wd.shape, BF16),
                   jax.ShapeDtypeStruct((d, GLA_COLS), BF16), jax.ShapeDtypeStruct((d, LANES), BF16),
                   jax.ShapeDtypeStruct((d, moba_cols), BF16)],
        compiler_params=pltpu.CompilerParams(dimension_semantics=("parallel",), vmem_limit_bytes=VMEM_LIMIT),
        name="prologue",
    )(table, wgu, wd, w_in_t)


def _moba_attn_kernel(qi_ref, gi_ref, qt_ref, ka_ref, vt_ref, bias_ref, o_ref, l_ref, s0_ref, s1_ref, p0_ref,
                      p1_ref, acc_ref):
    n_pairs = qi_ref.shape[0] - 2
    blk = MOBA_BLOCK
    s_refs = (s0_ref, s1_ref)
    p_refs = (p0_ref, p1_ref)

    def scores(t, slot):
        s_ref = s_refs[slot]
        qi = qi_ref[t]
        gi = gi_ref[t]
        qt = qt_ref[0, qi]
        col_max = None
        for b in range(KV_GROUP):
            j = gi * KV_GROUP + b
            s = jnp.dot(ka_ref[0, j], qt, preferred_element_type=F32).astype(BF16)
            s = s + bias_ref[0, jnp.clip(qi - j, 0, BIAS_TILES - 1)]
            s_ref[b] = s
            cm = jnp.max(s.reshape(blk // BF16_SUBLANES, BF16_SUBLANES, blk), axis=0)
            col_max = cm if col_max is None else jnp.maximum(col_max, cm)
        return jnp.max(col_max, axis=0, keepdims=True).astype(F32)

    def softmax(t, slot, m_run, col_max):
        m_prev = jnp.where(gi_ref[t] == 0, -jnp.inf, m_run)
        m = jnp.maximum(m_prev, col_max)
        for b in range(KV_GROUP):
            p_refs[slot][b * blk:(b + 1) * blk, :] = jnp.exp2(s_refs[slot][b] - m.astype(BF16))
        return m, jnp.exp2(m_prev - m)

    def values(t, slot, alpha):
        acc = acc_ref[...] * alpha + jnp.dot(vt_ref[0, gi_ref[t]], p_refs[slot][...], preferred_element_type=F32)
        acc_ref[...] = acc
        o_ref[0, qi_ref[t]] = acc[0:MOBA_HD].astype(o_ref.dtype)
        l_ref[0, qi_ref[t]] = acc[MOBA_HD:MOBA_HD + 8]

    acc_ref[...] = jnp.zeros_like(acc_ref)
    cm0 = scores(0, 0)
    cm1 = scores(1, 1)
    m0, alpha0 = softmax(0, 0, jnp.full((1, blk), -jnp.inf, F32), cm0)

    unroll = max(u for u in range(2, ATTN_UNROLL + 1, 2) if n_pairs % u == 0)

    def step(u, carry):
        m, alpha, cm = carry
        for i in range(unroll):
            t = unroll * u + i
            cm_next = scores(t + 2, i % 2)
            m, alpha_next = softmax(t + 1, 1 - i % 2, m, cm)
            values(t, i % 2, alpha)
            alpha, cm = alpha_next, cm_next
        return m, alpha, cm

    lax.fori_loop(0, n_pairs // unroll, step, (m0, alpha0, cm1))


def _moba_attn(qt, ka, vt, bias):
    nb = qt.shape[1]
    blk = MOBA_BLOCK
    pairs = [(qi, g) for qi in range(nb) for g in range(qi // KV_GROUP + 1)] + [(0, 0), (0, 0)]
    qi_tab = jnp.asarray([a for a, _ in pairs], jnp.int32)
    gi_tab = jnp.asarray([b for _, b in pairs], jnp.int32)
    head = lambda shape: pl.BlockSpec((1,) + shape, lambda h, *_: (h,) + (0,) * len(shape))
    return pl.pallas_call(
        _moba_attn_kernel,
        grid_spec=pltpu.PrefetchScalarGridSpec(
            num_scalar_prefetch=2,
            grid=(MOBA_HEADS,),
            in_specs=[head((nb, LANES, blk)), head((nb, blk, LANES)),
                      head((nb // KV_GROUP, V_ROWS, KV_GROUP * blk)), head((BIAS_TILES, blk, blk))],
            out_specs=[head((nb, MOBA_HD, blk)), head((nb, 8, blk))],
            scratch_shapes=[pltpu.VMEM((KV_GROUP, blk, blk), BF16), pltpu.VMEM((KV_GROUP, blk, blk), BF16),
                            pltpu.VMEM((KV_GROUP * blk, blk), BF16), pltpu.VMEM((KV_GROUP * blk, blk), BF16),
                            pltpu.VMEM((V_ROWS, blk), F32)]),
        out_shape=[jax.ShapeDtypeStruct((MOBA_HEADS, nb, MOBA_HD, blk), BF16),
                   jax.ShapeDtypeStruct((MOBA_HEADS, nb, 8, blk), F32)],
        compiler_params=pltpu.CompilerParams(dimension_semantics=("parallel",), vmem_limit_bytes=VMEM_LIMIT),
        name="moba_attn",
    )(qi_tab, gi_tab, qt, ka, vt, bias)


def _out_ffn_kernel(final, h_ref, og_ref, om_ref, ol_ref, p_ref, wo_ref, n3_ref, wgu_ref, wd_ref,
                    n4_ref, wpg_ref, wpp_ref, nf_ref, o_ref):
    mix = jnp.dot(og_ref[...].astype(BF16), wo_ref[0:GLA_V, :], preferred_element_type=F32)
    om = [(om_ref[:, b].astype(F32) / ol_ref[:, b, 0:1, :]).reshape(MOBA_W, MOBA_BLOCK).T.astype(BF16)
          for b in range(om_ref.shape[1])]
    mix = mix + jnp.dot(jnp.concatenate(om, axis=0), wo_ref[GLA_V:, :], preferred_element_type=F32)
    h = h_ref[...] + mix
    h = h + 0.5 * _swiglu(_rms(h, n3_ref[...]).astype(BF16), wgu_ref, wd_ref)
    gate = jax.nn.sigmoid(jnp.dot(_rms(h, n4_ref[...]).astype(BF16), wpg_ref[...], preferred_element_type=F32))
    h = h + gate * jnp.dot(p_ref[...].astype(BF16), wpp_ref[...], preferred_element_type=F32)
    o_ref[...] = _rms(h, nf_ref[...]) if final else h


def _out_ffn(final, h, og, om, ol, p, wo, n3, wgu, wd, n4, wpg, wpp, nf):
    t, d = h.shape
    tm = min(ROW_TILE, t)
    row = lambda w: pl.BlockSpec((tm, w), lambda i: (i, 0))
    consts = (wo, n3, wgu, wd, n4, wpg, wpp, nf)
    return pl.pallas_call(
        functools.partial(_out_ffn_kernel, final),
        grid=(t // tm,),
        in_specs=[row(d), row(GLA_V),
                  pl.BlockSpec((MOBA_HEADS, tm // MOBA_BLOCK, MOBA_HD, MOBA_BLOCK), lambda i: (0, i, 0, 0)),
                  pl.BlockSpec((MOBA_HEADS, tm // MOBA_BLOCK, 8, MOBA_BLOCK), lambda i: (0, i, 0, 0)),
                  row(p.shape[1])]
        + [_const_spec(a.shape) for a in consts],
        out_specs=row(d),
        out_shape=jax.ShapeDtypeStruct((t, d), F32),
        compiler_params=pltpu.CompilerParams(dimension_semantics=("parallel",), vmem_limit_bytes=VMEM_LIMIT),
        name="out_ffn",
    )(h, og, om, ol, p, *consts)


def kernel(x, p, ffn1_norm, ffn1_w_gu, ffn1_w_down, mix_norm, w_in, w_gk_up, b_gk, gla_out_norm, w_out,
           ffn2_norm, ffn2_w_gu, ffn2_w_down, ple_norm, w_ple_gate, w_ple_proj, rel_bias, final_norm):
    batch, t, d = x.shape
    depth = p.shape[0]
    assert batch == 1 and t % ROW_TILE == 0 and ROW_TILE % (KV_GROUP * MOBA_BLOCK) == 0
    row = lambda a: a.reshape(1, -1)
    h = x[0]
    for i in range(depth):
        bias, wgu1, wd1, wa, wlr, wm = _prologue(rel_bias, ffn1_w_gu[i], ffn1_w_down[i], w_in[i])
        wgk = jnp.pad(w_gk_up[i], ((0, LANES - GLA_LOWRANK), (0, 0))).astype(BF16)
        h1, qg, kg, vg, gg, la, qt, ka, vt = _ffn_in(
            h, row(ffn1_norm[i]), wgu1, wd1, row(mix_norm[i]), wa, wlr, wm, wgk, row(b_gk[i]))
        og, (wo, wgu2, wd2, wpg) = _gla(qg, kg, la, vg, gg, row(gla_out_norm[i]),
                                        [w_out[i], ffn2_w_gu[i], ffn2_w_down[i], w_ple_gate[i]])
        om, ol = _moba_attn(qt, ka, vt, bias)
        h = _out_ffn(i == depth - 1, h1, og, om, ol, p[i, 0], wo, row(ffn2_norm[i]), wgu2, wd2, row(ple_norm[i]),
                     wpg, w_ple_proj[i].astype(BF16), row(final_norm))
    return h[None]
```
